```python
import jax
import jax.numpy as jnp
from jax import lax
import numpy as np

D_MODEL = 2048
BATCH = 2
SEQ = 4096
DEPTH = 4
DEC_BATCH = 32
DEC_SEQ = 4
PAST_LEN = 16384
PAGE_SIZE = 128

HEAD_DIM = 64
D_ATTN = D_MODEL // 2
N_Q_HEADS = D_ATTN // HEAD_DIM
N_KV_HEADS = 2
Q_PER_KV = N_Q_HEADS // N_KV_HEADS
D_KV = N_KV_HEADS * HEAD_DIM
WINDOW = 128
ATTN_BLOCK = 128
ROPE_THETA = 10000.0

POOL_WINDOWS = (2, 4, 8, 16)
N_POOL_GROUPS = len(POOL_WINDOWS)
D_POOL = D_MODEL // 4
POOL_GROUP_DIM = D_POOL // N_POOL_GROUPS
POOL_STATE = max(POOL_WINDOWS) - 1

CHUNK = 128
N_SG_GROUPS = 4
D_SG = D_MODEL // 4
SG_GROUP_DIM = D_SG // N_SG_GROUPS

D_MIX = D_ATTN + D_POOL + D_SG
D_IN = D_ATTN + 2 * D_KV + D_POOL + 2 * D_SG
_SPLITS = (D_ATTN, D_ATTN + D_KV, D_ATTN + 2 * D_KV, D_ATTN + 2 * D_KV + D_POOL,
           D_ATTN + 2 * D_KV + D_POOL + D_SG)

DENSE_FF = 11 * D_MODEL // 4
N_EXPERTS = 8
TOP_K = 2
EXPERT_FF = D_MODEL
N_DENSE = (DEPTH + 1) // 2
N_MOE = DEPTH // 2

DN_ALPHA = (2 * DEPTH) ** 0.25
DN_BETA = (8 * DEPTH) ** -0.25
LN_EPS = 1e-5

kernel_name = "hymba_style_swa_pool_sgu_step"


def _layer_norm(x, g, b):
    xf = x.astype(jnp.float32)
    mu = jnp.mean(xf, axis=-1, keepdims=True)
    var = jnp.mean(jnp.square(xf - mu), axis=-1, keepdims=True)
    return ((xf - mu) * lax.rsqrt(var + LN_EPS) * g.astype(jnp.float32) + b.astype(jnp.float32)).astype(x.dtype)


def _rope(x, pos):
    half = HEAD_DIM // 2
    inv = ROPE_THETA ** (-jnp.arange(half, dtype=jnp.float32) / half)
    ang = pos.astype(jnp.float32)[:, None] * inv[None, :]
    cos = jnp.cos(ang)[:, None, :]
    sin = jnp.sin(ang)[:, None, :]
    xf = x.astype(jnp.float32)
    x1, x2 = xf[..., :half], xf[..., half:]
    return jnp.concatenate([x1 * cos - x2 * sin, x2 * cos + x1 * sin], axis=-1).astype(x.dtype)


def _project(h, w_in_l, pos, sg_g, sg_b):
    b, t, _ = h.shape
    z = jnp.einsum('btd,de->bte', h, w_in_l)
    zq, zk, zv, xpool, zu, zg = jnp.split(z, list(_SPLITS), axis=-1)
    q = _rope(zq.reshape(b, t, N_Q_HEADS, HEAD_DIM), pos)
    k = _rope(zk.reshape(b, t, N_KV_HEADS, HEAD_DIM), pos)
    v = zv.reshape(b, t, N_KV_HEADS, HEAD_DIM)
    u = jax.nn.gelu(zu)
    vg = jax.nn.gelu(zg).reshape(b, t, N_SG_GROUPS, SG_GROUP_DIM)
    vn = _layer_norm(vg, sg_g.reshape(N_SG_GROUPS, SG_GROUP_DIM), sg_b.reshape(N_SG_GROUPS, SG_GROUP_DIM))
    return q, k, v, xpool, u, vn


def _sink_attention(q, k, v, q_pos, k_pos, sinks):
    s = jnp.einsum('bnqkgd,bnskd->bnkgqs', q, k, preferred_element_type=jnp.float32) * (HEAD_DIM ** -0.5)
    rel = q_pos[:, :, None] - k_pos[:, None, :]
    allowed = (rel >= 0) & (rel <= WINDOW) & (k_pos[:, None, :] >= 0)
    s = jnp.where(allowed[None, :, None, None], s, -jnp.inf)
    sink = sinks.astype(jnp.float32)[None, None, :, :, None, None]
    m = jnp.maximum(jnp.max(s, axis=-1, keepdims=True), sink)
    p = jnp.exp(s - m)
    denom = jnp.sum(p, axis=-1, keepdims=True) + jnp.exp(sink - m)
    return jnp.einsum('bnkgqs,bnskd->bnqkgd', (p / denom).astype(v.dtype), v)


def _attn_prompt(q, k, v, sinks):
    b, t = q.shape[:2]
    nb = t // ATTN_BLOCK
    qb = q.reshape(b, nb, ATTN_BLOCK, N_KV_HEADS, Q_PER_KV, HEAD_DIM)
    kb = k.reshape(b, nb, ATTN_BLOCK, N_KV_HEADS, HEAD_DIM)
    vb = v.reshape(b, nb, ATTN_BLOCK, N_KV_HEADS, HEAD_DIM)
    shift = ((0, 0), (1, 0), (0, 0), (0, 0), (0, 0))
    kk = jnp.concatenate([jnp.pad(kb, shift)[:, :-1], kb], axis=2)
    vv = jnp.concatenate([jnp.pad(vb, shift)[:, :-1], vb], axis=2)
    pos = jnp.arange(t, dtype=jnp.int32).reshape(nb, ATTN_BLOCK)
    k_pos = jnp.concatenate([pos - ATTN_BLOCK, pos], axis=1)
    o = _sink_attention(qb, kk, vv, pos, k_pos, sinks)
    return o.reshape(b, t, D_ATTN)


def _attn_sample(q, k, v, k_buf, v_buf, sinks, pos0):
    b, t = q.shape[:2]
    L = k_buf.shape[1]
    kk = jnp.concatenate([k_buf, k], axis=1)
    vv = jnp.concatenate([v_buf, v], axis=1)
    q_pos = (pos0 + jnp.arange(t, dtype=jnp.int32))[None]
    k_pos = (pos0 - L + jnp.arange(L + t, dtype=jnp.int32))[None]
    o = _sink_attention(q.reshape(b, 1, t, N_KV_HEADS, Q_PER_KV, HEAD_DIM), kk[:, None], vv[:, None],
                        q_pos, k_pos, sinks)
    new_k = kk[:, -L:]
    new_v = vv[:, -L:]
    return o.reshape(b, t, D_ATTN), new_k, new_v


def _pool_mix(xpool, prefix, pos0, w_pool, scale):
    b, t, _ = xpool.shape
    full = jnp.concatenate([prefix, xpool], axis=1)
    fullf = full.astype(jnp.float32)
    cs = jnp.cumsum(jnp.pad(fullf, ((0, 0), (1, 0), (0, 0))), axis=1)
    end = cs[:, POOL_STATE + 1:]
    pos = pos0 + jnp.arange(t, dtype=jnp.int32)
    outs = []
    for g, w in enumerate(POOL_WINDOWS):
        sl = slice(g * POOL_GROUP_DIM, (g + 1) * POOL_GROUP_DIM)
        start = cs[:, POOL_STATE + 1 - w: POOL_STATE + 1 - w + t, sl]
        cnt = jnp.minimum(pos + 1, w).astype(jnp.float32)[None, :, None]
        outs.append((end[..., sl] - start) / cnt)
    pooled = jnp.concatenate(outs, axis=-1)
    d = (pooled - xpool.astype(jnp.float32)).astype(xpool.dtype).reshape(b, t, N_POOL_GROUPS, POOL_GROUP_DIM)
    y = jnp.einsum('btgc,gce->btge', d, w_pool).reshape(b, t, D_POOL) * scale
    return y, full[:, -POOL_STATE:]


def _spatial_gate(u, vn, w_s, b_s):
    L = vn.shape[2]
    causal = jnp.tril(jnp.ones((L, L), dtype=bool))
    w = jnp.where(causal[None], w_s[:, :L, :L], 0).astype(vn.dtype)
    s = jnp.einsum('gij,bnjgc->bnigc', w, vn) + b_s[:, :L].T[None, None, :, :, None]
    return u * s


def _swiglu(h, wg, wu, wd):
    a = jnp.einsum('btd,df->btf', h, wg)
    c = jnp.einsum('btd,df->btf', h, wu)
    return jnp.einsum('btf,fd->btd', jax.nn.silu(a) * c, wd)


def _moe(h, w_router, b_router, wg, wu, wd):
    logits = (jnp.einsum('btd,de->bte', h, w_router) + b_router).astype(jnp.float32)
    top_val, top_idx = lax.top_k(logits, TOP_K)
    gates = jax.nn.softmax(top_val, axis=-1)
    combine = jnp.sum(jax.nn.one_hot(top_idx, N_EXPERTS, dtype=jnp.float32) * gates[..., None], axis=-2)
    a = jnp.einsum('btd,edf->btef', h, wg)
    c = jnp.einsum('btd,edf->btef', h, wu)
    act = jax.nn.silu(a) * c * combine.astype(h.dtype)[..., None]
    return jnp.einsum('btef,efd->btd', act, wd)


def _post(x, mixed, l, w_out, ln1_g, ln1_b, ln2_g, ln2_b, ffn_w_gate, ffn_w_up, ffn_w_down,
          router_w, router_b, moe_w_gate, moe_w_up, moe_w_down):
    m = jnp.einsum('bte,ed->btd', mixed, w_out[l])
    x = _layer_norm(DN_ALPHA * x + m, ln1_g[l], ln1_b[l])
    i = l // 2
    if l % 2 == 0:
        f = _swiglu(x, ffn_w_gate[i], ffn_w_up[i], ffn_w_down[i])
    else:
        f = _moe(x, router_w[i], router_b[i], moe_w_gate[i], moe_w_up[i], moe_w_down[i])
    return _layer_norm(DN_ALPHA * x + f, ln2_g[l], ln2_b[l])


def setup_inputs(seed: int = 0) -> dict:
    key = jax.random.key(seed)
    ks = jax.random.split(key, 32)
    f32 = jnp.float32
    nrm = lambda k, shape, scale: jax.random.normal(k, shape, f32) * scale
    win_buf = min(WINDOW, PAST_LEN)
    return {
        'x_prompt': nrm(ks[0], (BATCH, SEQ, D_MODEL), 1.0),
        'x_sample': nrm(ks[1], (DEC_BATCH, DEC_SEQ, D_MODEL), 1.0),
        'cache_k': nrm(ks[2], (DEPTH, DEC_BATCH, win_buf, N_KV_HEADS, HEAD_DIM), 1.0),
        'cache_v': nrm(ks[3], (DEPTH, DEC_BATCH, win_buf, N_KV_HEADS, HEAD_DIM), 1.0),
        'state_pool': nrm(ks[4], (DEPTH, DEC_BATCH, POOL_STATE, D_POOL), 1.0),
        'w_in': nrm(ks[5], (DEPTH, D_MODEL, D_IN), D_MODEL ** -0.5),
        'w_out': nrm(ks[6], (DEPTH, D_MIX, D_MODEL), DN_BETA * D_MIX ** -0.5),
        'attn_sinks': nrm(ks[7], (DEPTH, N_Q_HEADS), 1.0),
        'pool_w': nrm(ks[8], (DEPTH, N_POOL_GROUPS, POOL_GROUP_DIM, POOL_GROUP_DIM), POOL_GROUP_DIM ** -0.5),
        'pool_scale': 1.0 + nrm(ks[9], (DEPTH, D_POOL), 0.1),
        'sg_w': nrm(ks[10], (DEPTH, N_SG_GROUPS, CHUNK, CHUNK), CHUNK ** -0.5),
        'sg_b': 1.0 + nrm(ks[11], (DEPTH, N_SG_GROUPS, CHUNK), 0.1),
        'sg_norm_g': 1.0 + nrm(ks[12], (DEPTH, D_SG), 0.1),
        'sg_norm_b': nrm(ks[13], (DEPTH, D_SG), 0.02),
        'ln1_g': 1.0 + nrm(ks[14], (DEPTH, D_MODEL), 0.1),
        'ln1_b': nrm(ks[15], (DEPTH, D_MODEL), 0.02),
        'ln2_g': 1.0 + nrm(ks[16], (DEPTH, D_MODEL), 0.1),
        'ln2_b': nrm(ks[17], (DEPTH, D_MODEL), 0.02),
        'ffn_w_gate': nrm(ks[18], (N_DENSE, D_MODEL, DENSE_FF), D_MODEL ** -0.5),
        'ffn_w_up': nrm(ks[19], (N_DENSE, D_MODEL, DENSE_FF), D_MODEL ** -0.5),
        'ffn_w_down': nrm(ks[20], (N_DENSE, DENSE_FF, D_MODEL), DN_BETA * DENSE_FF ** -0.5),
        'router_w': nrm(ks[21], (N_MOE, D_MODEL, N_EXPERTS), D_MODEL ** -0.5),
        'router_b': nrm(ks[22], (N_MOE, N_EXPERTS), 0.01),
        'moe_w_gate': nrm(ks[23], (N_MOE, N_EXPERTS, D_MODEL, EXPERT_FF), D_MODEL ** -0.5),
        'moe_w_up': nrm(ks[24], (N_MOE, N_EXPERTS, D_MODEL, EXPERT_FF), D_MODEL ** -0.5),
        'moe_w_down': nrm(ks[25], (N_MOE, N_EXPERTS, EXPERT_FF, D_MODEL), DN_BETA * EXPERT_FF ** -0.5),
    }


def reference(x_prompt, x_sample, cache_k, cache_v, state_pool, w_in, w_out, attn_sinks, pool_w,
              pool_scale, sg_w, sg_b, sg_norm_g, sg_norm_b, ln1_g, ln1_b, ln2_g, ln2_b,
              ffn_w_gate, ffn_w_up, ffn_w_down, router_w, router_b, moe_w_gate, moe_w_up, moe_w_down):
    bp, tp = x_prompt.shape[:2]
    bs, ts = x_sample.shape[:2]
    pos_p = jnp.arange(tp, dtype=jnp.int32)
    pos_s = PAST_LEN + jnp.arange(ts, dtype=jnp.int32)
    xp, xs = x_prompt, x_sample
    kp_l, vp_l, pp_l, ks_l, vs_l, ps_l, sg_l = [], [], [], [], [], [], []
    for l in range(DEPTH):
        sinks = attn_sinks[l].reshape(N_KV_HEADS, Q_PER_KV)
        ffn_args = (w_out, ln1_g, ln1_b, ln2_g, ln2_b, ffn_w_gate, ffn_w_up, ffn_w_down,
                    router_w, router_b, moe_w_gate, moe_w_up, moe_w_down)

        q, k, v, xpool, u, vn = _project(xp, w_in[l], pos_p, sg_norm_g[l], sg_norm_b[l])
        a = _attn_prompt(q, k, v, sinks)
        prefix = jnp.zeros((bp, POOL_STATE, D_POOL), xpool.dtype)
        p, pool_tail = _pool_mix(xpool, prefix, 0, pool_w[l], pool_scale[l])
        nc = tp // CHUNK
        g = _spatial_gate(u.reshape(bp, nc, CHUNK, N_SG_GROUPS, SG_GROUP_DIM),
                          vn.reshape(bp, nc, CHUNK, N_SG_GROUPS, SG_GROUP_DIM), sg_w[l], sg_b[l])
        mixed = jnp.concatenate([a, p, g.reshape(bp, tp, D_SG)], axis=-1)
        xp = _post(xp, mixed, l, *ffn_args)
        kp_l.append(k[:, -min(WINDOW, tp):])
        vp_l.append(v[:, -min(WINDOW, tp):])
        pp_l.append(pool_tail)

        q, k, v, xpool, u, vn = _project(xs, w_in[l], pos_s, sg_norm_g[l], sg_norm_b[l])
        a, nk, nv = _attn_sample(q, k, v, cache_k[l], cache_v[l], sinks, PAST_LEN)
        p, pool_tail = _pool_mix(xpool, state_pool[l], PAST_LEN, pool_w[l], pool_scale[l])
        g = _spatial_gate(u.reshape(bs, 1, ts, N_SG_GROUPS, SG_GROUP_DIM),
                          vn.reshape(bs, 1, ts, N_SG_GROUPS, SG_GROUP_DIM), sg_w[l], sg_b[l])
        mixed = jnp.concatenate([a, p, g.reshape(bs, ts, D_SG)], axis=-1)
        xs = _post(xs, mixed, l, *ffn_args)
        ks_l.append(nk)
        vs_l.append(nv)
        ps_l.append(pool_tail)
        sg_l.append(vn.reshape(bs, ts, D_SG))

    new_cache_k_prompt = jnp.stack(kp_l, axis=0)
    new_cache_v_prompt = jnp.stack(vp_l, axis=0)
    new_state_pool_prompt = jnp.stack(pp_l, axis=0)
    new_cache_k_sample = jnp.stack(ks_l, axis=0)
    new_cache_v_sample = jnp.stack(vs_l, axis=0)
    new_state_pool_sample = jnp.stack(ps_l, axis=0)
    new_state_sg_v_sample = jnp.stack(sg_l, axis=0)
    return (xp, xs, new_cache_k_prompt, new_cache_v_prompt, new_state_pool_prompt,
            new_cache_k_sample, new_cache_v_sample, new_state_pool_sample, new_state_sg_v_sample)
```

```python
import functools

import jax
import jax.numpy as jnp
from jax import lax
from jax.experimental import pallas as pl
from jax.experimental.pallas import tpu as pltpu

F32 = jnp.float32
BF16 = jnp.bfloat16

PAST_LEN = 16384
HEAD_DIM = 64
N_KV_HEADS = 2
WINDOW = 128
BLK = 128
ROPE_THETA = 10000.0
POOL_WINDOWS = (2, 4, 8, 16)
POOL_STATE = 15
N_SG_GROUPS = 4
N_EXPERTS = 8
LN_EPS = 1e-5

LANES = 128
SUBLANES = 8
SAMPLE_PAD = SUBLANES
VMEM_LIMIT = 56 * 1024 * 1024

NEG_INF = float("-inf")


def _cparams(n_axes):
    return pltpu.CompilerParams(
        dimension_semantics=("arbitrary",) * n_axes, vmem_limit_bytes=VMEM_LIMIT)


def _layer_norm(x, g, b):
    mu = jnp.mean(x, axis=-1, keepdims=True)
    xc = x - mu
    var = jnp.mean(xc * xc, axis=-1, keepdims=True)
    return xc * lax.rsqrt(var + LN_EPS) * g + b


def _gelu_tanh(x):
    return 0.5 * x * (1.0 + jnp.tanh(0.7978845608028654 * (x + 0.044715 * (x * x * x))))


def _inproj_kernel(x_ref, w_ref, cos_ref, sin_ref, g_ref, b_ref,
                   q_ref, k_ref, v_ref, xp_ref, u_ref, vn_ref, *, d_attn, d_kv, d_pool, d_sg):
    xb = x_ref[...].astype(BF16)
    cos = cos_ref[...]
    sin = sin_ref[...]
    tm = xb.shape[0]
    lane = lax.broadcasted_iota(jnp.int32, (tm, LANES), 1)
    first_half = (lane % HEAD_DIM) < (HEAD_DIM // 2)

    def rope(z):
        rot = jnp.where(first_half, pltpu.roll(z, LANES - HEAD_DIM // 2, 1),
                        pltpu.roll(z, HEAD_DIM // 2, 1))
        return z * cos + rot * sin

    c0 = 0
    zq = jnp.dot(xb, w_ref[:, c0:c0 + d_attn], preferred_element_type=F32)
    for j in range(d_attn // LANES):
        sl = slice(j * LANES, (j + 1) * LANES)
        q_ref[:, sl] = (rope(zq[:, sl]) * (HEAD_DIM ** -0.5)).astype(BF16)
    c0 += d_attn
    zkv = jnp.dot(xb, w_ref[:, c0:c0 + 2 * d_kv], preferred_element_type=F32)
    k_ref[...] = rope(zkv[:, :d_kv])
    v_ref[...] = zkv[:, d_kv:]
    c0 += 2 * d_kv
    xp_ref[...] = jnp.dot(xb, w_ref[:, c0:c0 + d_pool], preferred_element_type=F32)
    c0 += d_pool
    u_ref[...] = _gelu_tanh(jnp.dot(xb, w_ref[:, c0:c0 + d_sg], preferred_element_type=F32))
    c0 += d_sg
    vg = _gelu_tanh(jnp.dot(xb, w_ref[:, c0:c0 + d_sg], preferred_element_type=F32))
    gd = d_sg // N_SG_GROUPS
    for g in range(N_SG_GROUPS):
        sl = slice(g * gd, (g + 1) * gd)
        vn_ref[:, sl] = _layer_norm(vg[:, sl], g_ref[:, sl], b_ref[:, sl])


def _inproj(x, w, cos, sin, sg_g, sg_b, *, tm, d_attn, d_kv, d_pool, d_sg):
    r, d = x.shape
    d_in = w.shape[1]
    row = lambda i: (i, 0)
    const = lambda i: (0, 0)
    outs = (
        jax.ShapeDtypeStruct((r, d_attn), BF16),
        jax.ShapeDtypeStruct((r, d_kv), F32),
        jax.ShapeDtypeStruct((r, d_kv), F32),
        jax.ShapeDtypeStruct((r, d_pool), F32),
        jax.ShapeDtypeStruct((r, d_sg), F32),
        jax.ShapeDtypeStruct((r, d_sg), F32),
    )
    return pl.pallas_call(
        functools.partial(_inproj_kernel, d_attn=d_attn, d_kv=d_kv, d_pool=d_pool, d_sg=d_sg),
        out_shape=outs,
        grid=(r // tm,),
        in_specs=[
            pl.BlockSpec((tm, d), row),
            pl.BlockSpec((d, d_in), const, pipeline_mode=pl.Buffered(1)),
            pl.BlockSpec((tm, LANES), row),
            pl.BlockSpec((tm, LANES), row),
            pl.BlockSpec((1, d_sg), const),
            pl.BlockSpec((1, d_sg), const),
        ],
        out_specs=(
            pl.BlockSpec((tm, d_attn), row),
            pl.BlockSpec((tm, d_kv), row),
            pl.BlockSpec((tm, d_kv), row),
            pl.BlockSpec((tm, d_pool), row),
            pl.BlockSpec((tm, d_sg), row),
            pl.BlockSpec((tm, d_sg), row),
        ),
        compiler_params=_cparams(1),
        name="inproj",
    )(x, w, cos, sin, sg_g, sg_b)


def _split_kv(kv):
    lane = lax.broadcasted_iota(jnp.int32, kv.shape, 1)
    low = lane < HEAD_DIM
    a = kv.astype(BF16)
    b = pltpu.roll(kv, HEAD_DIM, 1).astype(BF16)
    zero = jnp.zeros_like(a)
    head0 = (jnp.where(low, a, zero), jnp.where(low, zero, b))
    head1 = (jnp.where(low, b, zero), jnp.where(low, zero, a))
    return head0, head1


def _attend(q, k_pair, v_pair, allowed, sinks):
    out = None
    for kx, vx, sk in zip(k_pair, v_pair, sinks):
        s = lax.dot_general(q, kx, (((1,), (1,)), ((), ())), preferred_element_type=F32)
        s = jnp.where(allowed, s, NEG_INF)
        m = jnp.maximum(jnp.max(s, axis=1, keepdims=True), sk)
        p = jnp.exp(s - m)
        denom = jnp.sum(p, axis=1, keepdims=True) + jnp.exp(sk - m)
        o = jnp.dot(p.astype(BF16), vx, preferred_element_type=F32) * (1.0 / denom)
        out = o if out is None else out + o
    return out


def _window_sums(full):
    sums = {1: full}
    w = 1
    while w < max(POOL_WINDOWS):
        sums[2 * w] = sums[w] + pltpu.roll(sums[w], w, 0)
        w *= 2
    return sums


def _pool_mix(full, cur, cnt_of, pw_ref, ps_ref, row0):
    n = cur.shape[0]
    gd = cur.shape[1] // len(POOL_WINDOWS)
    outs = []
    for g, w in enumerate(POOL_WINDOWS):
        sl = slice(g * gd, (g + 1) * gd)
        sums = _window_sums(full[:, sl])[w]
        pooled = sums[row0:row0 + n, :] / cnt_of(w)
        d = (pooled - cur[:, sl]).astype(BF16)
        outs.append(jnp.dot(d, pw_ref[g], preferred_element_type=F32) * ps_ref[:, sl])
    return outs


def _causal_weights(sw_ref, g):
    n = sw_ref.shape[1]
    ri = lax.broadcasted_iota(jnp.int32, (n, n), 0)
    ci = lax.broadcasted_iota(jnp.int32, (n, n), 1)
    return jnp.where(ri >= ci, sw_ref[g], 0.0).astype(BF16)


def _mix_prompt_kernel(sink_ref, q_ref, kc_ref, kp_ref, vc_ref, vp_ref, xc_ref, xt_ref, u_ref,
                       vn_ref, pw_ref, ps_ref, sw_ref, sb_ref, o_ref, *, d_attn, d_pool, d_sg):
    i = pl.program_id(1)
    has_prev = i > 0

    kall = jnp.concatenate([kp_ref[...], kc_ref[...]], axis=0)
    vall = jnp.concatenate([vp_ref[...], vc_ref[...]], axis=0)
    k_heads = _split_kv(kall)
    v_heads = _split_kv(vall)
    ri = lax.broadcasted_iota(jnp.int32, (BLK, 2 * BLK), 0)
    cj = lax.broadcasted_iota(jnp.int32, (BLK, 2 * BLK), 1)
    first_key = jnp.where(has_prev, 0, BLK)
    allowed = (cj >= ri) & (cj <= ri + WINDOW) & (cj >= first_key)
    n_pairs = d_attn // LANES
    pairs_per_kv = n_pairs // N_KV_HEADS
    for p in range(n_pairs):
        g = p // pairs_per_kv
        sl = slice(p * LANES, (p + 1) * LANES)
        o = _attend(q_ref[:, sl], k_heads[g], v_heads[g], allowed,
                    (sink_ref[2 * p], sink_ref[2 * p + 1]))
        o_ref[:, sl] = o.astype(BF16)

    xc = xc_ref[...]
    tail = jnp.where(has_prev, xt_ref[...], 0.0)
    full = jnp.concatenate([tail, xc], axis=0)
    hist = tail.shape[0]
    pos = i * BLK + lax.broadcasted_iota(jnp.int32, (BLK, 1), 0)
    cnt_of = lambda w: jnp.minimum(pos + 1, w).astype(F32)
    pooled = _pool_mix(full, xc, cnt_of, pw_ref, ps_ref, hist)
    gd = d_pool // len(POOL_WINDOWS)
    for g, y in enumerate(pooled):
        o_ref[:, d_attn + g * gd:d_attn + (g + 1) * gd] = y.astype(BF16)

    gs = d_sg // N_SG_GROUPS
    for g in range(N_SG_GROUPS):
        sl = slice(g * gs, (g + 1) * gs)
        s = jnp.dot(_causal_weights(sw_ref, g), vn_ref[:, sl].astype(BF16),
                    preferred_element_type=F32) + sb_ref[:, sl]
        c0 = d_attn + d_pool + g * gs
        o_ref[:, c0:c0 + gs] = (u_ref[:, sl] * s).astype(BF16)


def _mix_prompt(sinks, q, k, v, xp, u, vn, pool_w, pool_scale, sg_w, sg_bias, *, batch, seq, r_total):
    nb = seq // BLK
    d_attn, d_kv, d_pool, d_sg = q.shape[1], k.shape[1], xp.shape[1], u.shape[1]
    d_mix = d_attn + d_pool + d_sg
    hist = 2 * SUBLANES
    cur = lambda b, i: (b * nb + i, 0)
    prev = lambda b, i: (b * nb + jnp.maximum(i - 1, 0), 0)
    tail = lambda b, i: (jnp.maximum((b * nb + i) * (BLK // hist) - 1, 0), 0)
    const2 = lambda b, i: (0, 0)
    const3 = lambda b, i: (0, 0, 0)
    return pl.pallas_call(
        functools.partial(_mix_prompt_kernel, d_attn=d_attn, d_pool=d_pool, d_sg=d_sg),
        out_shape=jax.ShapeDtypeStruct((r_total, d_mix), BF16),
        grid=(batch, nb),
        in_specs=[
            pl.BlockSpec(memory_space=pltpu.SMEM),
            pl.BlockSpec((BLK, d_attn), cur),
            pl.BlockSpec((BLK, d_kv), cur),
            pl.BlockSpec((BLK, d_kv), prev),
            pl.BlockSpec((BLK, d_kv), cur),
            pl.BlockSpec((BLK, d_kv), prev),
            pl.BlockSpec((BLK, d_pool), cur),
            pl.BlockSpec((hist, d_pool), tail),
            pl.BlockSpec((BLK, d_sg), cur),
            pl.BlockSpec((BLK, d_sg), cur),
            pl.BlockSpec(pool_w.shape, const3),
            pl.BlockSpec((1, d_pool), const2),
            pl.BlockSpec(sg_w.shape, const3),
            pl.BlockSpec((BLK, d_sg), const2),
        ],
        out_specs=pl.BlockSpec((BLK, d_mix), cur),
        compiler_params=_cparams(2),
        name="mix_prompt",
    )(sinks, q, k, k, v, v, xp, xp, u, vn, pool_w, pool_scale, sg_w, sg_bias)


SEQ_PER_STEP = 2


def _mix_sample_kernel(sink_ref, mixed_hbm, q_ref, kn_ref, vn_new_ref, ck_ref, cv_ref, xn_ref, st_ref,
                       u_ref, vn_ref, pw_ref, ps_ref, sw_ref, sb_ref, o_ref, *, d_attn, d_pool, d_sg):
    del mixed_hbm
    sp = SAMPLE_PAD
    n_pairs = d_attn // LANES
    pairs_per_kv = n_pairs // N_KV_HEADS
    m_rows = pairs_per_kv * sp
    qf = q_ref[...].astype(F32)

    ri = lax.broadcasted_iota(jnp.int32, (m_rows, 2 * BLK), 0) % sp
    cj = lax.broadcasted_iota(jnp.int32, (m_rows, 2 * BLK), 1)
    allowed = (cj >= ri) & (cj <= ri + WINDOW)
    row_pair = lax.broadcasted_iota(jnp.int32, (m_rows, 1), 0) // sp

    attn_rows = []
    for s in range(SEQ_PER_STEP):
        rows = slice(s * sp, (s + 1) * sp)
        zpad = jnp.zeros((BLK - sp, kn_ref.shape[1]), F32)
        kall = jnp.concatenate([ck_ref[s * BLK:(s + 1) * BLK, :], kn_ref[rows, :], zpad], axis=0)
        vall = jnp.concatenate([cv_ref[s * BLK:(s + 1) * BLK, :], vn_new_ref[rows, :], zpad], axis=0)
        k_heads = _split_kv(kall)
        v_heads = _split_kv(vall)
        per_pair = []
        for g in range(N_KV_HEADS):
            qst = jnp.concatenate(
                [qf[rows, (g * pairs_per_kv + pl_) * LANES:(g * pairs_per_kv + pl_ + 1) * LANES]
                 for pl_ in range(pairs_per_kv)], axis=0).astype(BF16)
            sink_lo = jnp.zeros((m_rows, 1), F32)
            sink_hi = jnp.zeros((m_rows, 1), F32)
            for pl_ in range(pairs_per_kv):
                h = 2 * (g * pairs_per_kv + pl_)
                sink_lo = jnp.where(row_pair == pl_, sink_ref[h], sink_lo)
                sink_hi = jnp.where(row_pair == pl_, sink_ref[h + 1], sink_hi)
            o = _attend(qst, k_heads[g], v_heads[g], allowed, (sink_lo, sink_hi))
            per_pair.extend(o[pl_ * sp:(pl_ + 1) * sp, :] for pl_ in range(pairs_per_kv))
        attn_rows.append(per_pair)
    for p in range(n_pairs):
        o_ref[:, p * LANES:(p + 1) * LANES] = jnp.concatenate(
            [attn_rows[s][p] for s in range(SEQ_PER_STEP)], axis=0).astype(BF16)

    gd = d_pool // len(POOL_WINDOWS)
    pooled = []
    for s in range(SEQ_PER_STEP):
        rows = slice(s * sp, (s + 1) * sp)
        xc = xn_ref[rows, :]
        full = jnp.concatenate([st_ref[s], xc], axis=0)
        cnt_of = lambda w: float(w)
        pooled.append(_pool_mix(full, xc, cnt_of, pw_ref, ps_ref, st_ref.shape[1]))
    for g in range(len(POOL_WINDOWS)):
        o_ref[:, d_attn + g * gd:d_attn + (g + 1) * gd] = jnp.concatenate(
            [pooled[s][g] for s in range(SEQ_PER_STEP)], axis=0).astype(BF16)

    gs = d_sg // N_SG_GROUPS
    for g in range(N_SG_GROUPS):
        sl = slice(g * gs, (g + 1) * gs)
        wm = _causal_weights(sw_ref, g)
        outs = []
        for s in range(SEQ_PER_STEP):
            rows = slice(s * sp, (s + 1) * sp)
            vpad = jnp.concatenate([vn_ref[rows, sl], jnp.zeros((BLK - sp, gs), F32)], axis=0)
            sg = jnp.dot(wm, vpad.astype(BF16), preferred_element_type=F32)[:sp, :] + sb_ref[:sp, sl]
            outs.append(u_ref[rows, sl] * sg)
        c0 = d_attn + d_pool + g * gs
        o_ref[:, c0:c0 + gs] = jnp.concatenate(outs, axis=0).astype(BF16)


def _mix_sample(sinks, mixed, q, k, v, xp, u, vn, cache_k, cache_v, state, pool_w, pool_scale,
                sg_w, sg_bias, *, r_prompt, n_seq):
    d_attn, d_kv, d_pool, d_sg = q.shape[1], k.shape[1], xp.shape[1], u.shape[1]
    d_mix = d_attn + d_pool + d_sg
    rows = SEQ_PER_STEP * SAMPLE_PAD
    base = r_prompt // rows
    new = lambda i: (base + i, 0)
    per_seq = lambda i: (i, 0)
    const2 = lambda i: (0, 0)
    const3 = lambda i: (0, 0, 0)
    return pl.pallas_call(
        functools.partial(_mix_sample_kernel, d_attn=d_attn, d_pool=d_pool, d_sg=d_sg),
        out_shape=jax.ShapeDtypeStruct(mixed.shape, mixed.dtype),
        grid=(n_seq // SEQ_PER_STEP,),
        in_specs=[
            pl.BlockSpec(memory_space=pltpu.SMEM),
            pl.BlockSpec(memory_space=pl.ANY),
            pl.BlockSpec((rows, d_attn), new),
            pl.BlockSpec((rows, d_kv), new),
            pl.BlockSpec((rows, d_kv), new),
            pl.BlockSpec((SEQ_PER_STEP * BLK, d_kv), per_seq),
            pl.BlockSpec((SEQ_PER_STEP * BLK, d_kv), per_seq),
            pl.BlockSpec((rows, d_pool), new),
            pl.BlockSpec((SEQ_PER_STEP,) + state.shape[1:], lambda i: (i, 0, 0)),
            pl.BlockSpec((rows, d_sg), new),
            pl.BlockSpec((rows, d_sg), new),
            pl.BlockSpec(pool_w.shape, const3),
            pl.BlockSpec((1, d_pool), const2),
            pl.BlockSpec(sg_w.shape, const3),
            pl.BlockSpec((BLK, d_sg), const2),
        ],
        out_specs=pl.BlockSpec((rows, d_mix), new),
        input_output_aliases={1: 0},
        compiler_params=_cparams(1),
        name="mix_sample",
    )(sinks, mixed, q, k, v, cache_k, cache_v, xp, state, u, vn, pool_w, pool_scale, sg_w, sg_bias)


def _outproj_kernel(*refs, alpha, with_router):
    if with_router:
        (m_ref, x_ref, w_ref, g_ref, b_ref, rwh_ref, rwl_ref, rb_ref,
         x1_ref, x1b_ref, comb_ref) = refs
    else:
        m_ref, x_ref, w_ref, g_ref, b_ref, x1_ref, x1b_ref = refs
    y = alpha * x_ref[...] + jnp.dot(m_ref[...], w_ref[...], preferred_element_type=F32)
    x1 = _layer_norm(y, g_ref[...], b_ref[...])
    x1_ref[...] = x1
    hi = x1.astype(BF16)
    x1b_ref[...] = hi
    if with_router:
        lo = (x1 - hi.astype(F32)).astype(BF16)
        logits = (jnp.dot(hi, rwh_ref[...], preferred_element_type=F32)
                  + jnp.dot(lo, rwh_ref[...], preferred_element_type=F32)
                  + jnp.dot(hi, rwl_ref[...], preferred_element_type=F32) + rb_ref[...])
        lane = lax.broadcasted_iota(jnp.int32, logits.shape, 1).astype(F32)
        logits = jnp.where(lane < N_EXPERTS, logits, NEG_INF)
        m1 = jnp.max(logits, axis=1, keepdims=True)
        i1 = jnp.min(jnp.where(logits == m1, lane, float(LANES)), axis=1, keepdims=True)
        first = lane == i1
        rest = jnp.where(first, NEG_INF, logits)
        m2 = jnp.max(rest, axis=1, keepdims=True)
        i2 = jnp.min(jnp.where(rest == m2, lane, float(LANES)), axis=1, keepdims=True)
        second = lane == i2
        e = jnp.exp(m2 - m1)
        g1 = 1.0 / (1.0 + e)
        comb_ref[...] = jnp.where(first, g1, 0.0) + jnp.where(second, e * g1, 0.0)


def _outproj(mixed, x, w, g, b, router=None, *, tm, alpha):
    r, d = x.shape
    row = lambda i: (i, 0)
    const = lambda i: (0, 0)
    in_specs = [
        pl.BlockSpec((tm, mixed.shape[1]), row),
        pl.BlockSpec((tm, d), row),
        pl.BlockSpec(w.shape, const, pipeline_mode=pl.Buffered(1)),
        pl.BlockSpec((1, d), const),
        pl.BlockSpec((1, d), const),
    ]
    outs = [jax.ShapeDtypeStruct((r, d), F32), jax.ShapeDtypeStruct((r, d), BF16)]
    out_specs = [pl.BlockSpec((tm, d), row), pl.BlockSpec((tm, d), row)]
    args = [mixed, x, w, g, b]
    if router is not None:
        in_specs += [pl.BlockSpec((d, LANES), const), pl.BlockSpec((d, LANES), const),
                     pl.BlockSpec((1, LANES), const)]
        outs.append(jax.ShapeDtypeStruct((r, LANES), F32))
        out_specs.append(pl.BlockSpec((tm, LANES), row))
        args += list(router)
    return pl.pallas_call(
        functools.partial(_outproj_kernel, alpha=alpha, with_router=router is not None),
        out_shape=tuple(outs),
        grid=(r // tm,),
        in_specs=in_specs,
        out_specs=tuple(out_specs),
        compiler_params=_cparams(1),
        name="outproj_router" if router is not None else "outproj",
    )(*args)


def _ffn_kernel(*refs, alpha, with_gates):
    if with_gates:
        xb_ref, x1_ref, comb_ref, wg_ref, wu_ref, wd_ref, g_ref, b_ref, o_ref = refs
    else:
        xb_ref, x1_ref, wg_ref, wu_ref, wd_ref, g_ref, b_ref, o_ref = refs
    e = pl.program_id(1)
    f = pl.program_id(2)

    @pl.when((e == 0) & (f == 0))
    def _():
        o_ref[...] = alpha * x1_ref[...]

    xb = xb_ref[...]
    a = jnp.dot(xb, wg_ref[...], preferred_element_type=F32)
    c = jnp.dot(xb, wu_ref[...], preferred_element_type=F32)
    act = a * (1.0 / (1.0 + jnp.exp(-a))) * c
    if with_gates:
        comb = comb_ref[...]
        lane = lax.broadcasted_iota(jnp.int32, comb.shape, 1)
        act = act * jnp.sum(jnp.where(lane == e, comb, 0.0), axis=1, keepdims=True)
    o_ref[...] += jnp.dot(act.astype(BF16), wd_ref[...], preferred_element_type=F32)

    @pl.when((e == pl.num_programs(1) - 1) & (f == pl.num_programs(2) - 1))
    def _():
        o_ref[...] = _layer_norm(o_ref[...], g_ref[...], b_ref[...])


def _ffn(xb, x1, comb, wg, wu, wd, g, b, *, tm, tf, alpha):
    r, d = x1.shape
    n_e, _, ff = wg.shape
    row = lambda i, e, f: (i, 0)
    const = lambda i, e, f: (0, 0)
    in_specs = [pl.BlockSpec((tm, d), row), pl.BlockSpec((tm, d), row)]
    args = [xb, x1]
    if comb is not None:
        in_specs.append(pl.BlockSpec((tm, LANES), row))
        args.append(comb)
    in_specs += [
        pl.BlockSpec((None, d, tf), lambda i, e, f: (e, 0, f)),
        pl.BlockSpec((None, d, tf), lambda i, e, f: (e, 0, f)),
        pl.BlockSpec((None, tf, d), lambda i, e, f: (e, f, 0)),
        pl.BlockSpec((1, d), const),
        pl.BlockSpec((1, d), const),
    ]
    args += [wg, wu, wd, g, b]
    return pl.pallas_call(
        functools.partial(_ffn_kernel, alpha=alpha, with_gates=comb is not None),
        out_shape=jax.ShapeDtypeStruct((r, d), F32),
        grid=(r // tm, n_e, ff // tf),
        in_specs=in_specs,
        out_specs=pl.BlockSpec((tm, d), row),
        compiler_params=_cparams(3),
        name="moe_dense" if comb is not None else "ffn",
    )(*args)


def _rope_tables(pos):
    half = HEAD_DIM // 2
    inv = ROPE_THETA ** (-jnp.arange(half, dtype=F32) / half)
    ang = pos.astype(F32)[:, None] * inv[None, :]
    cos = jnp.cos(ang)
    sin = jnp.sin(ang)
    cos2 = jnp.concatenate([cos, cos, cos, cos], axis=1)
    sin2 = jnp.concatenate([-sin, sin, -sin, sin], axis=1)
    return cos2, sin2


def _row_tile(r, cap):
    best = 16
    for t in range(16, cap + 1, 16):
        if r % t == 0:
            best = t
    return best


def kernel(x_prompt, x_sample, cache_k, cache_v, state_pool, w_in, w_out, attn_sinks, pool_w,
           pool_scale, sg_w, sg_b, sg_norm_g, sg_norm_b, ln1_g, ln1_b, ln2_g, ln2_b,
           ffn_w_gate, ffn_w_up, ffn_w_down, router_w, router_b, moe_w_gate, moe_w_up, moe_w_down):
    bp, tp, d = x_prompt.shape
    bs, ts, _ = x_sample.shape
    depth = w_in.shape[0]
    d_kv = cache_k.shape[3] * cache_k.shape[4]
    d_pool = pool_scale.shape[1]
    d_sg = sg_norm_g.shape[1]
    d_attn = w_in.shape[2] - 2 * d_kv - d_pool - 2 * d_sg
    win_buf = cache_k.shape[2]
    assert win_buf == WINDOW == BLK and tp % BLK == 0 and tp >= BLK and ts <= SAMPLE_PAD
    assert bs % SEQ_PER_STEP == 0 and state_pool.shape[2] == POOL_STATE
    assert cache_k.shape[3] == N_KV_HEADS and cache_k.shape[4] == HEAD_DIM
    alpha = (2 * depth) ** 0.25

    r_p = bp * tp
    r_s = bs * SAMPLE_PAD
    r = r_p + r_s
    tm = _row_tile(r, 768)
    tm_out = _row_tile(r, 384)

    xs_pad = jnp.pad(x_sample, ((0, 0), (0, SAMPLE_PAD - ts), (0, 0)))
    x = jnp.concatenate([x_prompt.reshape(r_p, d), xs_pad.reshape(r_s, d)], axis=0)

    pos = jnp.concatenate([
        jnp.tile(jnp.arange(tp, dtype=jnp.int32), bp),
        jnp.tile(PAST_LEN + jnp.arange(SAMPLE_PAD, dtype=jnp.int32), bs)])
    cos, sin = _rope_tables(pos)

    w_in_b = w_in.astype(BF16)
    w_out_b = w_out.astype(BF16)
    pool_w_b = pool_w.astype(BF16)
    ffn_g, ffn_u, ffn_d = (w.astype(BF16) for w in (ffn_w_gate, ffn_w_up, ffn_w_down))
    moe_g, moe_u, moe_d = (w.astype(BF16) for w in (moe_w_gate, moe_w_up, moe_w_down))
    rw_pad = jnp.pad(router_w, ((0, 0), (0, 0), (0, LANES - router_w.shape[2])))
    rw_hi = rw_pad.astype(BF16)
    rw_lo = (rw_pad - rw_hi.astype(F32)).astype(BF16)
    rb_pad = jnp.pad(router_b, ((0, 0), (0, LANES - router_b.shape[1])))[:, None, :]
    sg_bias = jnp.repeat(jnp.swapaxes(sg_b, 1, 2), d_sg // N_SG_GROUPS, axis=2)
    state_pad = jnp.pad(state_pool, ((0, 0), (0, 0), (1, 0), (0, 0)))

    kp_l, vp_l, pp_l, ks_l, vs_l, ps_l, sg_l = [], [], [], [], [], [], []
    for l in range(depth):
        q, k, v, xp, u, vn = _inproj(
            x, w_in_b[l], cos, sin, sg_norm_g[l][None], sg_norm_b[l][None],
            tm=tm, d_attn=d_attn, d_kv=d_kv, d_pool=d_pool, d_sg=d_sg)
        mixed = _mix_prompt(attn_sinks[l], q, k, v, xp, u, vn, pool_w_b[l], pool_scale[l][None],
                            sg_w[l], sg_bias[l], batch=bp, seq=tp, r_total=r)
        mixed = _mix_sample(attn_sinks[l], mixed, q, k, v, xp, u, vn,
                            cache_k[l].reshape(bs * win_buf, d_kv), cache_v[l].reshape(bs * win_buf, d_kv),
                            state_pad[l], pool_w_b[l], pool_scale[l][None], sg_w[l], sg_bias[l],
                            r_prompt=r_p, n_seq=bs)
        i = l // 2
        if l % 2 == 0:
            x1, x1b = _outproj(mixed, x, w_out_b[l], ln1_g[l][None], ln1_b[l][None], tm=tm_out, alpha=alpha)
            x = _ffn(x1b, x1, None, ffn_g[i][None], ffn_u[i][None], ffn_d[i][None],
                     ln2_g[l][None], ln2_b[l][None], tm=tm, tf=512, alpha=alpha)
        else:
            x1, x1b, comb = _outproj(mixed, x, w_out_b[l], ln1_g[l][None], ln1_b[l][None],
                                     router=(rw_hi[i], rw_lo[i], rb_pad[i]), tm=tm_out, alpha=alpha)
            x = _ffn(x1b, x1, comb, moe_g[i], moe_u[i], moe_d[i],
                     ln2_g[l][None], ln2_b[l][None], tm=tm, tf=512, alpha=alpha)

        kh = (N_KV_HEADS, HEAD_DIM)
        kp_l.append(k[:r_p].reshape(bp, tp, *kh)[:, -WINDOW:])
        vp_l.append(v[:r_p].reshape(bp, tp, *kh)[:, -WINDOW:])
        pp_l.append(xp[:r_p].reshape(bp, tp, d_pool)[:, -POOL_STATE:])
        k_new = k[r_p:].reshape(bs, SAMPLE_PAD, *kh)[:, :ts]
        v_new = v[r_p:].reshape(bs, SAMPLE_PAD, *kh)[:, :ts]
        ks_l.append(jnp.concatenate([cache_k[l], k_new], axis=1)[:, -win_buf:])
        vs_l.append(jnp.concatenate([cache_v[l], v_new], axis=1)[:, -win_buf:])
        xp_new = xp[r_p:].reshape(bs, SAMPLE_PAD, d_pool)[:, :ts]
        ps_l.append(jnp.concatenate([state_pool[l], xp_new], axis=1)[:, -POOL_STATE:])
        sg_l.append(vn[r_p:].reshape(bs, SAMPLE_PAD, d_sg)[:, :ts])

    y_prompt = x[:r_p].reshape(bp, tp, d)
    y_sample = x[r_p:].reshape(bs, SAMPLE_PAD, d)[:, :ts]
    return (y_prompt, y_sample, jnp.stack(kp_l), jnp.stack(vp_l), jnp.stack(pp_l),
            jnp.stack(ks_l), jnp.stack(vs_l), jnp.stack(ps_l), jnp.stack(sg_l))
```

```python
import functools
import math

import jax
import jax.numpy as jnp
from jax import lax
from jax.experimental import pallas as pl
from jax.experimental.pallas import tpu as pltpu

F32 = jnp.float32
BF16 = jnp.bfloat16

PAST_LEN = 16384
HEAD_DIM = 64
N_KV_HEADS = 2
WINDOW = 128
BLK = 128
ROPE_THETA = 10000.0
POOL_WINDOWS = (2, 4, 8, 16)
POOL_STATE = 15
N_SG_GROUPS = 4
N_EXPERTS = 8
LN_EPS = 1e-5

LANES = 128
SUBLANES = 8
SAMPLE_PAD = SUBLANES
VMEM_LIMIT = 56 * 1024 * 1024

NEG_INF = float("-inf")


def _cparams(n_axes):
    return pltpu.CompilerParams(
        dimension_semantics=("arbitrary",) * n_axes, vmem_limit_bytes=VMEM_LIMIT)


def _layer_norm(x, g, b):
    mu = jnp.mean(x, axis=-1, keepdims=True)
    xc = x - mu
    var = jnp.mean(xc * xc, axis=-1, keepdims=True)
    return xc * lax.rsqrt(var + LN_EPS) * g + b


def _gelu_tanh(x):
    return 0.5 * x * (1.0 + jnp.tanh(0.7978845608028654 * (x + 0.044715 * (x * x * x))))


def _inproj_kernel(x_ref, w_ref, cos_ref, sin_ref, g_ref, b_ref,
                   q_ref, k_ref, v_ref, xp_ref, u_ref, vn_ref, *, d_attn, d_kv, d_pool, d_sg):
    xb = x_ref[...].astype(BF16)
    cos = cos_ref[...]
    sin = sin_ref[...]
    tm = xb.shape[0]
    lane = lax.broadcasted_iota(jnp.int32, (tm, LANES), 1)
    first_half = (lane % HEAD_DIM) < (HEAD_DIM // 2)

    def rope(z):
        rot = jnp.where(first_half, pltpu.roll(z, LANES - HEAD_DIM // 2, 1),
                        pltpu.roll(z, HEAD_DIM // 2, 1))
        return z * cos + rot * sin

    c0 = 0
    zq = jnp.dot(xb, w_ref[:, c0:c0 + d_attn], preferred_element_type=F32)
    for j in range(d_attn // LANES):
        sl = slice(j * LANES, (j + 1) * LANES)
        q_ref[:, sl] = (rope(zq[:, sl]) * (HEAD_DIM ** -0.5)).astype(BF16)
    c0 += d_attn
    zkv = jnp.dot(xb, w_ref[:, c0:c0 + 2 * d_kv], preferred_element_type=F32)
    k_ref[...] = rope(zkv[:, :d_kv])
    v_ref[...] = zkv[:, d_kv:]
    c0 += 2 * d_kv
    xp_ref[...] = jnp.dot(xb, w_ref[:, c0:c0 + d_pool], preferred_element_type=F32)
    c0 += d_pool
    u_ref[...] = _gelu_tanh(jnp.dot(xb, w_ref[:, c0:c0 + d_sg], preferred_element_type=F32))
    c0 += d_sg
    vg = _gelu_tanh(jnp.dot(xb, w_ref[:, c0:c0 + d_sg], preferred_element_type=F32))
    gd = d_sg // N_SG_GROUPS
    for g in range(N_SG_GROUPS):
        sl = slice(g * gd, (g + 1) * gd)
        vn_ref[:, sl] = _layer_norm(vg[:, sl], g_ref[:, sl], b_ref[:, sl])


def _inproj(x, w, cos, sin, sg_g, sg_b, *, tm, d_attn, d_kv, d_pool, d_sg):
    r, d = x.shape
    d_in = w.shape[1]
    row = lambda i: (i, 0)
    const = lambda i: (0, 0)
    outs = (
        jax.ShapeDtypeStruct((r, d_attn), BF16),
        jax.ShapeDtypeStruct((r, d_kv), F32),
        jax.ShapeDtypeStruct((r, d_kv), F32),
        jax.ShapeDtypeStruct((r, d_pool), F32),
        jax.ShapeDtypeStruct((r, d_sg), F32),
        jax.ShapeDtypeStruct((r, d_sg), F32),
    )
    return pl.pallas_call(
        functools.partial(_inproj_kernel, d_attn=d_attn, d_kv=d_kv, d_pool=d_pool, d_sg=d_sg),
        out_shape=outs,
        grid=(r // tm,),
        in_specs=[
            pl.BlockSpec((tm, d), row),
            pl.BlockSpec((d, d_in), const, pipeline_mode=pl.Buffered(1)),
            pl.BlockSpec((tm, LANES), row),
            pl.BlockSpec((tm, LANES), row),
            pl.BlockSpec((1, d_sg), const),
            pl.BlockSpec((1, d_sg), const),
        ],
        out_specs=(
            pl.BlockSpec((tm, d_attn), row),
            pl.BlockSpec((tm, d_kv), row),
            pl.BlockSpec((tm, d_kv), row),
            pl.BlockSpec((tm, d_pool), row),
            pl.BlockSpec((tm, d_sg), row),
            pl.BlockSpec((tm, d_sg), row),
        ),
        compiler_params=_cparams(1),
        name="inproj",
    )(x, w, cos, sin, sg_g, sg_b)


def _split_kv(kv):
    lane = lax.broadcasted_iota(jnp.int32, kv.shape, 1)
    low = lane < HEAD_DIM
    a = kv.astype(BF16)
    b = pltpu.roll(kv, HEAD_DIM, 1).astype(BF16)
    zero = jnp.zeros_like(a)
    head0 = (jnp.where(low, a, zero), jnp.where(low, zero, b))
    head1 = (jnp.where(low, b, zero), jnp.where(low, zero, a))
    return head0, head1


def _attend(q, k_pair, v_pair, allowed, sinks):
    out = None
    for kx, vx, sk in zip(k_pair, v_pair, sinks):
        s = lax.dot_general(q, kx, (((1,), (1,)), ((), ())), preferred_element_type=F32)
        s = jnp.where(allowed, s, NEG_INF)
        m = jnp.maximum(jnp.max(s, axis=1, keepdims=True), sk)
        p = jnp.exp(s - m)
        denom = jnp.sum(p, axis=1, keepdims=True) + jnp.exp(sk - m)
        o = jnp.dot(p.astype(BF16), vx, preferred_element_type=F32) * (1.0 / denom)
        out = o if out is None else out + o
    return out


def _window_sums(full):
    sums = {1: full}
    w = 1
    while w < max(POOL_WINDOWS):
        sums[2 * w] = sums[w] + pltpu.roll(sums[w], w, 0)
        w *= 2
    return sums


def _pool_mix(full, cur, cnt_of, pw_ref, ps_ref, row0):
    n = cur.shape[0]
    gd = cur.shape[1] // len(POOL_WINDOWS)
    outs = []
    for g, w in enumerate(POOL_WINDOWS):
        sl = slice(g * gd, (g + 1) * gd)
        sums = _window_sums(full[:, sl])[w]
        pooled = sums[row0:row0 + n, :] / cnt_of(w)
        d = (pooled - cur[:, sl]).astype(BF16)
        outs.append(jnp.dot(d, pw_ref[g], preferred_element_type=F32) * ps_ref[:, sl])
    return outs


def _causal_weights(sw_ref, g):
    n = sw_ref.shape[1]
    ri = lax.broadcasted_iota(jnp.int32, (n, n), 0)
    ci = lax.broadcasted_iota(jnp.int32, (n, n), 1)
    return jnp.where(ri >= ci, sw_ref[g], 0.0).astype(BF16)


def _mix_prompt_kernel(sink_ref, q_ref, kc_ref, kp_ref, vc_ref, vp_ref, xc_ref, xt_ref, u_ref,
                       vn_ref, pw_ref, ps_ref, sw_ref, sb_ref, o_ref, *, d_attn, d_pool, d_sg):
    i = pl.program_id(1)
    has_prev = i > 0

    kall = jnp.concatenate([kp_ref[...], kc_ref[...]], axis=0)
    vall = jnp.concatenate([vp_ref[...], vc_ref[...]], axis=0)
    k_heads = _split_kv(kall)
    v_heads = _split_kv(vall)
    ri = lax.broadcasted_iota(jnp.int32, (BLK, 2 * BLK), 0)
    cj = lax.broadcasted_iota(jnp.int32, (BLK, 2 * BLK), 1)
    first_key = jnp.where(has_prev, 0, BLK)
    allowed = (cj >= ri) & (cj <= ri + WINDOW) & (cj >= first_key)
    n_pairs = d_attn // LANES
    pairs_per_kv = n_pairs // N_KV_HEADS
    for p in range(n_pairs):
        g = p // pairs_per_kv
        sl = slice(p * LANES, (p + 1) * LANES)
        o = _attend(q_ref[:, sl], k_heads[g], v_heads[g], allowed,
                    (sink_ref[2 * p], sink_ref[2 * p + 1]))
        o_ref[:, sl] = o.astype(BF16)

    xc = xc_ref[...]
    tail = jnp.where(has_prev, xt_ref[...], 0.0)
    full = jnp.concatenate([tail, xc], axis=0)
    hist = tail.shape[0]
    pos = i * BLK + lax.broadcasted_iota(jnp.int32, (BLK, 1), 0)
    cnt_of = lambda w: jnp.minimum(pos + 1, w).astype(F32)
    pooled = _pool_mix(full, xc, cnt_of, pw_ref, ps_ref, hist)
    gd = d_pool // len(POOL_WINDOWS)
    for g, y in enumerate(pooled):
        o_ref[:, d_attn + g * gd:d_attn + (g + 1) * gd] = y.astype(BF16)

    gs = d_sg // N_SG_GROUPS
    for g in range(N_SG_GROUPS):
        sl = slice(g * gs, (g + 1) * gs)
        s = jnp.dot(_causal_weights(sw_ref, g), vn_ref[:, sl].astype(BF16),
                    preferred_element_type=F32) + sb_ref[:, sl]
        c0 = d_attn + d_pool + g * gs
        o_ref[:, c0:c0 + gs] = (u_ref[:, sl] * s).astype(BF16)


def _mix_prompt(sinks, q, k, v, xp, u, vn, pool_w, pool_scale, sg_w, sg_bias, *, batch, seq):
    nb = seq // BLK
    d_attn, d_kv, d_pool, d_sg = q.shape[1], k.shape[1], xp.shape[1], u.shape[1]
    d_mix = d_attn + d_pool + d_sg
    hist = 2 * SUBLANES
    cur = lambda b, i: (b * nb + i, 0)
    prev = lambda b, i: (b * nb + jnp.maximum(i - 1, 0), 0)
    tail = lambda b, i: (jnp.maximum((b * nb + i) * (BLK // hist) - 1, 0), 0)
    const2 = lambda b, i: (0, 0)
    const3 = lambda b, i: (0, 0, 0)
    return pl.pallas_call(
        functools.partial(_mix_prompt_kernel, d_attn=d_attn, d_pool=d_pool, d_sg=d_sg),
        out_shape=jax.ShapeDtypeStruct((batch * seq, d_mix), BF16),
        grid=(batch, nb),
        in_specs=[
            pl.BlockSpec(memory_space=pltpu.SMEM),
            pl.BlockSpec((BLK, d_attn), cur),
            pl.BlockSpec((BLK, d_kv), cur),
            pl.BlockSpec((BLK, d_kv), prev),
            pl.BlockSpec((BLK, d_kv), cur),
            pl.BlockSpec((BLK, d_kv), prev),
            pl.BlockSpec((BLK, d_pool), cur),
            pl.BlockSpec((hist, d_pool), tail),
            pl.BlockSpec((BLK, d_sg), cur),
            pl.BlockSpec((BLK, d_sg), cur),
            pl.BlockSpec(pool_w.shape, const3),
            pl.BlockSpec((1, d_pool), const2),
            pl.BlockSpec(sg_w.shape, const3),
            pl.BlockSpec((BLK, d_sg), const2),
        ],
        out_specs=pl.BlockSpec((BLK, d_mix), cur),
        compiler_params=_cparams(2),
        name="mix_prompt",
    )(sinks, q, k, k, v, v, xp, xp, u, vn, pool_w, pool_scale, sg_w, sg_bias)


SEQ_PER_STEP = 2


def _mix_sample_kernel(sink_ref, q_ref, kn_ref, vn_new_ref, ck_ref, cv_ref, xn_ref, st_ref,
                       u_ref, vn_ref, pw_ref, ps_ref, sw_ref, sb_ref, o_ref, *, d_attn, d_pool, d_sg):
    sp = SAMPLE_PAD
    n_pairs = d_attn // LANES
    pairs_per_kv = n_pairs // N_KV_HEADS
    m_rows = pairs_per_kv * sp
    qf = q_ref[...].astype(F32)

    ri = lax.broadcasted_iota(jnp.int32, (m_rows, 2 * BLK), 0) % sp
    cj = lax.broadcasted_iota(jnp.int32, (m_rows, 2 * BLK), 1)
    allowed = (cj >= ri) & (cj <= ri + WINDOW)
    row_pair = lax.broadcasted_iota(jnp.int32, (m_rows, 1), 0) // sp

    attn_rows = []
    for s in range(SEQ_PER_STEP):
        rows = slice(s * sp, (s + 1) * sp)
        zpad = jnp.zeros((BLK - sp, kn_ref.shape[1]), F32)
        kall = jnp.concatenate([ck_ref[s * BLK:(s + 1) * BLK, :], kn_ref[rows, :], zpad], axis=0)
        vall = jnp.concatenate([cv_ref[s * BLK:(s + 1) * BLK, :], vn_new_ref[rows, :], zpad], axis=0)
        k_heads = _split_kv(kall)
        v_heads = _split_kv(vall)
        per_pair = []
        for g in range(N_KV_HEADS):
            qst = jnp.concatenate(
                [qf[rows, (g * pairs_per_kv + pl_) * LANES:(g * pairs_per_kv + pl_ + 1) * LANES]
                 for pl_ in range(pairs_per_kv)], axis=0).astype(BF16)
            sink_lo = jnp.zeros((m_rows, 1), F32)
            sink_hi = jnp.zeros((m_rows, 1), F32)
            for pl_ in range(pairs_per_kv):
                h = 2 * (g * pairs_per_kv + pl_)
                sink_lo = jnp.where(row_pair == pl_, sink_ref[h], sink_lo)
                sink_hi = jnp.where(row_pair == pl_, sink_ref[h + 1], sink_hi)
            o = _attend(qst, k_heads[g], v_heads[g], allowed, (sink_lo, sink_hi))
            per_pair.extend(o[pl_ * sp:(pl_ + 1) * sp, :] for pl_ in range(pairs_per_kv))
        attn_rows.append(per_pair)
    for p in range(n_pairs):
        o_ref[:, p * LANES:(p + 1) * LANES] = jnp.concatenate(
            [attn_rows[s][p] for s in range(SEQ_PER_STEP)], axis=0).astype(BF16)

    gd = d_pool // len(POOL_WINDOWS)
    pooled = []
    for s in range(SEQ_PER_STEP):
        rows = slice(s * sp, (s + 1) * sp)
        xc = xn_ref[rows, :]
        full = jnp.concatenate([st_ref[s], xc], axis=0)
        cnt_of = lambda w: float(w)
        pooled.append(_pool_mix(full, xc, cnt_of, pw_ref, ps_ref, st_ref.shape[1]))
    for g in range(len(POOL_WINDOWS)):
        o_ref[:, d_attn + g * gd:d_attn + (g + 1) * gd] = jnp.concatenate(
            [pooled[s][g] for s in range(SEQ_PER_STEP)], axis=0).astype(BF16)

    gs = d_sg // N_SG_GROUPS
    for g in range(N_SG_GROUPS):
        sl = slice(g * gs, (g + 1) * gs)
        wm = _causal_weights(sw_ref, g)
        outs = []
        for s in range(SEQ_PER_STEP):
            rows = slice(s * sp, (s + 1) * sp)
            vpad = jnp.concatenate([vn_ref[rows, sl], jnp.zeros((BLK - sp, gs), F32)], axis=0)
            sg = jnp.dot(wm, vpad.astype(BF16), preferred_element_type=F32)[:sp, :] + sb_ref[:sp, sl]
            outs.append(u_ref[rows, sl] * sg)
        c0 = d_attn + d_pool + g * gs
        o_ref[:, c0:c0 + gs] = jnp.concatenate(outs, axis=0).astype(BF16)


def _mix_sample(sinks, q, k, v, xp, u, vn, cache_k, cache_v, state, pool_w, pool_scale,
                sg_w, sg_bias, *, r_prompt, n_seq):
    d_attn, d_kv, d_pool, d_sg = q.shape[1], k.shape[1], xp.shape[1], u.shape[1]
    d_mix = d_attn + d_pool + d_sg
    rows = SEQ_PER_STEP * SAMPLE_PAD
    base = r_prompt // rows
    new = lambda i: (base + i, 0)
    per_seq = lambda i: (i, 0)
    const2 = lambda i: (0, 0)
    const3 = lambda i: (0, 0, 0)
    return pl.pallas_call(
        functools.partial(_mix_sample_kernel, d_attn=d_attn, d_pool=d_pool, d_sg=d_sg),
        out_shape=jax.ShapeDtypeStruct((n_seq * SAMPLE_PAD, d_mix), BF16),
        grid=(n_seq // SEQ_PER_STEP,),
        in_specs=[
            pl.BlockSpec(memory_space=pltpu.SMEM),
            pl.BlockSpec((rows, d_attn), new),
            pl.BlockSpec((rows, d_kv), new),
            pl.BlockSpec((rows, d_kv), new),
            pl.BlockSpec((SEQ_PER_STEP * BLK, d_kv), per_seq),
            pl.BlockSpec((SEQ_PER_STEP * BLK, d_kv), per_seq),
            pl.BlockSpec((rows, d_pool), new),
            pl.BlockSpec((SEQ_PER_STEP,) + state.shape[1:], lambda i: (i, 0, 0)),
            pl.BlockSpec((rows, d_sg), new),
            pl.BlockSpec((rows, d_sg), new),
            pl.BlockSpec(pool_w.shape, const3),
            pl.BlockSpec((1, d_pool), const2),
            pl.BlockSpec(sg_w.shape, const3),
            pl.BlockSpec((BLK, d_sg), const2),
        ],
        out_specs=pl.BlockSpec((rows, d_mix), per_seq),
        compiler_params=_cparams(1),
        name="mix_sample",
    )(sinks, q, k, v, cache_k, cache_v, xp, state, u, vn, pool_w, pool_scale, sg_w, sg_bias)


ROUTE_E1, ROUTE_E2, ROUTE_G1, ROUTE_G2, ROUTE_R1, ROUTE_R2 = range(6)


def _lane_pick(rec, k):
    lane = lax.broadcasted_iota(jnp.int32, rec.shape, 1)
    return jnp.sum(jnp.where(lane == k, rec, 0.0), axis=1, keepdims=True)


def _outproj_kernel(*refs, alpha, with_router, n_prompt_tiles):
    if with_router:
        (mp_ref, ms_ref, x_ref, w_ref, g_ref, b_ref, rwh_ref, rwl_ref, rb_ref,
         x1_ref, route_ref, cnt_ref, carry_ref) = refs
    else:
        mp_ref, ms_ref, x_ref, w_ref, g_ref, b_ref, x1_ref, x1b_ref = refs
    mixed = jnp.where(pl.program_id(0) < n_prompt_tiles, mp_ref[...], ms_ref[...])
    y = alpha * x_ref[...] + jnp.dot(mixed, w_ref[...], preferred_element_type=F32)
    x1 = _layer_norm(y, g_ref[...], b_ref[...])
    x1_ref[...] = x1
    hi = x1.astype(BF16)
    if not with_router:
        x1b_ref[...] = hi
        return

    lo = (x1 - hi.astype(F32)).astype(BF16)
    logits = (jnp.dot(hi, rwh_ref[...], preferred_element_type=F32)
              + jnp.dot(lo, rwh_ref[...], preferred_element_type=F32)
              + jnp.dot(hi, rwl_ref[...], preferred_element_type=F32) + rb_ref[...])
    lane_i = lax.broadcasted_iota(jnp.int32, logits.shape, 1)
    lane = lane_i.astype(F32)
    logits = jnp.where(lane_i < N_EXPERTS, logits, NEG_INF)
    m1 = jnp.max(logits, axis=1, keepdims=True)
    i1 = jnp.min(jnp.where(logits == m1, lane, float(LANES)), axis=1, keepdims=True)
    first = lane == i1
    rest = jnp.where(first, NEG_INF, logits)
    m2 = jnp.max(rest, axis=1, keepdims=True)
    i2 = jnp.min(jnp.where(rest == m2, lane, float(LANES)), axis=1, keepdims=True)
    second = lane == i2
    e = jnp.exp(m2 - m1)
    g1 = 1.0 / (1.0 + e)
    g2 = e * g1

    @pl.when(pl.program_id(0) == 0)
    def _():
        carry_ref[...] = jnp.zeros_like(carry_ref)

    sel = jnp.where(first, 1.0, 0.0) + jnp.where(second, 1.0, 0.0)
    tm = sel.shape[0]
    ri = lax.broadcasted_iota(jnp.int32, (tm, tm), 0)
    ci = lax.broadcasted_iota(jnp.int32, (tm, tm), 1)
    earlier = jnp.where(ri > ci, 1.0, 0.0).astype(BF16)
    ranks = jnp.dot(earlier, sel.astype(BF16), preferred_element_type=F32) + carry_ref[...]
    r1 = jnp.sum(jnp.where(first, ranks, 0.0), axis=1, keepdims=True)
    r2 = jnp.sum(jnp.where(second, ranks, 0.0), axis=1, keepdims=True)
    carry_ref[...] += jnp.sum(sel, axis=0, keepdims=True)
    cnt_ref[...] = carry_ref[...]

    rec = jnp.zeros_like(logits)
    for k, val in ((ROUTE_E1, i1), (ROUTE_E2, i2), (ROUTE_G1, g1), (ROUTE_G2, g2),
                   (ROUTE_R1, r1), (ROUTE_R2, r2)):
        rec = jnp.where(lane_i == k, val, rec)
    route_ref[...] = rec


def _outproj(mixed_p, mixed_s, x, w, g, b, router=None, *, tm, alpha):
    r, d = x.shape
    assert mixed_p.shape[0] % tm == 0 and mixed_s.shape[0] % tm == 0
    n_p = mixed_p.shape[0] // tm
    row = lambda i: (i, 0)
    const = lambda i: (0, 0)
    in_specs = [
        pl.BlockSpec((tm, mixed_p.shape[1]), lambda i: (jnp.minimum(i, n_p - 1), 0)),
        pl.BlockSpec((tm, mixed_s.shape[1]), lambda i: (jnp.maximum(i - n_p, 0), 0)),
        pl.BlockSpec((tm, d), row),
        pl.BlockSpec(w.shape, const, pipeline_mode=pl.Buffered(1)),
        pl.BlockSpec((1, d), const),
        pl.BlockSpec((1, d), const),
    ]
    args = [mixed_p, mixed_s, x, w, g, b]
    scratch = []
    if router is None:
        outs = [jax.ShapeDtypeStruct((r, d), F32), jax.ShapeDtypeStruct((r, d), BF16)]
        out_specs = [pl.BlockSpec((tm, d), row), pl.BlockSpec((tm, d), row)]
    else:
        in_specs += [pl.BlockSpec((d, LANES), const), pl.BlockSpec((d, LANES), const),
                     pl.BlockSpec((1, LANES), const)]
        args += list(router)
        outs = [jax.ShapeDtypeStruct((r, d), F32), jax.ShapeDtypeStruct((r, LANES), F32),
                jax.ShapeDtypeStruct((1, LANES), F32)]
        out_specs = [pl.BlockSpec((tm, d), row), pl.BlockSpec((tm, LANES), row),
                     pl.BlockSpec((1, LANES), const)]
        scratch = [pltpu.VMEM((1, LANES), F32)]
    return pl.pallas_call(
        functools.partial(_outproj_kernel, alpha=alpha, with_router=router is not None,
                          n_prompt_tiles=n_p),
        out_shape=tuple(outs),
        grid=(r // tm,),
        in_specs=in_specs,
        out_specs=tuple(out_specs),
        scratch_shapes=scratch,
        compiler_params=_cparams(1),
        name="outproj_router" if router is not None else "outproj",
    )(*args)


def _swiglu(xb, wg_ref, wu_ref):
    a = jnp.dot(xb, wg_ref[...], preferred_element_type=F32)
    c = jnp.dot(xb, wu_ref[...], preferred_element_type=F32)
    return (a * (1.0 / (1.0 + jnp.exp(-a))) * c).astype(BF16)


def _ffn_kernel(xb_ref, x1_ref, wg_ref, wu_ref, wd_ref, g_ref, b_ref, o_ref, *, alpha):
    f = pl.program_id(1)

    @pl.when(f == 0)
    def _():
        o_ref[...] = alpha * x1_ref[...]

    o_ref[...] += jnp.dot(_swiglu(xb_ref[...], wg_ref, wu_ref), wd_ref[...],
                          preferred_element_type=F32)

    @pl.when(f == pl.num_programs(1) - 1)
    def _():
        o_ref[...] = _layer_norm(o_ref[...], g_ref[...], b_ref[...])


def _ffn(xb, x1, wg, wu, wd, g, b, *, tm, tf, alpha):
    r, d = x1.shape
    ff = wg.shape[1]
    row = lambda i, f: (i, 0)
    const = lambda i, f: (0, 0)
    return pl.pallas_call(
        functools.partial(_ffn_kernel, alpha=alpha),
        out_shape=jax.ShapeDtypeStruct((r, d), F32),
        grid=(r // tm, ff // tf),
        in_specs=[
            pl.BlockSpec((tm, d), row),
            pl.BlockSpec((tm, d), row),
            pl.BlockSpec((d, tf), lambda i, f: (0, f)),
            pl.BlockSpec((d, tf), lambda i, f: (0, f)),
            pl.BlockSpec((tf, d), lambda i, f: (f, 0)),
            pl.BlockSpec((1, d), const),
            pl.BlockSpec((1, d), const),
        ],
        out_specs=pl.BlockSpec((tm, d), row),
        compiler_params=_cparams(2),
        name="ffn",
    )(xb, x1, wg, wu, wd, g, b)


MOE_BLOCK = 512
DMA_RING = 16


def _routing_tables(route, counts, *, n_work):
    e1 = route[:, ROUTE_E1].astype(jnp.int32)
    e2 = route[:, ROUTE_E2].astype(jnp.int32)
    r1 = route[:, ROUTE_R1].astype(jnp.int32)
    r2 = route[:, ROUTE_R2].astype(jnp.int32)
    cnt = counts[0, :N_EXPERTS].astype(jnp.int32)
    nblk = (cnt + MOE_BLOCK - 1) // MOE_BLOCK
    blk_end = jnp.cumsum(nblk)
    start = (blk_end - nblk) * MOE_BLOCK
    slot1 = jnp.take(start, e1) + r1
    slot2 = jnp.take(start, e2) + r2
    n_active = blk_end[-1]
    blk = jnp.minimum(jnp.arange(n_work, dtype=jnp.int32), n_active - 1)
    work_exp = jnp.sum(blk[:, None] >= blk_end[None, :], axis=1).astype(jnp.int32)
    return slot1, slot2, work_exp, n_active.reshape(1)


def _ring_copies(n, make_copies):
    def start(i):
        for c in make_copies(i):
            c.start()

    def wait(i):
        for c in make_copies(i):
            c.wait()

    def fill(i, carry):
        start(i)
        return carry

    def steady(i, carry):
        wait(i - DMA_RING)
        start(i)
        return carry

    def drain(i, carry):
        wait(i)
        return carry

    lax.fori_loop(0, DMA_RING, fill, 0)
    lax.fori_loop(DMA_RING, n, steady, 0)
    lax.fori_loop(n - DMA_RING, n, drain, 0)


def _dispatch_kernel(s1_ref, s2_ref, x_hbm, xs_init, xs_hbm, sem, *, n_rows):
    del xs_init

    def copies(t):
        src = x_hbm.at[pl.ds(t, 1), :]
        k = t & (DMA_RING - 1)
        return (pltpu.make_async_copy(src, xs_hbm.at[pl.ds(s1_ref[t], 1), :], sem.at[0, k]),
                pltpu.make_async_copy(src, xs_hbm.at[pl.ds(s2_ref[t], 1), :], sem.at[1, k]))

    _ring_copies(n_rows, copies)


def _moe_dispatch(slot1, slot2, x1, *, n_slots):
    r, d = x1.shape
    return pl.pallas_call(
        functools.partial(_dispatch_kernel, n_rows=r),
        out_shape=jax.ShapeDtypeStruct((n_slots, d), F32),
        grid_spec=pltpu.PrefetchScalarGridSpec(
            num_scalar_prefetch=2,
            grid=(1,),
            in_specs=[pl.BlockSpec(memory_space=pl.ANY), pl.BlockSpec(memory_space=pl.ANY)],
            out_specs=pl.BlockSpec(memory_space=pl.ANY),
            scratch_shapes=[pltpu.SemaphoreType.DMA((2, DMA_RING))],
        ),
        input_output_aliases={3: 0},
        compiler_params=_cparams(1),
        name="moe_dispatch",
    )(slot1, slot2, x1, jnp.zeros((n_slots, d), F32))


def _moe_ffn_kernel(we_ref, na_ref, xs_ref, wg_ref, wu_ref, wd_ref, ys_ref, xb_ref):
    del we_ref
    w = pl.program_id(0)
    f = pl.program_id(1)

    @pl.when(f == 0)
    def _():
        xb_ref[...] = xs_ref[...].astype(BF16)
        ys_ref[...] = jnp.zeros_like(ys_ref)

    @pl.when(w < na_ref[0])
    def _():
        ys_ref[...] += jnp.dot(_swiglu(xb_ref[...], wg_ref, wu_ref), wd_ref[...],
                               preferred_element_type=F32)


def _moe_ffn(work_exp, n_active, xs, wg, wu, wd, *, tf):
    n_slots, d = xs.shape
    ff = wg.shape[2]
    nf = ff // tf
    rows = lambda w, f, we, na: (w, 0)
    chunk = lambda w, f, na: jnp.where(w < na[0], f, nf - 1)
    return pl.pallas_call(
        _moe_ffn_kernel,
        out_shape=jax.ShapeDtypeStruct((n_slots, d), F32),
        grid_spec=pltpu.PrefetchScalarGridSpec(
            num_scalar_prefetch=2,
            grid=(n_slots // MOE_BLOCK, nf),
            in_specs=[
                pl.BlockSpec((MOE_BLOCK, d), rows),
                pl.BlockSpec((None, d, tf), lambda w, f, we, na: (we[w], 0, chunk(w, f, na))),
                pl.BlockSpec((None, d, tf), lambda w, f, we, na: (we[w], 0, chunk(w, f, na))),
                pl.BlockSpec((None, tf, d), lambda w, f, we, na: (we[w], chunk(w, f, na), 0)),
            ],
            out_specs=pl.BlockSpec((MOE_BLOCK, d), rows),
            scratch_shapes=[pltpu.VMEM((MOE_BLOCK, d), BF16)],
        ),
        compiler_params=_cparams(2),
        name="moe_ffn",
    )(work_exp, n_active, xs, wg, wu, wd)


def _combine_kernel(s1_ref, s2_ref, ys_hbm, x1_ref, route_ref, g_ref, b_ref, o_ref,
                    y1_ref, y2_ref, sem, *, alpha):
    tm = o_ref.shape[0]
    base = pl.program_id(0) * tm

    def copies(r):
        k = r & (DMA_RING - 1)
        dst = pl.ds(r, 1)
        return (pltpu.make_async_copy(ys_hbm.at[pl.ds(s1_ref[base + r], 1), :], y1_ref.at[dst, :],
                                      sem.at[0, k]),
                pltpu.make_async_copy(ys_hbm.at[pl.ds(s2_ref[base + r], 1), :], y2_ref.at[dst, :],
                                      sem.at[1, k]))

    _ring_copies(tm, copies)
    route = route_ref[...]
    y = (alpha * x1_ref[...] + _lane_pick(route, ROUTE_G1) * y1_ref[...]
         + _lane_pick(route, ROUTE_G2) * y2_ref[...])
    o_ref[...] = _layer_norm(y, g_ref[...], b_ref[...])


def _moe_combine(slot1, slot2, ys, x1, route, g, b, *, tm, alpha):
    r, d = x1.shape
    row = lambda i, s1, s2: (i, 0)
    const = lambda i, s1, s2: (0, 0)
    return pl.pallas_call(
        functools.partial(_combine_kernel, alpha=alpha),
        out_shape=jax.ShapeDtypeStruct((r, d), F32),
        grid_spec=pltpu.PrefetchScalarGridSpec(
            num_scalar_prefetch=2,
            grid=(r // tm,),
            in_specs=[
                pl.BlockSpec(memory_space=pl.ANY),
                pl.BlockSpec((tm, d), row),
                pl.BlockSpec((tm, LANES), row),
                pl.BlockSpec((1, d), const),
                pl.BlockSpec((1, d), const),
            ],
            out_specs=pl.BlockSpec((tm, d), row),
            scratch_shapes=[pltpu.VMEM((tm, d), F32), pltpu.VMEM((tm, d), F32),
                            pltpu.SemaphoreType.DMA((2, DMA_RING))],
        ),
        compiler_params=_cparams(1),
        name="moe_combine",
    )(slot1, slot2, ys, x1, route, g, b)


def _moe(x1, route, counts, wg, wu, wd, g, b, *, tm, tf, alpha):
    r = x1.shape[0]
    n_work = (2 * r + MOE_BLOCK - 1) // MOE_BLOCK + N_EXPERTS
    slot1, slot2, work_exp, n_active = _routing_tables(route, counts, n_work=n_work)
    xs = _moe_dispatch(slot1, slot2, x1, n_slots=n_work * MOE_BLOCK)
    ys = _moe_ffn(work_exp, n_active, xs, wg, wu, wd, tf=tf)
    return _moe_combine(slot1, slot2, ys, x1, route, g, b, tm=tm, alpha=alpha)


def _rope_tables(pos):
    half = HEAD_DIM // 2
    inv = ROPE_THETA ** (-jnp.arange(half, dtype=F32) / half)
    ang = pos.astype(F32)[:, None] * inv[None, :]
    cos = jnp.cos(ang)
    sin = jnp.sin(ang)
    cos2 = jnp.concatenate([cos, cos, cos, cos], axis=1)
    sin2 = jnp.concatenate([-sin, sin, -sin, sin], axis=1)
    return cos2, sin2


def _row_tile(r, cap):
    best = 16
    for t in range(16, cap + 1, 16):
        if r % t == 0:
            best = t
    return best


def kernel(x_prompt, x_sample, cache_k, cache_v, state_pool, w_in, w_out, attn_sinks, pool_w,
           pool_scale, sg_w, sg_b, sg_norm_g, sg_norm_b, ln1_g, ln1_b, ln2_g, ln2_b,
           ffn_w_gate, ffn_w_up, ffn_w_down, router_w, router_b, moe_w_gate, moe_w_up, moe_w_down):
    bp, tp, d = x_prompt.shape
    bs, ts, _ = x_sample.shape
    depth = w_in.shape[0]
    d_kv = cache_k.shape[3] * cache_k.shape[4]
    d_pool = pool_scale.shape[1]
    d_sg = sg_norm_g.shape[1]
    d_attn = w_in.shape[2] - 2 * d_kv - d_pool - 2 * d_sg
    win_buf = cache_k.shape[2]
    assert win_buf == WINDOW == BLK and tp % BLK == 0 and tp >= BLK and ts <= SAMPLE_PAD
    assert bs % SEQ_PER_STEP == 0 and state_pool.shape[2] == POOL_STATE
    assert cache_k.shape[3] == N_KV_HEADS and cache_k.shape[4] == HEAD_DIM
    alpha = (2 * depth) ** 0.25

    r_p = bp * tp
    r_s = bs * SAMPLE_PAD
    r = r_p + r_s
    tm = _row_tile(r, 768)
    tm_out = _row_tile(math.gcd(r_p, r_s), 256)
    tm_comb = _row_tile(r, 256)

    xs_pad = jnp.pad(x_sample, ((0, 0), (0, SAMPLE_PAD - ts), (0, 0)))
    x = jnp.concatenate([x_prompt.reshape(r_p, d), xs_pad.reshape(r_s, d)], axis=0)

    pos = jnp.concatenate([
        jnp.tile(jnp.arange(tp, dtype=jnp.int32), bp),
        jnp.tile(PAST_LEN + jnp.arange(SAMPLE_PAD, dtype=jnp.int32), bs)])
    cos, sin = _rope_tables(pos)

    w_in_b = w_in.astype(BF16)
    w_out_b = w_out.astype(BF16)
    pool_w_b = pool_w.astype(BF16)
    ffn_g, ffn_u, ffn_d = (w.astype(BF16) for w in (ffn_w_gate, ffn_w_up, ffn_w_down))
    moe_g, moe_u, moe_d = (w.astype(BF16) for w in (moe_w_gate, moe_w_up, moe_w_down))
    rw_pad = jnp.pad(router_w, ((0, 0), (0, 0), (0, LANES - router_w.shape[2])))
    rw_hi = rw_pad.astype(BF16)
    rw_lo = (rw_pad - rw_hi.astype(F32)).astype(BF16)
    rb_pad = jnp.pad(router_b, ((0, 0), (0, LANES - router_b.shape[1])))[:, None, :]
    sg_bias = jnp.repeat(jnp.swapaxes(sg_b, 1, 2), d_sg // N_SG_GROUPS, axis=2)
    state_pad = jnp.pad(state_pool, ((0, 0), (0, 0), (1, 0), (0, 0)))

    kp_l, vp_l, pp_l, ks_l, vs_l, ps_l, sg_l = [], [], [], [], [], [], []
    for l in range(depth):
        q, k, v, xp, u, vn = _inproj(
            x, w_in_b[l], cos, sin, sg_norm_g[l][None], sg_norm_b[l][None],
            tm=tm, d_attn=d_attn, d_kv=d_kv, d_pool=d_pool, d_sg=d_sg)
        mixed_p = _mix_prompt(attn_sinks[l], q, k, v, xp, u, vn, pool_w_b[l], pool_scale[l][None],
                              sg_w[l], sg_bias[l], batch=bp, seq=tp)
        mixed_s = _mix_sample(attn_sinks[l], q, k, v, xp, u, vn,
                              cache_k[l].reshape(bs * win_buf, d_kv), cache_v[l].reshape(bs * win_buf, d_kv),
                              state_pad[l], pool_w_b[l], pool_scale[l][None], sg_w[l], sg_bias[l],
                              r_prompt=r_p, n_seq=bs)
        i = l // 2
        if l % 2 == 0:
            x1, x1b = _outproj(mixed_p, mixed_s, x, w_out_b[l], ln1_g[l][None], ln1_b[l][None],
                               tm=tm_out, alpha=alpha)
            x = _ffn(x1b, x1, ffn_g[i], ffn_u[i], ffn_d[i],
                     ln2_g[l][None], ln2_b[l][None], tm=tm, tf=512, alpha=alpha)
        else:
            x1, route, counts = _outproj(mixed_p, mixed_s, x, w_out_b[l], ln1_g[l][None], ln1_b[l][None],
                                         router=(rw_hi[i], rw_lo[i], rb_pad[i]), tm=tm_out, alpha=alpha)
            x = _moe(x1, route, counts, moe_g[i], moe_u[i], moe_d[i],
                     ln2_g[l][None], ln2_b[l][None], tm=tm_comb, tf=512, alpha=alpha)

        kh = (N_KV_HEADS, HEAD_DIM)
        kp_l.append(k[:r_p].reshape(bp, tp, *kh)[:, -WINDOW:])
        vp_l.append(v[:r_p].reshape(bp, tp, *kh)[:, -WINDOW:])
        pp_l.append(xp[:r_p].reshape(bp, tp, d_pool)[:, -POOL_STATE:])
        k_new = k[r_p:].reshape(bs, SAMPLE_PAD, *kh)[:, :ts]
        v_new = v[r_p:].reshape(bs, SAMPLE_PAD, *kh)[:, :ts]
        ks_l.append(jnp.concatenate([cache_k[l], k_new], axis=1)[:, -win_buf:])
        vs_l.append(jnp.concatenate([cache_v[l], v_new], axis=1)[:, -win_buf:])
        xp_new = xp[r_p:].reshape(bs, SAMPLE_PAD, d_pool)[:, :ts]
        ps_l.append(jnp.concatenate([state_pool[l], xp_new], axis=1)[:, -POOL_STATE:])
        sg_l.append(vn[r_p:].reshape(bs, SAMPLE_PAD, d_sg)[:, :ts])

    y_prompt = x[:r_p].reshape(bp, tp, d)
    y_sample = x[r_p:].reshape(bs, SAMPLE_PAD, d)[:, :ts]
    return (y_prompt, y_sample, jnp.stack(kp_l), jnp.stack(vp_l), jnp.stack(pp_l),
            jnp.stack(ks_l), jnp.stack(vs_l), jnp.stack(ps_l), jnp.stack(sg_l))
```

```python
import functools
import math

import jax
import jax.numpy as jnp
from jax import lax
from jax.experimental import pallas as pl
from jax.experimental.pallas import tpu as pltpu

F32 = jnp.float32
BF16 = jnp.bfloat16

PAST_LEN = 16384
HEAD_DIM = 64
N_KV_HEADS = 2
WINDOW = 128
BLK = 128
ROPE_THETA = 10000.0
POOL_WINDOWS = (2, 4, 8, 16)
POOL_STATE = 15
N_SG_GROUPS = 4
N_EXPERTS = 8
LN_EPS = 1e-5

LANES = 128
SUBLANES = 8
SAMPLE_PAD = SUBLANES
VMEM_LIMIT = 56 * 1024 * 1024

NEG_INF = float("-inf")


def _cparams(n_axes):
    return pltpu.CompilerParams(
        dimension_semantics=("arbitrary",) * n_axes, vmem_limit_bytes=VMEM_LIMIT)


def _layer_norm(x, g, b):
    mu = jnp.mean(x, axis=-1, keepdims=True)
    xc = x - mu
    var = jnp.mean(xc * xc, axis=-1, keepdims=True)
    return xc * lax.rsqrt(var + LN_EPS) * g + b


def _gelu_tanh(x):
    return 0.5 * x * (1.0 + jnp.tanh(0.7978845608028654 * (x + 0.044715 * (x * x * x))))


def _inproj_kernel(x_ref, w_ref, cos_ref, sin_ref, g_ref, b_ref,
                   q_ref, k_ref, v_ref, xp_ref, u_ref, vn_ref, *, d_attn, d_kv, d_pool, d_sg):
    xb = x_ref[...].astype(BF16)
    cos = cos_ref[...]
    sin = sin_ref[...]
    tm = xb.shape[0]
    lane = lax.broadcasted_iota(jnp.int32, (tm, LANES), 1)
    first_half = (lane % HEAD_DIM) < (HEAD_DIM // 2)

    def rope(z):
        rot = jnp.where(first_half, pltpu.roll(z, LANES - HEAD_DIM // 2, 1),
                        pltpu.roll(z, HEAD_DIM // 2, 1))
        return z * cos + rot * sin

    c0 = 0
    zq = jnp.dot(xb, w_ref[:, c0:c0 + d_attn], preferred_element_type=F32)
    for j in range(d_attn // LANES):
        sl = slice(j * LANES, (j + 1) * LANES)
        q_ref[:, sl] = (rope(zq[:, sl]) * (HEAD_DIM ** -0.5)).astype(BF16)
    c0 += d_attn
    zkv = jnp.dot(xb, w_ref[:, c0:c0 + 2 * d_kv], preferred_element_type=F32)
    k_ref[...] = rope(zkv[:, :d_kv])
    v_ref[...] = zkv[:, d_kv:]
    c0 += 2 * d_kv
    xp_ref[...] = jnp.dot(xb, w_ref[:, c0:c0 + d_pool], preferred_element_type=F32)
    c0 += d_pool
    u_ref[...] = _gelu_tanh(jnp.dot(xb, w_ref[:, c0:c0 + d_sg], preferred_element_type=F32))
    c0 += d_sg
    vg = _gelu_tanh(jnp.dot(xb, w_ref[:, c0:c0 + d_sg], preferred_element_type=F32))
    gd = d_sg // N_SG_GROUPS
    for g in range(N_SG_GROUPS):
        sl = slice(g * gd, (g + 1) * gd)
        vn_ref[:, sl] = _layer_norm(vg[:, sl], g_ref[:, sl], b_ref[:, sl])


def _inproj(x, w, cos, sin, sg_g, sg_b, *, tm, d_attn, d_kv, d_pool, d_sg):
    r, d = x.shape
    d_in = w.shape[1]
    row = lambda i: (i, 0)
    const = lambda i: (0, 0)
    outs = (
        jax.ShapeDtypeStruct((r, d_attn), BF16),
        jax.ShapeDtypeStruct((r, d_kv), F32),
        jax.ShapeDtypeStruct((r, d_kv), F32),
        jax.ShapeDtypeStruct((r, d_pool), F32),
        jax.ShapeDtypeStruct((r, d_sg), F32),
        jax.ShapeDtypeStruct((r, d_sg), F32),
    )
    return pl.pallas_call(
        functools.partial(_inproj_kernel, d_attn=d_attn, d_kv=d_kv, d_pool=d_pool, d_sg=d_sg),
        out_shape=outs,
        grid=(r // tm,),
        in_specs=[
            pl.BlockSpec((tm, d), row),
            pl.BlockSpec((d, d_in), const, pipeline_mode=pl.Buffered(1)),
            pl.BlockSpec((tm, LANES), row),
            pl.BlockSpec((tm, LANES), row),
            pl.BlockSpec((1, d_sg), const),
            pl.BlockSpec((1, d_sg), const),
        ],
        out_specs=(
            pl.BlockSpec((tm, d_attn), row),
            pl.BlockSpec((tm, d_kv), row),
            pl.BlockSpec((tm, d_kv), row),
            pl.BlockSpec((tm, d_pool), row),
            pl.BlockSpec((tm, d_sg), row),
            pl.BlockSpec((tm, d_sg), row),
        ),
        compiler_params=_cparams(1),
        name="inproj",
    )(x, w, cos, sin, sg_g, sg_b)


def _split_kv(kv):
    lane = lax.broadcasted_iota(jnp.int32, kv.shape, 1)
    low = lane < HEAD_DIM
    a = kv.astype(BF16)
    b = pltpu.roll(kv, HEAD_DIM, 1).astype(BF16)
    zero = jnp.zeros_like(a)
    head0 = (jnp.where(low, a, zero), jnp.where(low, zero, b))
    head1 = (jnp.where(low, b, zero), jnp.where(low, zero, a))
    return head0, head1


def _attend(q, k_pair, v_pair, allowed, sinks):
    out = None
    for kx, vx, sk in zip(k_pair, v_pair, sinks):
        s = lax.dot_general(q, kx, (((1,), (1,)), ((), ())), preferred_element_type=F32)
        s = jnp.where(allowed, s, NEG_INF)
        m = jnp.maximum(jnp.max(s, axis=1, keepdims=True), sk)
        p = jnp.exp(s - m)
        denom = jnp.sum(p, axis=1, keepdims=True) + jnp.exp(sk - m)
        o = jnp.dot(p.astype(BF16), vx, preferred_element_type=F32) * (1.0 / denom)
        out = o if out is None else out + o
    return out


def _window_sums(full):
    sums = {1: full}
    w = 1
    while w < max(POOL_WINDOWS):
        sums[2 * w] = sums[w] + pltpu.roll(sums[w], w, 0)
        w *= 2
    return sums


def _pool_mix(full, cur, cnt_of, pw_ref, ps_ref, row0):
    n = cur.shape[0]
    gd = cur.shape[1] // len(POOL_WINDOWS)
    outs = []
    for g, w in enumerate(POOL_WINDOWS):
        sl = slice(g * gd, (g + 1) * gd)
        sums = _window_sums(full[:, sl])[w]
        pooled = sums[row0:row0 + n, :] / cnt_of(w)
        d = (pooled - cur[:, sl]).astype(BF16)
        outs.append(jnp.dot(d, pw_ref[g], preferred_element_type=F32) * ps_ref[:, sl])
    return outs


def _causal_weights(sw_ref, g):
    n = sw_ref.shape[1]
    ri = lax.broadcasted_iota(jnp.int32, (n, n), 0)
    ci = lax.broadcasted_iota(jnp.int32, (n, n), 1)
    return jnp.where(ri >= ci, sw_ref[g], 0.0).astype(BF16)


def _mix_prompt_kernel(sink_ref, q_ref, kc_ref, kp_ref, vc_ref, vp_ref, xc_ref, xt_ref, u_ref,
                       vn_ref, pw_ref, ps_ref, sw_ref, sb_ref, o_ref, *, d_attn, d_pool, d_sg):
    i = pl.program_id(1)
    has_prev = i > 0

    kall = jnp.concatenate([kp_ref[...], kc_ref[...]], axis=0)
    vall = jnp.concatenate([vp_ref[...], vc_ref[...]], axis=0)
    k_heads = _split_kv(kall)
    v_heads = _split_kv(vall)
    ri = lax.broadcasted_iota(jnp.int32, (BLK, 2 * BLK), 0)
    cj = lax.broadcasted_iota(jnp.int32, (BLK, 2 * BLK), 1)
    first_key = jnp.where(has_prev, 0, BLK)
    allowed = (cj >= ri) & (cj <= ri + WINDOW) & (cj >= first_key)
    n_pairs = d_attn // LANES
    pairs_per_kv = n_pairs // N_KV_HEADS
    for p in range(n_pairs):
        g = p // pairs_per_kv
        sl = slice(p * LANES, (p + 1) * LANES)
        o = _attend(q_ref[:, sl], k_heads[g], v_heads[g], allowed,
                    (sink_ref[2 * p], sink_ref[2 * p + 1]))
        o_ref[:, sl] = o.astype(BF16)

    xc = xc_ref[...]
    tail = jnp.where(has_prev, xt_ref[...], 0.0)
    full = jnp.concatenate([tail, xc], axis=0)
    hist = tail.shape[0]
    pos = i * BLK + lax.broadcasted_iota(jnp.int32, (BLK, 1), 0)
    cnt_of = lambda w: jnp.minimum(pos + 1, w).astype(F32)
    pooled = _pool_mix(full, xc, cnt_of, pw_ref, ps_ref, hist)
    gd = d_pool // len(POOL_WINDOWS)
    for g, y in enumerate(pooled):
        o_ref[:, d_attn + g * gd:d_attn + (g + 1) * gd] = y.astype(BF16)

    gs = d_sg // N_SG_GROUPS
    for g in range(N_SG_GROUPS):
        sl = slice(g * gs, (g + 1) * gs)
        s = jnp.dot(_causal_weights(sw_ref, g), vn_ref[:, sl].astype(BF16),
                    preferred_element_type=F32) + sb_ref[:, sl]
        c0 = d_attn + d_pool + g * gs
        o_ref[:, c0:c0 + gs] = (u_ref[:, sl] * s).astype(BF16)


def _mix_prompt(sinks, q, k, v, xp, u, vn, pool_w, pool_scale, sg_w, sg_bias, *, batch, seq):
    nb = seq // BLK
    d_attn, d_kv, d_pool, d_sg = q.shape[1], k.shape[1], xp.shape[1], u.shape[1]
    d_mix = d_attn + d_pool + d_sg
    hist = 2 * SUBLANES
    cur = lambda b, i: (b * nb + i, 0)
    prev = lambda b, i: (b * nb + jnp.maximum(i - 1, 0), 0)
    tail = lambda b, i: (jnp.maximum((b * nb + i) * (BLK // hist) - 1, 0), 0)
    const2 = lambda b, i: (0, 0)
    const3 = lambda b, i: (0, 0, 0)
    return pl.pallas_call(
        functools.partial(_mix_prompt_kernel, d_attn=d_attn, d_pool=d_pool, d_sg=d_sg),
        out_shape=jax.ShapeDtypeStruct((batch * seq, d_mix), BF16),
        grid=(batch, nb),
        in_specs=[
            pl.BlockSpec(memory_space=pltpu.SMEM),
            pl.BlockSpec((BLK, d_attn), cur),
            pl.BlockSpec((BLK, d_kv), cur),
            pl.BlockSpec((BLK, d_kv), prev),
            pl.BlockSpec((BLK, d_kv), cur),
            pl.BlockSpec((BLK, d_kv), prev),
            pl.BlockSpec((BLK, d_pool), cur),
            pl.BlockSpec((hist, d_pool), tail),
            pl.BlockSpec((BLK, d_sg), cur),
            pl.BlockSpec((BLK, d_sg), cur),
            pl.BlockSpec(pool_w.shape, const3),
            pl.BlockSpec((1, d_pool), const2),
            pl.BlockSpec(sg_w.shape, const3),
            pl.BlockSpec((BLK, d_sg), const2),
        ],
        out_specs=pl.BlockSpec((BLK, d_mix), cur),
        compiler_params=_cparams(2),
        name="mix_prompt",
    )(sinks, q, k, k, v, v, xp, xp, u, vn, pool_w, pool_scale, sg_w, sg_bias)


SEQ_PER_STEP = 2


def _mix_sample_kernel(sink_ref, q_ref, kn_ref, vn_new_ref, ck_ref, cv_ref, xn_ref, st_ref,
                       u_ref, vn_ref, pw_ref, ps_ref, sw_ref, sb_ref, o_ref, *, d_attn, d_pool, d_sg):
    sp = SAMPLE_PAD
    n_pairs = d_attn // LANES
    pairs_per_kv = n_pairs // N_KV_HEADS
    m_rows = pairs_per_kv * sp
    qf = q_ref[...].astype(F32)

    ri = lax.broadcasted_iota(jnp.int32, (m_rows, 2 * BLK), 0) % sp
    cj = lax.broadcasted_iota(jnp.int32, (m_rows, 2 * BLK), 1)
    allowed = (cj >= ri) & (cj <= ri + WINDOW)
    row_pair = lax.broadcasted_iota(jnp.int32, (m_rows, 1), 0) // sp

    attn_rows = []
    for s in range(SEQ_PER_STEP):
        rows = slice(s * sp, (s + 1) * sp)
        zpad = jnp.zeros((BLK - sp, kn_ref.shape[1]), F32)
        kall = jnp.concatenate([ck_ref[s * BLK:(s + 1) * BLK, :], kn_ref[rows, :], zpad], axis=0)
        vall = jnp.concatenate([cv_ref[s * BLK:(s + 1) * BLK, :], vn_new_ref[rows, :], zpad], axis=0)
        k_heads = _split_kv(kall)
        v_heads = _split_kv(vall)
        per_pair = []
        for g in range(N_KV_HEADS):
            qst = jnp.concatenate(
                [qf[rows, (g * pairs_per_kv + pl_) * LANES:(g * pairs_per_kv + pl_ + 1) * LANES]
                 for pl_ in range(pairs_per_kv)], axis=0).astype(BF16)
            sink_lo = jnp.zeros((m_rows, 1), F32)
            sink_hi = jnp.zeros((m_rows, 1), F32)
            for pl_ in range(pairs_per_kv):
                h = 2 * (g * pairs_per_kv + pl_)
                sink_lo = jnp.where(row_pair == pl_, sink_ref[h], sink_lo)
                sink_hi = jnp.where(row_pair == pl_, sink_ref[h + 1], sink_hi)
            o = _attend(qst, k_heads[g], v_heads[g], allowed, (sink_lo, sink_hi))
            per_pair.extend(o[pl_ * sp:(pl_ + 1) * sp, :] for pl_ in range(pairs_per_kv))
        attn_rows.append(per_pair)
    for p in range(n_pairs):
        o_ref[:, p * LANES:(p + 1) * LANES] = jnp.concatenate(
            [attn_rows[s][p] for s in range(SEQ_PER_STEP)], axis=0).astype(BF16)

    gd = d_pool // len(POOL_WINDOWS)
    pooled = []
    for s in range(SEQ_PER_STEP):
        rows = slice(s * sp, (s + 1) * sp)
        xc = xn_ref[rows, :]
        full = jnp.concatenate([st_ref[s], xc], axis=0)
        cnt_of = lambda w: float(w)
        pooled.append(_pool_mix(full, xc, cnt_of, pw_ref, ps_ref, st_ref.shape[1]))
    for g in range(len(POOL_WINDOWS)):
        o_ref[:, d_attn + g * gd:d_attn + (g + 1) * gd] = jnp.concatenate(
            [pooled[s][g] for s in range(SEQ_PER_STEP)], axis=0).astype(BF16)

    gs = d_sg // N_SG_GROUPS
    for g in range(N_SG_GROUPS):
        sl = slice(g * gs, (g + 1) * gs)
        wm = _causal_weights(sw_ref, g)
        outs = []
        for s in range(SEQ_PER_STEP):
            rows = slice(s * sp, (s + 1) * sp)
            vpad = jnp.concatenate([vn_ref[rows, sl], jnp.zeros((BLK - sp, gs), F32)], axis=0)
            sg = jnp.dot(wm, vpad.astype(BF16), preferred_element_type=F32)[:sp, :] + sb_ref[:sp, sl]
            outs.append(u_ref[rows, sl] * sg)
        c0 = d_attn + d_pool + g * gs
        o_ref[:, c0:c0 + gs] = jnp.concatenate(outs, axis=0).astype(BF16)


def _mix_sample(sinks, q, k, v, xp, u, vn, cache_k, cache_v, state, pool_w, pool_scale,
                sg_w, sg_bias, *, r_prompt, n_seq):
    d_attn, d_kv, d_pool, d_sg = q.shape[1], k.shape[1], xp.shape[1], u.shape[1]
    d_mix = d_attn + d_pool + d_sg
    rows = SEQ_PER_STEP * SAMPLE_PAD
    base = r_prompt // rows
    new = lambda i: (base + i, 0)
    per_seq = lambda i: (i, 0)
    const2 = lambda i: (0, 0)
    const3 = lambda i: (0, 0, 0)
    return pl.pallas_call(
        functools.partial(_mix_sample_kernel, d_attn=d_attn, d_pool=d_pool, d_sg=d_sg),
        out_shape=jax.ShapeDtypeStruct((n_seq * SAMPLE_PAD, d_mix), BF16),
        grid=(n_seq // SEQ_PER_STEP,),
        in_specs=[
            pl.BlockSpec(memory_space=pltpu.SMEM),
            pl.BlockSpec((rows, d_attn), new),
            pl.BlockSpec((rows, d_kv), new),
            pl.BlockSpec((rows, d_kv), new),
            pl.BlockSpec((SEQ_PER_STEP * BLK, d_kv), per_seq),
            pl.BlockSpec((SEQ_PER_STEP * BLK, d_kv), per_seq),
            pl.BlockSpec((rows, d_pool), new),
            pl.BlockSpec((SEQ_PER_STEP,) + state.shape[1:], lambda i: (i, 0, 0)),
            pl.BlockSpec((rows, d_sg), new),
            pl.BlockSpec((rows, d_sg), new),
            pl.BlockSpec(pool_w.shape, const3),
            pl.BlockSpec((1, d_pool), const2),
            pl.BlockSpec(sg_w.shape, const3),
            pl.BlockSpec((BLK, d_sg), const2),
        ],
        out_specs=pl.BlockSpec((rows, d_mix), per_seq),
        compiler_params=_cparams(1),
        name="mix_sample",
    )(sinks, q, k, v, cache_k, cache_v, xp, state, u, vn, pool_w, pool_scale, sg_w, sg_bias)


ROUTE_E1, ROUTE_E2, ROUTE_G1, ROUTE_G2, ROUTE_R1, ROUTE_R2 = range(6)

SLAB_PITCH = 24


def _lane_pick(rec, k):
    lane = lax.broadcasted_iota(jnp.int32, rec.shape, 1)
    return jnp.sum(jnp.where(lane == k, rec, 0.0), axis=1, keepdims=True)


def _store_slabs(slab_ref, x):
    n, d = x.shape
    slab_ref[...] = jnp.zeros_like(slab_ref)
    for j in range(d // LANES):
        slab_ref[pl.ds(j, n, stride=SLAB_PITCH), :] = x[:, j * LANES:(j + 1) * LANES]


def _load_slab_chunk(slab_ref, j, n):
    return slab_ref[pl.ds(j, n, stride=SLAB_PITCH), :]


def _outproj_kernel(*refs, alpha, with_router, n_prompt_tiles):
    if with_router:
        (mp_ref, ms_ref, x_ref, w_ref, g_ref, b_ref, rwh_ref, rwl_ref, rb_ref,
         x1_ref, x1s_ref, route_ref, cnt_ref, carry_ref) = refs
    else:
        mp_ref, ms_ref, x_ref, w_ref, g_ref, b_ref, x1_ref, x1b_ref = refs
    mixed = jnp.where(pl.program_id(0) < n_prompt_tiles, mp_ref[...], ms_ref[...])
    y = alpha * x_ref[...] + jnp.dot(mixed, w_ref[...], preferred_element_type=F32)
    x1 = _layer_norm(y, g_ref[...], b_ref[...])
    x1_ref[...] = x1
    hi = x1.astype(BF16)
    if not with_router:
        x1b_ref[...] = hi
        return
    _store_slabs(x1s_ref, x1)

    lo = (x1 - hi.astype(F32)).astype(BF16)
    logits = (jnp.dot(hi, rwh_ref[...], preferred_element_type=F32)
              + jnp.dot(lo, rwh_ref[...], preferred_element_type=F32)
              + jnp.dot(hi, rwl_ref[...], preferred_element_type=F32) + rb_ref[...])
    lane_i = lax.broadcasted_iota(jnp.int32, logits.shape, 1)
    lane = lane_i.astype(F32)
    logits = jnp.where(lane_i < N_EXPERTS, logits, NEG_INF)
    m1 = jnp.max(logits, axis=1, keepdims=True)
    i1 = jnp.min(jnp.where(logits == m1, lane, float(LANES)), axis=1, keepdims=True)
    first = lane == i1
    rest = jnp.where(first, NEG_INF, logits)
    m2 = jnp.max(rest, axis=1, keepdims=True)
    i2 = jnp.min(jnp.where(rest == m2, lane, float(LANES)), axis=1, keepdims=True)
    second = lane == i2
    e = jnp.exp(m2 - m1)
    g1 = 1.0 / (1.0 + e)
    g2 = e * g1

    @pl.when(pl.program_id(0) == 0)
    def _():
        carry_ref[...] = jnp.zeros_like(carry_ref)

    sel = jnp.where(first, 1.0, 0.0) + jnp.where(second, 1.0, 0.0)
    tm = sel.shape[0]
    ri = lax.broadcasted_iota(jnp.int32, (tm, tm), 0)
    ci = lax.broadcasted_iota(jnp.int32, (tm, tm), 1)
    earlier = jnp.where(ri > ci, 1.0, 0.0).astype(BF16)
    ranks = jnp.dot(earlier, sel.astype(BF16), preferred_element_type=F32) + carry_ref[...]
    r1 = jnp.sum(jnp.where(first, ranks, 0.0), axis=1, keepdims=True)
    r2 = jnp.sum(jnp.where(second, ranks, 0.0), axis=1, keepdims=True)
    carry_ref[...] += jnp.sum(sel, axis=0, keepdims=True)
    cnt_ref[...] = carry_ref[...]

    rec = jnp.zeros_like(logits)
    for k, val in ((ROUTE_E1, i1), (ROUTE_E2, i2), (ROUTE_G1, g1), (ROUTE_G2, g2),
                   (ROUTE_R1, r1), (ROUTE_R2, r2)):
        rec = jnp.where(lane_i == k, val, rec)
    route_ref[...] = rec


def _outproj(mixed_p, mixed_s, x, w, g, b, router=None, *, tm, alpha):
    r, d = x.shape
    assert mixed_p.shape[0] % tm == 0 and mixed_s.shape[0] % tm == 0
    n_p = mixed_p.shape[0] // tm
    row = lambda i: (i, 0)
    const = lambda i: (0, 0)
    in_specs = [
        pl.BlockSpec((tm, mixed_p.shape[1]), lambda i: (jnp.minimum(i, n_p - 1), 0)),
        pl.BlockSpec((tm, mixed_s.shape[1]), lambda i: (jnp.maximum(i - n_p, 0), 0)),
        pl.BlockSpec((tm, d), row),
        pl.BlockSpec(w.shape, const, pipeline_mode=pl.Buffered(1)),
        pl.BlockSpec((1, d), const),
        pl.BlockSpec((1, d), const),
    ]
    args = [mixed_p, mixed_s, x, w, g, b]
    scratch = []
    if router is None:
        outs = [jax.ShapeDtypeStruct((r, d), F32), jax.ShapeDtypeStruct((r, d), BF16)]
        out_specs = [pl.BlockSpec((tm, d), row), pl.BlockSpec((tm, d), row)]
    else:
        in_specs += [pl.BlockSpec((d, LANES), const), pl.BlockSpec((d, LANES), const),
                     pl.BlockSpec((1, LANES), const)]
        args += list(router)
        outs = [jax.ShapeDtypeStruct((r, d), F32),
                jax.ShapeDtypeStruct((r * SLAB_PITCH, LANES), F32),
                jax.ShapeDtypeStruct((r, LANES), F32),
                jax.ShapeDtypeStruct((1, LANES), F32)]
        out_specs = [pl.BlockSpec((tm, d), row),
                     pl.BlockSpec((tm * SLAB_PITCH, LANES), row),
                     pl.BlockSpec((tm, LANES), row),
                     pl.BlockSpec((1, LANES), const)]
        scratch = [pltpu.VMEM((1, LANES), F32)]
    return pl.pallas_call(
        functools.partial(_outproj_kernel, alpha=alpha, with_router=router is not None,
                          n_prompt_tiles=n_p),
        out_shape=tuple(outs),
        grid=(r // tm,),
        in_specs=in_specs,
        out_specs=tuple(out_specs),
        scratch_shapes=scratch,
        compiler_params=_cparams(1),
        name="outproj_router" if router is not None else "outproj",
    )(*args)


def _swiglu(xb, wg_ref, wu_ref):
    a = jnp.dot(xb, wg_ref[...], preferred_element_type=F32)
    c = jnp.dot(xb, wu_ref[...], preferred_element_type=F32)
    return (a * (1.0 / (1.0 + jnp.exp(-a))) * c).astype(BF16)


def _ffn_kernel(xb_ref, x1_ref, wg_ref, wu_ref, wd_ref, g_ref, b_ref, o_ref, *, alpha):
    f = pl.program_id(1)

    @pl.when(f == 0)
    def _():
        o_ref[...] = alpha * x1_ref[...]

    o_ref[...] += jnp.dot(_swiglu(xb_ref[...], wg_ref, wu_ref), wd_ref[...],
                          preferred_element_type=F32)

    @pl.when(f == pl.num_programs(1) - 1)
    def _():
        o_ref[...] = _layer_norm(o_ref[...], g_ref[...], b_ref[...])


def _ffn(xb, x1, wg, wu, wd, g, b, *, tm, tf, alpha):
    r, d = x1.shape
    ff = wg.shape[1]
    row = lambda i, f: (i, 0)
    const = lambda i, f: (0, 0)
    return pl.pallas_call(
        functools.partial(_ffn_kernel, alpha=alpha),
        out_shape=jax.ShapeDtypeStruct((r, d), F32),
        grid=(r // tm, ff // tf),
        in_specs=[
            pl.BlockSpec((tm, d), row),
            pl.BlockSpec((tm, d), row),
            pl.BlockSpec((d, tf), lambda i, f: (0, f)),
            pl.BlockSpec((d, tf), lambda i, f: (0, f)),
            pl.BlockSpec((tf, d), lambda i, f: (f, 0)),
            pl.BlockSpec((1, d), const),
            pl.BlockSpec((1, d), const),
        ],
        out_specs=pl.BlockSpec((tm, d), row),
        compiler_params=_cparams(2),
        name="ffn",
    )(xb, x1, wg, wu, wd, g, b)


MOE_BLOCK = 512
DMA_RING = 16


def _routing_tables(route, counts, *, n_work):
    e1 = route[:, ROUTE_E1].astype(jnp.int32)
    e2 = route[:, ROUTE_E2].astype(jnp.int32)
    r1 = route[:, ROUTE_R1].astype(jnp.int32)
    r2 = route[:, ROUTE_R2].astype(jnp.int32)
    cnt = counts[0, :N_EXPERTS].astype(jnp.int32)
    nblk = (cnt + MOE_BLOCK - 1) // MOE_BLOCK
    blk_end = jnp.cumsum(nblk)
    start = (blk_end - nblk) * MOE_BLOCK
    slot1 = jnp.take(start, e1) + r1
    slot2 = jnp.take(start, e2) + r2
    n_active = blk_end[-1]
    blk = jnp.minimum(jnp.arange(n_work, dtype=jnp.int32), n_active - 1)
    work_exp = jnp.sum(blk[:, None] >= blk_end[None, :], axis=1).astype(jnp.int32)
    return slot1, slot2, work_exp, n_active.reshape(1)


def _ring_copies(n, make_copies):
    def start(i):
        for c in make_copies(i):
            c.start()

    def wait(i):
        for c in make_copies(i):
            c.wait()

    def fill(i, carry):
        start(i)
        return carry

    def steady(i, carry):
        wait(i - DMA_RING)
        start(i)
        return carry

    def drain(i, carry):
        wait(i)
        return carry

    lax.fori_loop(0, DMA_RING, fill, 0)
    lax.fori_loop(DMA_RING, n, steady, 0)
    lax.fori_loop(n - DMA_RING, n, drain, 0)


SLAB_ROWS = 16


def _slab(ref, token):
    return ref.at[pl.ds(pl.multiple_of(token * SLAB_PITCH, SUBLANES), SLAB_ROWS), :]


def _dispatch_kernel(s1_ref, s2_ref, x1s_ref, xs_init, xs_hbm, sem):
    del xs_init
    tm = x1s_ref.shape[0] // SLAB_PITCH
    base = pl.program_id(0) * tm

    def copies(r):
        k = r & (DMA_RING - 1)
        src = _slab(x1s_ref, r)
        return (pltpu.make_async_copy(src, _slab(xs_hbm, s1_ref[base + r]), sem.at[0, k]),
                pltpu.make_async_copy(src, _slab(xs_hbm, s2_ref[base + r]), sem.at[1, k]))

    _ring_copies(tm, copies)


def _moe_dispatch(slot1, slot2, x1s, *, tm, n_slots):
    r = x1s.shape[0] // SLAB_PITCH
    return pl.pallas_call(
        _dispatch_kernel,
        out_shape=jax.ShapeDtypeStruct((n_slots * SLAB_PITCH, LANES), F32),
        grid_spec=pltpu.PrefetchScalarGridSpec(
            num_scalar_prefetch=2,
            grid=(r // tm,),
            in_specs=[pl.BlockSpec((tm * SLAB_PITCH, LANES), lambda i, s1, s2: (i, 0)),
                      pl.BlockSpec(memory_space=pl.ANY)],
            out_specs=pl.BlockSpec(memory_space=pl.ANY),
            scratch_shapes=[pltpu.SemaphoreType.DMA((2, DMA_RING))],
        ),
        input_output_aliases={3: 0},
        compiler_params=_cparams(1),
        name="moe_dispatch",
    )(slot1, slot2, x1s, jnp.zeros((n_slots * SLAB_PITCH, LANES), F32))


def _moe_ffn_kernel(we_ref, na_ref, xs_ref, wg_ref, wu_ref, wd_ref, ys_ref, xb_ref, acc_ref):
    del we_ref
    w = pl.program_id(0)
    f = pl.program_id(1)
    n, d = acc_ref.shape

    @pl.when(f == 0)
    def _():
        for j in range(d // LANES):
            xb_ref[:, j * LANES:(j + 1) * LANES] = _load_slab_chunk(xs_ref, j, n).astype(BF16)
        acc_ref[...] = jnp.zeros_like(acc_ref)

    @pl.when(w < na_ref[0])
    def _():
        acc_ref[...] += jnp.dot(_swiglu(xb_ref[...], wg_ref, wu_ref), wd_ref[...],
                                preferred_element_type=F32)

    @pl.when(f == pl.num_programs(1) - 1)
    def _():
        _store_slabs(ys_ref, acc_ref[...])


def _moe_ffn(work_exp, n_active, xs, wg, wu, wd, *, tf):
    d = wg.shape[1]
    ff = wg.shape[2]
    nf = ff // tf
    rows = lambda w, f, we, na: (w, 0)
    chunk = lambda w, f, na: jnp.where(w < na[0], f, nf - 1)
    return pl.pallas_call(
        _moe_ffn_kernel,
        out_shape=jax.ShapeDtypeStruct(xs.shape, F32),
        grid_spec=pltpu.PrefetchScalarGridSpec(
            num_scalar_prefetch=2,
            grid=(xs.shape[0] // (MOE_BLOCK * SLAB_PITCH), nf),
            in_specs=[
                pl.BlockSpec((MOE_BLOCK * SLAB_PITCH, LANES), rows),
                pl.BlockSpec((None, d, tf), lambda w, f, we, na: (we[w], 0, chunk(w, f, na))),
                pl.BlockSpec((None, d, tf), lambda w, f, we, na: (we[w], 0, chunk(w, f, na))),
                pl.BlockSpec((None, tf, d), lambda w, f, we, na: (we[w], chunk(w, f, na), 0)),
            ],
            out_specs=pl.BlockSpec((MOE_BLOCK * SLAB_PITCH, LANES), rows),
            scratch_shapes=[pltpu.VMEM((MOE_BLOCK, d), BF16), pltpu.VMEM((MOE_BLOCK, d), F32)],
        ),
        compiler_params=_cparams(2),
        name="moe_ffn",
    )(work_exp, n_active, xs, wg, wu, wd)


def _combine_kernel(s1_ref, s2_ref, ys_hbm, x1_ref, route_ref, g_ref, b_ref, o_ref,
                    y1_ref, y2_ref, sem, *, alpha):
    tm, d = o_ref.shape
    base = pl.program_id(0) * tm

    def copies(r):
        k = r & (DMA_RING - 1)
        return (pltpu.make_async_copy(_slab(ys_hbm, s1_ref[base + r]), _slab(y1_ref, r), sem.at[0, k]),
                pltpu.make_async_copy(_slab(ys_hbm, s2_ref[base + r]), _slab(y2_ref, r), sem.at[1, k]))

    _ring_copies(tm, copies)
    route = route_ref[...]
    g1 = _lane_pick(route, ROUTE_G1)
    g2 = _lane_pick(route, ROUTE_G2)
    for j in range(d // LANES):
        sl = slice(j * LANES, (j + 1) * LANES)
        o_ref[:, sl] = (alpha * x1_ref[:, sl] + g1 * _load_slab_chunk(y1_ref, j, tm)
                        + g2 * _load_slab_chunk(y2_ref, j, tm))
    o_ref[...] = _layer_norm(o_ref[...], g_ref[...], b_ref[...])


def _moe_combine(slot1, slot2, ys, x1, route, g, b, *, tm, alpha):
    r, d = x1.shape
    row = lambda i, s1, s2: (i, 0)
    const = lambda i, s1, s2: (0, 0)
    return pl.pallas_call(
        functools.partial(_combine_kernel, alpha=alpha),
        out_shape=jax.ShapeDtypeStruct((r, d), F32),
        grid_spec=pltpu.PrefetchScalarGridSpec(
            num_scalar_prefetch=2,
            grid=(r // tm,),
            in_specs=[
                pl.BlockSpec(memory_space=pl.ANY),
                pl.BlockSpec((tm, d), row),
                pl.BlockSpec((tm, LANES), row),
                pl.BlockSpec((1, d), const),
                pl.BlockSpec((1, d), const),
            ],
            out_specs=pl.BlockSpec((tm, d), row),
            scratch_shapes=[pltpu.VMEM((tm * SLAB_PITCH, LANES), F32),
                            pltpu.VMEM((tm * SLAB_PITCH, LANES), F32),
                            pltpu.SemaphoreType.DMA((2, DMA_RING))],
        ),
        compiler_params=_cparams(1),
        name="moe_combine",
    )(slot1, slot2, ys, x1, route, g, b)


def _moe(x1, x1s, route, counts, wg, wu, wd, g, b, *, tm, tf, alpha):
    r = x1.shape[0]
    n_work = (2 * r + MOE_BLOCK - 1) // MOE_BLOCK + N_EXPERTS
    slot1, slot2, work_exp, n_active = _routing_tables(route, counts, n_work=n_work)
    xs = _moe_dispatch(slot1, slot2, x1s, tm=tm, n_slots=n_work * MOE_BLOCK)
    ys = _moe_ffn(work_exp, n_active, xs, wg, wu, wd, tf=tf)
    return _moe_combine(slot1, slot2, ys, x1, route, g, b, tm=tm, alpha=alpha)


def _rope_tables(pos):
    half = HEAD_DIM // 2
    inv = ROPE_THETA ** (-jnp.arange(half, dtype=F32) / half)
    ang = pos.astype(F32)[:, None] * inv[None, :]
    cos = jnp.cos(ang)
    sin = jnp.sin(ang)
    cos2 = jnp.concatenate([cos, cos, cos, cos], axis=1)
    sin2 = jnp.concatenate([-sin, sin, -sin, sin], axis=1)
    return cos2, sin2


def _row_tile(r, cap):
    best = 16
    for t in range(16, cap + 1, 16):
        if r % t == 0:
            best = t
    return best


def kernel(x_prompt, x_sample, cache_k, cache_v, state_pool, w_in, w_out, attn_sinks, pool_w,
           pool_scale, sg_w, sg_b, sg_norm_g, sg_norm_b, ln1_g, ln1_b, ln2_g, ln2_b,
           ffn_w_gate, ffn_w_up, ffn_w_down, router_w, router_b, moe_w_gate, moe_w_up, moe_w_down):
    bp, tp, d = x_prompt.shape
    bs, ts, _ = x_sample.shape
    depth = w_in.shape[0]
    d_kv = cache_k.shape[3] * cache_k.shape[4]
    d_pool = pool_scale.shape[1]
    d_sg = sg_norm_g.shape[1]
    d_attn = w_in.shape[2] - 2 * d_kv - d_pool - 2 * d_sg
    win_buf = cache_k.shape[2]
    assert win_buf == WINDOW == BLK and tp % BLK == 0 and tp >= BLK and ts <= SAMPLE_PAD
    assert bs % SEQ_PER_STEP == 0 and state_pool.shape[2] == POOL_STATE
    assert cache_k.shape[3] == N_KV_HEADS and cache_k.shape[4] == HEAD_DIM
    assert d == SLAB_ROWS * LANES
    alpha = (2 * depth) ** 0.25

    r_p = bp * tp
    r_s = bs * SAMPLE_PAD
    r = r_p + r_s
    tm = _row_tile(r, 768)
    tm_out = _row_tile(math.gcd(r_p, r_s), 256)
    tm_comb = _row_tile(r, 256)

    xs_pad = jnp.pad(x_sample, ((0, 0), (0, SAMPLE_PAD - ts), (0, 0)))
    x = jnp.concatenate([x_prompt.reshape(r_p, d), xs_pad.reshape(r_s, d)], axis=0)

    pos = jnp.concatenate([
        jnp.tile(jnp.arange(tp, dtype=jnp.int32), bp),
        jnp.tile(PAST_LEN + jnp.arange(SAMPLE_PAD, dtype=jnp.int32), bs)])
    cos, sin = _rope_tables(pos)

    w_in_b = w_in.astype(BF16)
    w_out_b = w_out.astype(BF16)
    pool_w_b = pool_w.astype(BF16)
    ffn_g, ffn_u, ffn_d = (w.astype(BF16) for w in (ffn_w_gate, ffn_w_up, ffn_w_down))
    moe_g, moe_u, moe_d = (w.astype(BF16) for w in (moe_w_gate, moe_w_up, moe_w_down))
    rw_pad = jnp.pad(router_w, ((0, 0), (0, 0), (0, LANES - router_w.shape[2])))
    rw_hi = rw_pad.astype(BF16)
    rw_lo = (rw_pad - rw_hi.astype(F32)).astype(BF16)
    rb_pad = jnp.pad(router_b, ((0, 0), (0, LANES - router_b.shape[1])))[:, None, :]
    sg_bias = jnp.repeat(jnp.swapaxes(sg_b, 1, 2), d_sg // N_SG_GROUPS, axis=2)
    state_pad = jnp.pad(state_pool, ((0, 0), (0, 0), (1, 0), (0, 0)))

    kp_l, vp_l, pp_l, ks_l, vs_l, ps_l, sg_l = [], [], [], [], [], [], []
    for l in range(depth):
        q, k, v, xp, u, vn = _inproj(
            x, w_in_b[l], cos, sin, sg_norm_g[l][None], sg_norm_b[l][None],
            tm=tm, d_attn=d_attn, d_kv=d_kv, d_pool=d_pool, d_sg=d_sg)
        mixed_p = _mix_prompt(attn_sinks[l], q, k, v, xp, u, vn, pool_w_b[l], pool_scale[l][None],
                              sg_w[l], sg_bias[l], batch=bp, seq=tp)
        mixed_s = _mix_sample(attn_sinks[l], q, k, v, xp, u, vn,
                              cache_k[l].reshape(bs * win_buf, d_kv), cache_v[l].reshape(bs * win_buf, d_kv),
                              state_pad[l], pool_w_b[l], pool_scale[l][None], sg_w[l], sg_bias[l],
                              r_prompt=r_p, n_seq=bs)
        i = l // 2
        if l % 2 == 0:
            x1, x1b = _outproj(mixed_p, mixed_s, x, w_out_b[l], ln1_g[l][None], ln1_b[l][None],
                               tm=tm_out, alpha=alpha)
            x = _ffn(x1b, x1, ffn_g[i], ffn_u[i], ffn_d[i],
                     ln2_g[l][None], ln2_b[l][None], tm=tm, tf=512, alpha=alpha)
        else:
            x1, x1s, route, counts = _outproj(
                mixed_p, mixed_s, x, w_out_b[l], ln1_g[l][None], ln1_b[l][None],
                router=(rw_hi[i], rw_lo[i], rb_pad[i]), tm=tm_out, alpha=alpha)
            x = _moe(x1, x1s, route, counts, moe_g[i], moe_u[i], moe_d[i],
                     ln2_g[l][None], ln2_b[l][None], tm=tm_comb, tf=512, alpha=alpha)

        kh = (N_KV_HEADS, HEAD_DIM)
        kp_l.append(k[:r_p].reshape(bp, tp, *kh)[:, -WINDOW:])
        vp_l.append(v[:r_p].reshape(bp, tp, *kh)[:, -WINDOW:])
        pp_l.append(xp[:r_p].reshape(bp, tp, d_pool)[:, -POOL_STATE:])
        k_new = k[r_p:].reshape(bs, SAMPLE_PAD, *kh)[:, :ts]
        v_new = v[r_p:].reshape(bs, SAMPLE_PAD, *kh)[:, :ts]
        ks_l.append(jnp.concatenate([cache_k[l], k_new], axis=1)[:, -win_buf:])
        vs_l.append(jnp.concatenate([cache_v[l], v_new], axis=1)[:, -win_buf:])
        xp_new = xp[r_p:].reshape(bs, SAMPLE_PAD, d_pool)[:, :ts]
        ps_l.append(jnp.concatenate([state_pool[l], xp_new], axis=1)[:, -POOL_STATE:])
        sg_l.append(vn[r_p:].reshape(bs, SAMPLE_PAD, d_sg)[:, :ts])

    y_prompt = x[:r_p].reshape(bp, tp, d)
    y_sample = x[r_p:].reshape(bs, SAMPLE_PAD, d)[:, :ts]
    return (y_prompt, y_sample, jnp.stack(kp_l), jnp.stack(vp_l), jnp.stack(pp_l),
            jnp.stack(ks_l), jnp.stack(vs_l), jnp.stack(ps_l), jnp.stack(sg_l))
```

```python
import functools
import math

import jax
import jax.numpy as jnp
from jax import lax
from jax.experimental import pallas as pl
from jax.experimental.pallas import tpu as pltpu

F32 = jnp.float32
BF16 = jnp.bfloat16

PAST_LEN = 16384
HEAD_DIM = 64
N_KV_HEADS = 2
WINDOW = 128
BLK = 128
ROPE_THETA = 10000.0
POOL_WINDOWS = (2, 4, 8, 16)
POOL_STATE = 15
N_SG_GROUPS = 4
N_EXPERTS = 8
LN_EPS = 1e-5

LANES = 128
SUBLANES = 8
SAMPLE_PAD = SUBLANES
VMEM_LIMIT = 56 * 1024 * 1024

NEG_INF = float("-inf")


def _cparams(n_axes):
    return pltpu.CompilerParams(
        dimension_semantics=("arbitrary",) * n_axes, vmem_limit_bytes=VMEM_LIMIT)


def _layer_norm(x, g, b):
    mu = jnp.mean(x, axis=-1, keepdims=True)
    xc = x - mu
    var = jnp.mean(xc * xc, axis=-1, keepdims=True)
    return xc * lax.rsqrt(var + LN_EPS) * g + b


def _gelu_tanh(x):
    return 0.5 * x * (1.0 + jnp.tanh(0.7978845608028654 * (x + 0.044715 * (x * x * x))))


def _inproj_kernel(x_ref, w_ref, cos_ref, sin_ref, g_ref, b_ref,
                   q_ref, k_ref, v_ref, xp_ref, u_ref, vn_ref, *, d_attn, d_kv, d_pool, d_sg):
    xb = x_ref[...].astype(BF16)
    cos = cos_ref[...]
    sin = sin_ref[...]
    tm = xb.shape[0]
    lane = lax.broadcasted_iota(jnp.int32, (tm, LANES), 1)
    first_half = (lane % HEAD_DIM) < (HEAD_DIM // 2)

    def rope(z):
        rot = jnp.where(first_half, pltpu.roll(z, LANES - HEAD_DIM // 2, 1),
                        pltpu.roll(z, HEAD_DIM // 2, 1))
        return z * cos + rot * sin

    c0 = 0
    zq = jnp.dot(xb, w_ref[:, c0:c0 + d_attn], preferred_element_type=F32)
    for j in range(d_attn // LANES):
        sl = slice(j * LANES, (j + 1) * LANES)
        q_ref[:, sl] = (rope(zq[:, sl]) * (HEAD_DIM ** -0.5)).astype(BF16)
    c0 += d_attn
    zkv = jnp.dot(xb, w_ref[:, c0:c0 + 2 * d_kv], preferred_element_type=F32)
    k_ref[...] = rope(zkv[:, :d_kv])
    v_ref[...] = zkv[:, d_kv:]
    c0 += 2 * d_kv
    xp_ref[...] = jnp.dot(xb, w_ref[:, c0:c0 + d_pool], preferred_element_type=F32)
    c0 += d_pool
    u_ref[...] = _gelu_tanh(jnp.dot(xb, w_ref[:, c0:c0 + d_sg], preferred_element_type=F32))
    c0 += d_sg
    vg = _gelu_tanh(jnp.dot(xb, w_ref[:, c0:c0 + d_sg], preferred_element_type=F32))
    gd = d_sg // N_SG_GROUPS
    for g in range(N_SG_GROUPS):
        sl = slice(g * gd, (g + 1) * gd)
        vn_ref[:, sl] = _layer_norm(vg[:, sl], g_ref[:, sl], b_ref[:, sl])


def _inproj(x, w, cos, sin, sg_g, sg_b, *, layer, tm, d_attn, d_kv, d_pool, d_sg):
    r, d = x.shape
    d_in = w.shape[2]
    row = lambda i: (i, 0)
    const = lambda i: (0, 0)
    outs = (
        jax.ShapeDtypeStruct((r, d_attn), BF16),
        jax.ShapeDtypeStruct((r, d_kv), F32),
        jax.ShapeDtypeStruct((r, d_kv), F32),
        jax.ShapeDtypeStruct((r, d_pool), F32),
        jax.ShapeDtypeStruct((r, d_sg), F32),
        jax.ShapeDtypeStruct((r, d_sg), F32),
    )
    return pl.pallas_call(
        functools.partial(_inproj_kernel, d_attn=d_attn, d_kv=d_kv, d_pool=d_pool, d_sg=d_sg),
        out_shape=outs,
        grid=(r // tm,),
        in_specs=[
            pl.BlockSpec((tm, d), row),
            pl.BlockSpec((None, d, d_in), lambda i: (layer, 0, 0), pipeline_mode=pl.Buffered(1)),
            pl.BlockSpec((tm, LANES), row),
            pl.BlockSpec((tm, LANES), row),
            pl.BlockSpec((1, d_sg), const),
            pl.BlockSpec((1, d_sg), const),
        ],
        out_specs=(
            pl.BlockSpec((tm, d_attn), row),
            pl.BlockSpec((tm, d_kv), row),
            pl.BlockSpec((tm, d_kv), row),
            pl.BlockSpec((tm, d_pool), row),
            pl.BlockSpec((tm, d_sg), row),
            pl.BlockSpec((tm, d_sg), row),
        ),
        compiler_params=_cparams(1),
        name="inproj",
    )(x, w, cos, sin, sg_g, sg_b)


def _split_kv(kv):
    lane = lax.broadcasted_iota(jnp.int32, kv.shape, 1)
    low = lane < HEAD_DIM
    a = kv.astype(BF16)
    b = pltpu.roll(kv, HEAD_DIM, 1).astype(BF16)
    zero = jnp.zeros_like(a)
    head0 = (jnp.where(low, a, zero), jnp.where(low, zero, b))
    head1 = (jnp.where(low, b, zero), jnp.where(low, zero, a))
    return head0, head1


def _attend(q, k_pair, v_pair, allowed, sinks):
    out = None
    for kx, vx, sk in zip(k_pair, v_pair, sinks):
        s = lax.dot_general(q, kx, (((1,), (1,)), ((), ())), preferred_element_type=F32)
        s = jnp.where(allowed, s, NEG_INF)
        m = jnp.maximum(jnp.max(s, axis=1, keepdims=True), sk)
        p = jnp.exp(s - m)
        denom = jnp.sum(p, axis=1, keepdims=True) + jnp.exp(sk - m)
        o = jnp.dot(p.astype(BF16), vx, preferred_element_type=F32) * (1.0 / denom)
        out = o if out is None else out + o
    return out


def _window_sums(full):
    sums = {1: full}
    w = 1
    while w < max(POOL_WINDOWS):
        sums[2 * w] = sums[w] + pltpu.roll(sums[w], w, 0)
        w *= 2
    return sums


def _pool_mix(full, cur, cnt_of, pw_ref, ps_ref, row0):
    n = cur.shape[0]
    gd = cur.shape[1] // len(POOL_WINDOWS)
    outs = []
    for g, w in enumerate(POOL_WINDOWS):
        sl = slice(g * gd, (g + 1) * gd)
        sums = _window_sums(full[:, sl])[w]
        pooled = sums[row0:row0 + n, :] / cnt_of(w)
        d = (pooled - cur[:, sl]).astype(BF16)
        outs.append(jnp.dot(d, pw_ref[g], preferred_element_type=F32) * ps_ref[:, sl])
    return outs


def _causal_weights(sw_ref, g):
    n = sw_ref.shape[1]
    ri = lax.broadcasted_iota(jnp.int32, (n, n), 0)
    ci = lax.broadcasted_iota(jnp.int32, (n, n), 1)
    return jnp.where(ri >= ci, sw_ref[g], 0.0).astype(BF16)


def _mix_prompt_kernel(sink_ref, q_ref, kc_ref, kp_ref, vc_ref, vp_ref, xc_ref, xt_ref, u_ref,
                       vn_ref, pw_ref, ps_ref, sw_ref, sb_ref, o_ref, *, d_attn, d_pool, d_sg):
    i = pl.program_id(1)
    has_prev = i > 0

    kall = jnp.concatenate([kp_ref[...], kc_ref[...]], axis=0)
    vall = jnp.concatenate([vp_ref[...], vc_ref[...]], axis=0)
    k_heads = _split_kv(kall)
    v_heads = _split_kv(vall)
    ri = lax.broadcasted_iota(jnp.int32, (BLK, 2 * BLK), 0)
    cj = lax.broadcasted_iota(jnp.int32, (BLK, 2 * BLK), 1)
    first_key = jnp.where(has_prev, 0, BLK)
    allowed = (cj >= ri) & (cj <= ri + WINDOW) & (cj >= first_key)
    n_pairs = d_attn // LANES
    pairs_per_kv = n_pairs // N_KV_HEADS
    for p in range(n_pairs):
        g = p // pairs_per_kv
        sl = slice(p * LANES, (p + 1) * LANES)
        o = _attend(q_ref[:, sl], k_heads[g], v_heads[g], allowed,
                    (sink_ref[2 * p], sink_ref[2 * p + 1]))
        o_ref[:, sl] = o.astype(BF16)

    xc = xc_ref[...]
    tail = jnp.where(has_prev, xt_ref[...], 0.0)
    full = jnp.concatenate([tail, xc], axis=0)
    hist = tail.shape[0]
    pos = i * BLK + lax.broadcasted_iota(jnp.int32, (BLK, 1), 0)
    cnt_of = lambda w: jnp.minimum(pos + 1, w).astype(F32)
    pooled = _pool_mix(full, xc, cnt_of, pw_ref, ps_ref, hist)
    gd = d_pool // len(POOL_WINDOWS)
    for g, y in enumerate(pooled):
        o_ref[:, d_attn + g * gd:d_attn + (g + 1) * gd] = y.astype(BF16)

    gs = d_sg // N_SG_GROUPS
    for g in range(N_SG_GROUPS):
        sl = slice(g * gs, (g + 1) * gs)
        s = jnp.dot(_causal_weights(sw_ref, g), vn_ref[:, sl].astype(BF16),
                    preferred_element_type=F32) + sb_ref[:, sl]
        c0 = d_attn + d_pool + g * gs
        o_ref[:, c0:c0 + gs] = (u_ref[:, sl] * s).astype(BF16)


def _mix_prompt(sinks, q, k, v, xp, u, vn, pool_w, pool_scale, sg_w, sg_bias, *, batch, seq):
    nb = seq // BLK
    d_attn, d_kv, d_pool, d_sg = q.shape[1], k.shape[1], xp.shape[1], u.shape[1]
    d_mix = d_attn + d_pool + d_sg
    hist = 2 * SUBLANES
    cur = lambda b, i: (b * nb + i, 0)
    prev = lambda b, i: (b * nb + jnp.maximum(i - 1, 0), 0)
    tail = lambda b, i: (jnp.maximum((b * nb + i) * (BLK // hist) - 1, 0), 0)
    const2 = lambda b, i: (0, 0)
    const3 = lambda b, i: (0, 0, 0)
    return pl.pallas_call(
        functools.partial(_mix_prompt_kernel, d_attn=d_attn, d_pool=d_pool, d_sg=d_sg),
        out_shape=jax.ShapeDtypeStruct((batch * seq, d_mix), BF16),
        grid=(batch, nb),
        in_specs=[
            pl.BlockSpec(memory_space=pltpu.SMEM),
            pl.BlockSpec((BLK, d_attn), cur),
            pl.BlockSpec((BLK, d_kv), cur),
            pl.BlockSpec((BLK, d_kv), prev),
            pl.BlockSpec((BLK, d_kv), cur),
            pl.BlockSpec((BLK, d_kv), prev),
            pl.BlockSpec((BLK, d_pool), cur),
            pl.BlockSpec((hist, d_pool), tail),
            pl.BlockSpec((BLK, d_sg), cur),
            pl.BlockSpec((BLK, d_sg), cur),
            pl.BlockSpec(pool_w.shape, const3),
            pl.BlockSpec((1, d_pool), const2),
            pl.BlockSpec(sg_w.shape, const3),
            pl.BlockSpec((BLK, d_sg), const2),
        ],
        out_specs=pl.BlockSpec((BLK, d_mix), cur),
        compiler_params=_cparams(2),
        name="mix_prompt",
    )(sinks, q, k, k, v, v, xp, xp, u, vn, pool_w, pool_scale, sg_w, sg_bias)


SEQ_PER_STEP = 2


def _mix_sample_kernel(sink_ref, q_ref, kn_ref, vn_new_ref, ck_ref, cv_ref, xn_ref, st_ref,
                       u_ref, vn_ref, pw_ref, ps_ref, sw_ref, sb_ref, o_ref, *, d_attn, d_pool, d_sg):
    sp = SAMPLE_PAD
    n_pairs = d_attn // LANES
    pairs_per_kv = n_pairs // N_KV_HEADS
    m_rows = pairs_per_kv * sp
    qf = q_ref[...].astype(F32)

    ri = lax.broadcasted_iota(jnp.int32, (m_rows, 2 * BLK), 0) % sp
    cj = lax.broadcasted_iota(jnp.int32, (m_rows, 2 * BLK), 1)
    allowed = (cj >= ri) & (cj <= ri + WINDOW)
    row_pair = lax.broadcasted_iota(jnp.int32, (m_rows, 1), 0) // sp

    attn_rows = []
    for s in range(SEQ_PER_STEP):
        rows = slice(s * sp, (s + 1) * sp)
        zpad = jnp.zeros((BLK - sp, kn_ref.shape[1]), F32)
        kall = jnp.concatenate([ck_ref[s * BLK:(s + 1) * BLK, :], kn_ref[rows, :], zpad], axis=0)
        vall = jnp.concatenate([cv_ref[s * BLK:(s + 1) * BLK, :], vn_new_ref[rows, :], zpad], axis=0)
        k_heads = _split_kv(kall)
        v_heads = _split_kv(vall)
        per_pair = []
        for g in range(N_KV_HEADS):
            qst = jnp.concatenate(
                [qf[rows, (g * pairs_per_kv + pl_) * LANES:(g * pairs_per_kv + pl_ + 1) * LANES]
                 for pl_ in range(pairs_per_kv)], axis=0).astype(BF16)
            sink_lo = jnp.zeros((m_rows, 1), F32)
            sink_hi = jnp.zeros((m_rows, 1), F32)
            for pl_ in range(pairs_per_kv):
                h = 2 * (g * pairs_per_kv + pl_)
                sink_lo = jnp.where(row_pair == pl_, sink_ref[h], sink_lo)
                sink_hi = jnp.where(row_pair == pl_, sink_ref[h + 1], sink_hi)
            o = _attend(qst, k_heads[g], v_heads[g], allowed, (sink_lo, sink_hi))
            per_pair.extend(o[pl_ * sp:(pl_ + 1) * sp, :] for pl_ in range(pairs_per_kv))
        attn_rows.append(per_pair)
    for p in range(n_pairs):
        o_ref[:, p * LANES:(p + 1) * LANES] = jnp.concatenate(
            [attn_rows[s][p] for s in range(SEQ_PER_STEP)], axis=0).astype(BF16)

    gd = d_pool // len(POOL_WINDOWS)
    pooled = []
    for s in range(SEQ_PER_STEP):
        rows = slice(s * sp, (s + 1) * sp)
        xc = xn_ref[rows, :]
        full = jnp.concatenate([st_ref[s], xc], axis=0)
        cnt_of = lambda w: float(w)
        pooled.append(_pool_mix(full, xc, cnt_of, pw_ref, ps_ref, st_ref.shape[1]))
    for g in range(len(POOL_WINDOWS)):
        o_ref[:, d_attn + g * gd:d_attn + (g + 1) * gd] = jnp.concatenate(
            [pooled[s][g] for s in range(SEQ_PER_STEP)], axis=0).astype(BF16)

    gs = d_sg // N_SG_GROUPS
    for g in range(N_SG_GROUPS):
        sl = slice(g * gs, (g + 1) * gs)
        wm = _causal_weights(sw_ref, g)
        outs = []
        for s in range(SEQ_PER_STEP):
            rows = slice(s * sp, (s + 1) * sp)
            vpad = jnp.concatenate([vn_ref[rows, sl], jnp.zeros((BLK - sp, gs), F32)], axis=0)
            sg = jnp.dot(wm, vpad.astype(BF16), preferred_element_type=F32)[:sp, :] + sb_ref[:sp, sl]
            outs.append(u_ref[rows, sl] * sg)
        c0 = d_attn + d_pool + g * gs
        o_ref[:, c0:c0 + gs] = jnp.concatenate(outs, axis=0).astype(BF16)


def _mix_sample(sinks, q, k, v, xp, u, vn, cache_k, cache_v, state, pool_w, pool_scale,
                sg_w, sg_bias, *, r_prompt, n_seq):
    d_attn, d_kv, d_pool, d_sg = q.shape[1], k.shape[1], xp.shape[1], u.shape[1]
    d_mix = d_attn + d_pool + d_sg
    rows = SEQ_PER_STEP * SAMPLE_PAD
    base = r_prompt // rows
    new = lambda i: (base + i, 0)
    per_seq = lambda i: (i, 0)
    const2 = lambda i: (0, 0)
    const3 = lambda i: (0, 0, 0)
    return pl.pallas_call(
        functools.partial(_mix_sample_kernel, d_attn=d_attn, d_pool=d_pool, d_sg=d_sg),
        out_shape=jax.ShapeDtypeStruct((n_seq * SAMPLE_PAD, d_mix), BF16),
        grid=(n_seq // SEQ_PER_STEP,),
        in_specs=[
            pl.BlockSpec(memory_space=pltpu.SMEM),
            pl.BlockSpec((rows, d_attn), new),
            pl.BlockSpec((rows, d_kv), new),
            pl.BlockSpec((rows, d_kv), new),
            pl.BlockSpec((SEQ_PER_STEP * BLK, d_kv), per_seq),
            pl.BlockSpec((SEQ_PER_STEP * BLK, d_kv), per_seq),
            pl.BlockSpec((rows, d_pool), new),
            pl.BlockSpec((SEQ_PER_STEP,) + state.shape[1:], lambda i: (i, 0, 0)),
            pl.BlockSpec((rows, d_sg), new),
            pl.BlockSpec((rows, d_sg), new),
            pl.BlockSpec(pool_w.shape, const3),
            pl.BlockSpec((1, d_pool), const2),
            pl.BlockSpec(sg_w.shape, const3),
            pl.BlockSpec((BLK, d_sg), const2),
        ],
        out_specs=pl.BlockSpec((rows, d_mix), per_seq),
        compiler_params=_cparams(1),
        name="mix_sample",
    )(sinks, q, k, v, cache_k, cache_v, xp, state, u, vn, pool_w, pool_scale, sg_w, sg_bias)


ROUTE_E1, ROUTE_E2, ROUTE_G1, ROUTE_G2, ROUTE_R1, ROUTE_R2 = range(6)

SLAB_PITCH = 24


def _lane_pick(rec, k):
    lane = lax.broadcasted_iota(jnp.int32, rec.shape, 1)
    return jnp.sum(jnp.where(lane == k, rec, 0.0), axis=1, keepdims=True)


def _store_slabs(slab_ref, x):
    n, d = x.shape
    slab_ref[...] = jnp.zeros_like(slab_ref)
    for j in range(d // LANES):
        slab_ref[pl.ds(j, n, stride=SLAB_PITCH), :] = x[:, j * LANES:(j + 1) * LANES]


def _load_slab_chunk(slab_ref, j, n):
    return slab_ref[pl.ds(j, n, stride=SLAB_PITCH), :]


def _outproj_kernel(*refs, alpha, with_router, n_prompt_tiles):
    if with_router:
        (mp_ref, ms_ref, x_ref, w_ref, g_ref, b_ref, rwh_ref, rwl_ref, rb_ref,
         x1_ref, x1s_ref, route_ref, cnt_ref, carry_ref) = refs
    else:
        mp_ref, ms_ref, x_ref, w_ref, g_ref, b_ref, x1_ref, x1b_ref = refs
    mixed = jnp.where(pl.program_id(0) < n_prompt_tiles, mp_ref[...], ms_ref[...])
    y = alpha * x_ref[...] + jnp.dot(mixed, w_ref[...], preferred_element_type=F32)
    x1 = _layer_norm(y, g_ref[...], b_ref[...])
    x1_ref[...] = x1
    hi = x1.astype(BF16)
    if not with_router:
        x1b_ref[...] = hi
        return
    _store_slabs(x1s_ref, x1)

    lo = (x1 - hi.astype(F32)).astype(BF16)
    logits = (jnp.dot(hi, rwh_ref[...], preferred_element_type=F32)
              + jnp.dot(lo, rwh_ref[...], preferred_element_type=F32)
              + jnp.dot(hi, rwl_ref[...], preferred_element_type=F32) + rb_ref[...])
    lane_i = lax.broadcasted_iota(jnp.int32, logits.shape, 1)
    lane = lane_i.astype(F32)
    logits = jnp.where(lane_i < N_EXPERTS, logits, NEG_INF)
    m1 = jnp.max(logits, axis=1, keepdims=True)
    i1 = jnp.min(jnp.where(logits == m1, lane, float(LANES)), axis=1, keepdims=True)
    first = lane == i1
    rest = jnp.where(first, NEG_INF, logits)
    m2 = jnp.max(rest, axis=1, keepdims=True)
    i2 = jnp.min(jnp.where(rest == m2, lane, float(LANES)), axis=1, keepdims=True)
    second = lane == i2
    e = jnp.exp(m2 - m1)
    g1 = 1.0 / (1.0 + e)
    g2 = e * g1

    @pl.when(pl.program_id(0) == 0)
    def _():
        carry_ref[...] = jnp.zeros_like(carry_ref)

    sel = jnp.where(first, 1.0, 0.0) + jnp.where(second, 1.0, 0.0)
    tm = sel.shape[0]
    ri = lax.broadcasted_iota(jnp.int32, (tm, tm), 0)
    ci = lax.broadcasted_iota(jnp.int32, (tm, tm), 1)
    earlier = jnp.where(ri > ci, 1.0, 0.0).astype(BF16)
    ranks = jnp.dot(earlier, sel.astype(BF16), preferred_element_type=F32) + carry_ref[...]
    r1 = jnp.sum(jnp.where(first, ranks, 0.0), axis=1, keepdims=True)
    r2 = jnp.sum(jnp.where(second, ranks, 0.0), axis=1, keepdims=True)
    carry_ref[...] += jnp.sum(sel, axis=0, keepdims=True)
    cnt_ref[...] = carry_ref[...]

    rec = jnp.zeros_like(logits)
    for k, val in ((ROUTE_E1, i1), (ROUTE_E2, i2), (ROUTE_G1, g1), (ROUTE_G2, g2),
                   (ROUTE_R1, r1), (ROUTE_R2, r2)):
        rec = jnp.where(lane_i == k, val, rec)
    route_ref[...] = rec


def _outproj(mixed_p, mixed_s, x, w, g, b, router=None, *, layer, tm, alpha):
    r, d = x.shape
    assert mixed_p.shape[0] % tm == 0 and mixed_s.shape[0] % tm == 0
    n_p = mixed_p.shape[0] // tm
    row = lambda i: (i, 0)
    const = lambda i: (0, 0)
    in_specs = [
        pl.BlockSpec((tm, mixed_p.shape[1]), lambda i: (jnp.minimum(i, n_p - 1), 0)),
        pl.BlockSpec((tm, mixed_s.shape[1]), lambda i: (jnp.maximum(i - n_p, 0), 0)),
        pl.BlockSpec((tm, d), row),
        pl.BlockSpec((None,) + w.shape[1:], lambda i: (layer, 0, 0), pipeline_mode=pl.Buffered(1)),
        pl.BlockSpec((1, d), const),
        pl.BlockSpec((1, d), const),
    ]
    args = [mixed_p, mixed_s, x, w, g, b]
    scratch = []
    if router is None:
        outs = [jax.ShapeDtypeStruct((r, d), F32), jax.ShapeDtypeStruct((r, d), BF16)]
        out_specs = [pl.BlockSpec((tm, d), row), pl.BlockSpec((tm, d), row)]
    else:
        in_specs += [pl.BlockSpec((d, LANES), const), pl.BlockSpec((d, LANES), const),
                     pl.BlockSpec((1, LANES), const)]
        args += list(router)
        outs = [jax.ShapeDtypeStruct((r, d), F32),
                jax.ShapeDtypeStruct((r * SLAB_PITCH, LANES), F32),
                jax.ShapeDtypeStruct((r, LANES), F32),
                jax.ShapeDtypeStruct((1, LANES), F32)]
        out_specs = [pl.BlockSpec((tm, d), row),
                     pl.BlockSpec((tm * SLAB_PITCH, LANES), row),
                     pl.BlockSpec((tm, LANES), row),
                     pl.BlockSpec((1, LANES), const)]
        scratch = [pltpu.VMEM((1, LANES), F32)]
    return pl.pallas_call(
        functools.partial(_outproj_kernel, alpha=alpha, with_router=router is not None,
                          n_prompt_tiles=n_p),
        out_shape=tuple(outs),
        grid=(r // tm,),
        in_specs=in_specs,
        out_specs=tuple(out_specs),
        scratch_shapes=scratch,
        compiler_params=_cparams(1),
        name="outproj_router" if router is not None else "outproj",
    )(*args)


def _swiglu(xb, wg_ref, wu_ref):
    a = jnp.dot(xb, wg_ref[...], preferred_element_type=F32)
    c = jnp.dot(xb, wu_ref[...], preferred_element_type=F32)
    return (a * (1.0 / (1.0 + jnp.exp(-a))) * c).astype(BF16)


def _ffn_kernel(xb_ref, x1_ref, wg_ref, wu_ref, wd_ref, g_ref, b_ref, o_ref, *, alpha):
    f = pl.program_id(1)

    @pl.when(f == 0)
    def _():
        o_ref[...] = alpha * x1_ref[...]

    o_ref[...] += jnp.dot(_swiglu(xb_ref[...], wg_ref, wu_ref), wd_ref[...],
                          preferred_element_type=F32)

    @pl.when(f == pl.num_programs(1) - 1)
    def _():
        o_ref[...] = _layer_norm(o_ref[...], g_ref[...], b_ref[...])


def _ffn(xb, x1, wg, wu, wd, g, b, *, layer, tm, tf, alpha):
    r, d = x1.shape
    ff = wg.shape[2]
    row = lambda i, f: (i, 0)
    const = lambda i, f: (0, 0)
    return pl.pallas_call(
        functools.partial(_ffn_kernel, alpha=alpha),
        out_shape=jax.ShapeDtypeStruct((r, d), F32),
        grid=(r // tm, ff // tf),
        in_specs=[
            pl.BlockSpec((tm, d), row),
            pl.BlockSpec((tm, d), row),
            pl.BlockSpec((None, d, tf), lambda i, f: (layer, 0, f)),
            pl.BlockSpec((None, d, tf), lambda i, f: (layer, 0, f)),
            pl.BlockSpec((None, tf, d), lambda i, f: (layer, f, 0)),
            pl.BlockSpec((1, d), const),
            pl.BlockSpec((1, d), const),
        ],
        out_specs=pl.BlockSpec((tm, d), row),
        compiler_params=_cparams(2),
        name="ffn",
    )(xb, x1, wg, wu, wd, g, b)


MOE_BLOCK = 384
DMA_RING = 128


def _ring_depth(n):
    return 1 << (min(DMA_RING, n).bit_length() - 1)


def _routing_tables(route, counts, *, n_work):
    e1 = route[:, ROUTE_E1].astype(jnp.int32)
    e2 = route[:, ROUTE_E2].astype(jnp.int32)
    r1 = route[:, ROUTE_R1].astype(jnp.int32)
    r2 = route[:, ROUTE_R2].astype(jnp.int32)
    cnt = counts[0, :N_EXPERTS].astype(jnp.int32)
    nblk = (cnt + MOE_BLOCK - 1) // MOE_BLOCK
    blk_end = jnp.cumsum(nblk)
    start = (blk_end - nblk) * MOE_BLOCK
    slot1 = jnp.take(start, e1) + r1
    slot2 = jnp.take(start, e2) + r2
    n_active = blk_end[-1]
    blk = jnp.minimum(jnp.arange(n_work, dtype=jnp.int32), n_active - 1)
    work_exp = jnp.sum(blk[:, None] >= blk_end[None, :], axis=1).astype(jnp.int32)
    return slot1, slot2, work_exp, n_active.reshape(1)


def _ring_copies(n, ring, make_copies):
    def start(i):
        for c in make_copies(i):
            c.start()

    def wait(i):
        for c in make_copies(i):
            c.wait()

    def fill(i, carry):
        start(i)
        return carry

    def steady(i, carry):
        wait(i - ring)
        start(i)
        return carry

    def drain(i, carry):
        wait(i)
        return carry

    lax.fori_loop(0, ring, fill, 0)
    lax.fori_loop(ring, n, steady, 0)
    lax.fori_loop(n - ring, n, drain, 0)


SLAB_ROWS = 16


def _slab(ref, token):
    return ref.at[pl.ds(pl.multiple_of(token * SLAB_PITCH, SUBLANES), SLAB_ROWS), :]


def _dispatch_kernel(s1_ref, s2_ref, x1s_ref, xs_init, xs_hbm, sem):
    del xs_init
    tm = x1s_ref.shape[0] // SLAB_PITCH
    base = pl.program_id(0) * tm
    ring = sem.shape[1]

    def copies(r):
        k = r & (ring - 1)
        src = _slab(x1s_ref, r)
        return (pltpu.make_async_copy(src, _slab(xs_hbm, s1_ref[base + r]), sem.at[0, k]),
                pltpu.make_async_copy(src, _slab(xs_hbm, s2_ref[base + r]), sem.at[1, k]))

    _ring_copies(tm, ring, copies)


def _moe_dispatch(slot1, slot2, x1s, *, tm, n_slots):
    r = x1s.shape[0] // SLAB_PITCH
    return pl.pallas_call(
        _dispatch_kernel,
        out_shape=jax.ShapeDtypeStruct((n_slots * SLAB_PITCH, LANES), F32),
        grid_spec=pltpu.PrefetchScalarGridSpec(
            num_scalar_prefetch=2,
            grid=(r // tm,),
            in_specs=[pl.BlockSpec((tm * SLAB_PITCH, LANES), lambda i, s1, s2: (i, 0)),
                      pl.BlockSpec(memory_space=pl.ANY)],
            out_specs=pl.BlockSpec(memory_space=pl.ANY),
            scratch_shapes=[pltpu.SemaphoreType.DMA((2, _ring_depth(tm)))],
        ),
        input_output_aliases={3: 0},
        compiler_params=_cparams(1),
        name="moe_dispatch",
    )(slot1, slot2, x1s, jnp.zeros((n_slots * SLAB_PITCH, LANES), F32))


def _moe_ffn_kernel(we_ref, na_ref, xs_ref, wg_ref, wu_ref, wd_ref, ys_ref, xb_ref, acc_ref):
    del we_ref
    w = pl.program_id(0)
    f = pl.program_id(1)
    n, d = acc_ref.shape

    @pl.when(f == 0)
    def _():
        for j in range(d // LANES):
            xb_ref[:, j * LANES:(j + 1) * LANES] = _load_slab_chunk(xs_ref, j, n).astype(BF16)
        acc_ref[...] = jnp.zeros_like(acc_ref)

    @pl.when(w < na_ref[0])
    def _():
        acc_ref[...] += jnp.dot(_swiglu(xb_ref[...], wg_ref, wu_ref), wd_ref[...],
                                preferred_element_type=F32)

    @pl.when(f == pl.num_programs(1) - 1)
    def _():
        _store_slabs(ys_ref, acc_ref[...])


def _moe_ffn(work_exp, n_active, xs, wg, wu, wd, *, layer, tf):
    d = wg.shape[2]
    ff = wg.shape[3]
    nf = ff // tf
    rows = lambda w, f, we, na: (w, 0)
    chunk = lambda w, f, na: jnp.where(w < na[0], f, nf - 1)
    return pl.pallas_call(
        _moe_ffn_kernel,
        out_shape=jax.ShapeDtypeStruct(xs.shape, F32),
        grid_spec=pltpu.PrefetchScalarGridSpec(
            num_scalar_prefetch=2,
            grid=(xs.shape[0] // (MOE_BLOCK * SLAB_PITCH), nf),
            in_specs=[
                pl.BlockSpec((MOE_BLOCK * SLAB_PITCH, LANES), rows),
                pl.BlockSpec((None, None, d, tf),
                             lambda w, f, we, na: (layer, we[w], 0, chunk(w, f, na))),
                pl.BlockSpec((None, None, d, tf),
                             lambda w, f, we, na: (layer, we[w], 0, chunk(w, f, na))),
                pl.BlockSpec((None, None, tf, d),
                             lambda w, f, we, na: (layer, we[w], chunk(w, f, na), 0)),
            ],
            out_specs=pl.BlockSpec((MOE_BLOCK * SLAB_PITCH, LANES), rows),
            scratch_shapes=[pltpu.VMEM((MOE_BLOCK, d), BF16), pltpu.VMEM((MOE_BLOCK, d), F32)],
        ),
        compiler_params=_cparams(2),
        name="moe_ffn",
    )(work_exp, n_active, xs, wg, wu, wd)


def _combine_kernel(s1_ref, s2_ref, ys_hbm, x1_ref, route_ref, g_ref, b_ref, o_ref,
                    y1_ref, y2_ref, sem, *, alpha):
    tm, d = o_ref.shape
    base = pl.program_id(0) * tm
    ring = sem.shape[1]

    def copies(r):
        k = r & (ring - 1)
        return (pltpu.make_async_copy(_slab(ys_hbm, s1_ref[base + r]), _slab(y1_ref, r), sem.at[0, k]),
                pltpu.make_async_copy(_slab(ys_hbm, s2_ref[base + r]), _slab(y2_ref, r), sem.at[1, k]))

    _ring_copies(tm, ring, copies)
    route = route_ref[...]
    g1 = _lane_pick(route, ROUTE_G1)
    g2 = _lane_pick(route, ROUTE_G2)
    for j in range(d // LANES):
        sl = slice(j * LANES, (j + 1) * LANES)
        o_ref[:, sl] = (alpha * x1_ref[:, sl] + g1 * _load_slab_chunk(y1_ref, j, tm)
                        + g2 * _load_slab_chunk(y2_ref, j, tm))
    o_ref[...] = _layer_norm(o_ref[...], g_ref[...], b_ref[...])


def _moe_combine(slot1, slot2, ys, x1, route, g, b, *, tm, alpha):
    r, d = x1.shape
    row = lambda i, s1, s2: (i, 0)
    const = lambda i, s1, s2: (0, 0)
    return pl.pallas_call(
        functools.partial(_combine_kernel, alpha=alpha),
        out_shape=jax.ShapeDtypeStruct((r, d), F32),
        grid_spec=pltpu.PrefetchScalarGridSpec(
            num_scalar_prefetch=2,
            grid=(r // tm,),
            in_specs=[
                pl.BlockSpec(memory_space=pl.ANY),
                pl.BlockSpec((tm, d), row),
                pl.BlockSpec((tm, LANES), row),
                pl.BlockSpec((1, d), const),
                pl.BlockSpec((1, d), const),
            ],
            out_specs=pl.BlockSpec((tm, d), row),
            scratch_shapes=[pltpu.VMEM((tm * SLAB_PITCH, LANES), F32),
                            pltpu.VMEM((tm * SLAB_PITCH, LANES), F32),
                            pltpu.SemaphoreType.DMA((2, _ring_depth(tm)))],
        ),
        compiler_params=_cparams(1),
        name="moe_combine",
    )(slot1, slot2, ys, x1, route, g, b)


def _moe(x1, x1s, route, counts, wg, wu, wd, g, b, *, layer, tm, tf, alpha):
    r = x1.shape[0]
    n_work = (2 * r + MOE_BLOCK - 1) // MOE_BLOCK + N_EXPERTS
    slot1, slot2, work_exp, n_active = _routing_tables(route, counts, n_work=n_work)
    xs = _moe_dispatch(slot1, slot2, x1s, tm=tm, n_slots=n_work * MOE_BLOCK)
    ys = _moe_ffn(work_exp, n_active, xs, wg, wu, wd, layer=layer, tf=tf)
    return _moe_combine(slot1, slot2, ys, x1, route, g, b, tm=tm, alpha=alpha)


def _rope_tables(pos):
    half = HEAD_DIM // 2
    inv = ROPE_THETA ** (-jnp.arange(half, dtype=F32) / half)
    ang = pos.astype(F32)[:, None] * inv[None, :]
    cos = jnp.cos(ang)
    sin = jnp.sin(ang)
    cos2 = jnp.concatenate([cos, cos, cos, cos], axis=1)
    sin2 = jnp.concatenate([-sin, sin, -sin, sin], axis=1)
    return cos2, sin2


def _row_tile(r, cap):
    best = 16
    for t in range(16, cap + 1, 16):
        if r % t == 0:
            best = t
    return best


def kernel(x_prompt, x_sample, cache_k, cache_v, state_pool, w_in, w_out, attn_sinks, pool_w,
           pool_scale, sg_w, sg_b, sg_norm_g, sg_norm_b, ln1_g, ln1_b, ln2_g, ln2_b,
           ffn_w_gate, ffn_w_up, ffn_w_down, router_w, router_b, moe_w_gate, moe_w_up, moe_w_down):
    bp, tp, d = x_prompt.shape
    bs, ts, _ = x_sample.shape
    depth = w_in.shape[0]
    d_kv = cache_k.shape[3] * cache_k.shape[4]
    d_pool = pool_scale.shape[1]
    d_sg = sg_norm_g.shape[1]
    d_attn = w_in.shape[2] - 2 * d_kv - d_pool - 2 * d_sg
    win_buf = cache_k.shape[2]
    assert win_buf == WINDOW == BLK and tp % BLK == 0 and tp >= BLK and ts <= SAMPLE_PAD
    assert bs % SEQ_PER_STEP == 0 and state_pool.shape[2] == POOL_STATE
    assert cache_k.shape[3] == N_KV_HEADS and cache_k.shape[4] == HEAD_DIM
    assert d == SLAB_ROWS * LANES
    alpha = (2 * depth) ** 0.25

    r_p = bp * tp
    r_s = bs * SAMPLE_PAD
    r = r_p + r_s
    tm = _row_tile(r, 768)
    tm_out = _row_tile(math.gcd(r_p, r_s), 256)
    tm_comb = _row_tile(r, 256)

    xs_pad = jnp.pad(x_sample, ((0, 0), (0, SAMPLE_PAD - ts), (0, 0)))
    x = jnp.concatenate([x_prompt.reshape(r_p, d), xs_pad.reshape(r_s, d)], axis=0)

    pos = jnp.concatenate([
        jnp.tile(jnp.arange(tp, dtype=jnp.int32), bp),
        jnp.tile(PAST_LEN + jnp.arange(SAMPLE_PAD, dtype=jnp.int32), bs)])
    cos, sin = _rope_tables(pos)

    w_in_b = w_in.astype(BF16)
    w_out_b = w_out.astype(BF16)
    pool_w_b = pool_w.astype(BF16)
    ffn_g, ffn_u, ffn_d = (w.astype(BF16) for w in (ffn_w_gate, ffn_w_up, ffn_w_down))
    moe_g, moe_u, moe_d = (w.astype(BF16) for w in (moe_w_gate, moe_w_up, moe_w_down))
    rw_pad = jnp.pad(router_w, ((0, 0), (0, 0), (0, LANES - router_w.shape[2])))
    rw_hi = rw_pad.astype(BF16)
    rw_lo = (rw_pad - rw_hi.astype(F32)).astype(BF16)
    rb_pad = jnp.pad(router_b, ((0, 0), (0, LANES - router_b.shape[1])))[:, None, :]
    sg_bias = jnp.repeat(jnp.swapaxes(sg_b, 1, 2), d_sg // N_SG_GROUPS, axis=2)
    state_pad = jnp.pad(state_pool, ((0, 0), (0, 0), (1, 0), (0, 0)))

    kp_l, vp_l, pp_l, ks_l, vs_l, ps_l, sg_l = [], [], [], [], [], [], []
    for l in range(depth):
        q, k, v, xp, u, vn = _inproj(
            x, w_in_b, cos, sin, sg_norm_g[l][None], sg_norm_b[l][None],
            layer=l, tm=tm, d_attn=d_attn, d_kv=d_kv, d_pool=d_pool, d_sg=d_sg)
        mixed_p = _mix_prompt(attn_sinks[l], q, k, v, xp, u, vn, pool_w_b[l], pool_scale[l][None],
                              sg_w[l], sg_bias[l], batch=bp, seq=tp)
        mixed_s = _mix_sample(attn_sinks[l], q, k, v, xp, u, vn,
                              cache_k[l].reshape(bs * win_buf, d_kv), cache_v[l].reshape(bs * win_buf, d_kv),
                              state_pad[l], pool_w_b[l], pool_scale[l][None], sg_w[l], sg_bias[l],
                              r_prompt=r_p, n_seq=bs)
        i = l // 2
        if l % 2 == 0:
            x1, x1b = _outproj(mixed_p, mixed_s, x, w_out_b, ln1_g[l][None], ln1_b[l][None],
                               layer=l, tm=tm_out, alpha=alpha)
            x = _ffn(x1b, x1, ffn_g, ffn_u, ffn_d, ln2_g[l][None], ln2_b[l][None],
                     layer=i, tm=tm, tf=512, alpha=alpha)
        else:
            x1, x1s, route, counts = _outproj(
                mixed_p, mixed_s, x, w_out_b, ln1_g[l][None], ln1_b[l][None],
                router=(rw_hi[i], rw_lo[i], rb_pad[i]), layer=l, tm=tm_out, alpha=alpha)
            x = _moe(x1, x1s, route, counts, moe_g, moe_u, moe_d, ln2_g[l][None], ln2_b[l][None],
                     layer=i, tm=tm_comb, tf=512, alpha=alpha)

        kh = (N_KV_HEADS, HEAD_DIM)
        kp_l.append(k[:r_p].reshape(bp, tp, *kh)[:, -WINDOW:])
        vp_l.append(v[:r_p].reshape(bp, tp, *kh)[:, -WINDOW:])
        pp_l.append(xp[:r_p].reshape(bp, tp, d_pool)[:, -POOL_STATE:])
        k_new = k[r_p:].reshape(bs, SAMPLE_PAD, *kh)[:, :ts]
        v_new = v[r_p:].reshape(bs, SAMPLE_PAD, *kh)[:, :ts]
        ks_l.append(jnp.concatenate([cache_k[l], k_new], axis=1)[:, -win_buf:])
        vs_l.append(jnp.concatenate([cache_v[l], v_new], axis=1)[:, -win_buf:])
        xp_new = xp[r_p:].reshape(bs, SAMPLE_PAD, d_pool)[:, :ts]
        ps_l.append(jnp.concatenate([state_pool[l], xp_new], axis=1)[:, -POOL_STATE:])
        sg_l.append(vn[r_p:].reshape(bs, SAMPLE_PAD, d_sg)[:, :ts])

    y_prompt = x[:r_p].reshape(bp, tp, d)
    y_sample = x[r_p:].reshape(bs, SAMPLE_PAD, d)[:, :ts]
    return (y_prompt, y_sample, jnp.stack(kp_l), jnp.stack(vp_l), jnp.stack(pp_l),
            jnp.stack(ks_l), jnp.stack(vs_l), jnp.stack(ps_l), jnp.stack(sg_l))
```

```python
import functools
import math

import jax
import jax.numpy as jnp
from jax import lax
from jax.experimental import pallas as pl
from jax.experimental.pallas import tpu as pltpu

F32 = jnp.float32
BF16 = jnp.bfloat16

PAST_LEN = 16384
HEAD_DIM = 64
N_KV_HEADS = 2
WINDOW = 128
BLK = 128
ROPE_THETA = 10000.0
POOL_WINDOWS = (2, 4, 8, 16)
POOL_STATE = 15
N_SG_GROUPS = 4
N_EXPERTS = 8
LN_EPS = 1e-5

LANES = 128
SUBLANES = 8
SAMPLE_PAD = SUBLANES
VMEM_LIMIT = 56 * 1024 * 1024

NEG_INF = float("-inf")


def _cparams(n_axes):
    return pltpu.CompilerParams(
        dimension_semantics=("arbitrary",) * n_axes, vmem_limit_bytes=VMEM_LIMIT)


def _layer_norm(x, g, b):
    mu = jnp.mean(x, axis=-1, keepdims=True)
    xc = x - mu
    var = jnp.mean(xc * xc, axis=-1, keepdims=True)
    return xc * lax.rsqrt(var + LN_EPS) * g + b


def _gelu_tanh(x):
    return 0.5 * x * (1.0 + jnp.tanh(0.7978845608028654 * (x + 0.044715 * (x * x * x))))


def _inproj_kernel(x_ref, w_ref, cos_ref, sin_ref, g_ref, b_ref,
                   q_ref, k_ref, v_ref, xp_ref, u_ref, vn_ref, *, d_attn, d_kv, d_pool, d_sg):
    xb = x_ref[...].astype(BF16)
    cos = cos_ref[...]
    sin = sin_ref[...]
    tm = xb.shape[0]
    lane = lax.broadcasted_iota(jnp.int32, (tm, LANES), 1)
    first_half = (lane % HEAD_DIM) < (HEAD_DIM // 2)

    def rope(z):
        rot = jnp.where(first_half, pltpu.roll(z, LANES - HEAD_DIM // 2, 1),
                        pltpu.roll(z, HEAD_DIM // 2, 1))
        return z * cos + rot * sin

    c0 = 0
    zq = jnp.dot(xb, w_ref[:, c0:c0 + d_attn], preferred_element_type=F32)
    for j in range(d_attn // LANES):
        sl = slice(j * LANES, (j + 1) * LANES)
        q_ref[:, sl] = (rope(zq[:, sl]) * (HEAD_DIM ** -0.5)).astype(BF16)
    c0 += d_attn
    zkv = jnp.dot(xb, w_ref[:, c0:c0 + 2 * d_kv], preferred_element_type=F32)
    k_ref[...] = rope(zkv[:, :d_kv])
    v_ref[...] = zkv[:, d_kv:]
    c0 += 2 * d_kv
    xp_ref[...] = jnp.dot(xb, w_ref[:, c0:c0 + d_pool], preferred_element_type=F32)
    c0 += d_pool
    u_ref[...] = _gelu_tanh(jnp.dot(xb, w_ref[:, c0:c0 + d_sg], preferred_element_type=F32))
    c0 += d_sg
    vg = _gelu_tanh(jnp.dot(xb, w_ref[:, c0:c0 + d_sg], preferred_element_type=F32))
    gd = d_sg // N_SG_GROUPS
    for g in range(N_SG_GROUPS):
        sl = slice(g * gd, (g + 1) * gd)
        vn_ref[:, sl] = _layer_norm(vg[:, sl], g_ref[:, sl], b_ref[:, sl])


def _inproj(x, w, cos, sin, sg_g, sg_b, *, layer, tm, d_attn, d_kv, d_pool, d_sg):
    r, d = x.shape
    d_in = w.shape[2]
    row = lambda i: (i, 0)
    const = lambda i: (0, 0)
    outs = (
        jax.ShapeDtypeStruct((r, d_attn), BF16),
        jax.ShapeDtypeStruct((r, d_kv), F32),
        jax.ShapeDtypeStruct((r, d_kv), F32),
        jax.ShapeDtypeStruct((r, d_pool), F32),
        jax.ShapeDtypeStruct((r, d_sg), F32),
        jax.ShapeDtypeStruct((r, d_sg), F32),
    )
    return pl.pallas_call(
        functools.partial(_inproj_kernel, d_attn=d_attn, d_kv=d_kv, d_pool=d_pool, d_sg=d_sg),
        out_shape=outs,
        grid=(r // tm,),
        in_specs=[
            pl.BlockSpec((tm, d), row),
            pl.BlockSpec((None, d, d_in), lambda i: (layer, 0, 0), pipeline_mode=pl.Buffered(1)),
            pl.BlockSpec((tm, LANES), row),
            pl.BlockSpec((tm, LANES), row),
            pl.BlockSpec((1, d_sg), const),
            pl.BlockSpec((1, d_sg), const),
        ],
        out_specs=(
            pl.BlockSpec((tm, d_attn), row),
            pl.BlockSpec((tm, d_kv), row),
            pl.BlockSpec((tm, d_kv), row),
            pl.BlockSpec((tm, d_pool), row),
            pl.BlockSpec((tm, d_sg), row),
            pl.BlockSpec((tm, d_sg), row),
        ),
        compiler_params=_cparams(1),
        name="inproj",
    )(x, w, cos, sin, sg_g, sg_b)


def _split_kv(kv):
    lane = lax.broadcasted_iota(jnp.int32, kv.shape, 1)
    low = lane < HEAD_DIM
    a = kv.astype(BF16)
    b = pltpu.roll(kv, HEAD_DIM, 1).astype(BF16)
    zero = jnp.zeros_like(a)
    head0 = (jnp.where(low, a, zero), jnp.where(low, zero, b))
    head1 = (jnp.where(low, b, zero), jnp.where(low, zero, a))
    return head0, head1


def _attend(q, k_pair, v_pair, allowed, sinks):
    out = None
    for kx, vx, sk in zip(k_pair, v_pair, sinks):
        s = lax.dot_general(q, kx, (((1,), (1,)), ((), ())), preferred_element_type=F32)
        s = jnp.where(allowed, s, NEG_INF)
        m = jnp.maximum(jnp.max(s, axis=1, keepdims=True), sk)
        p = jnp.exp(s - m)
        denom = jnp.sum(p, axis=1, keepdims=True) + jnp.exp(sk - m)
        o = jnp.dot(p.astype(BF16), vx, preferred_element_type=F32) * (1.0 / denom)
        out = o if out is None else out + o
    return out


def _window_sums(full):
    sums = {1: full}
    w = 1
    while w < max(POOL_WINDOWS):
        sums[2 * w] = sums[w] + pltpu.roll(sums[w], w, 0)
        w *= 2
    return sums


def _pool_mix(full, cur, cnt_of, pw_ref, ps_ref, row0):
    n = cur.shape[0]
    gd = cur.shape[1] // len(POOL_WINDOWS)
    outs = []
    for g, w in enumerate(POOL_WINDOWS):
        sl = slice(g * gd, (g + 1) * gd)
        sums = _window_sums(full[:, sl])[w]
        pooled = sums[row0:row0 + n, :] / cnt_of(w)
        d = (pooled - cur[:, sl]).astype(BF16)
        outs.append(jnp.dot(d, pw_ref[g], preferred_element_type=F32) * ps_ref[:, sl])
    return outs


def _causal_weights(sw_ref, g):
    n = sw_ref.shape[1]
    ri = lax.broadcasted_iota(jnp.int32, (n, n), 0)
    ci = lax.broadcasted_iota(jnp.int32, (n, n), 1)
    return jnp.where(ri >= ci, sw_ref[g], 0.0).astype(BF16)


def _mix_prompt_kernel(sink_ref, q_ref, kc_ref, kp_ref, vc_ref, vp_ref, xc_ref, xt_ref, u_ref,
                       vn_ref, pw_ref, ps_ref, sw_ref, sb_ref, o_ref, *, d_attn, d_pool, d_sg):
    i = pl.program_id(1)
    has_prev = i > 0

    kall = jnp.concatenate([kp_ref[...], kc_ref[...]], axis=0)
    vall = jnp.concatenate([vp_ref[...], vc_ref[...]], axis=0)
    k_heads = _split_kv(kall)
    v_heads = _split_kv(vall)
    ri = lax.broadcasted_iota(jnp.int32, (BLK, 2 * BLK), 0)
    cj = lax.broadcasted_iota(jnp.int32, (BLK, 2 * BLK), 1)
    first_key = jnp.where(has_prev, 0, BLK)
    allowed = (cj >= ri) & (cj <= ri + WINDOW) & (cj >= first_key)
    n_pairs = d_attn // LANES
    pairs_per_kv = n_pairs // N_KV_HEADS
    for p in range(n_pairs):
        g = p // pairs_per_kv
        sl = slice(p * LANES, (p + 1) * LANES)
        o = _attend(q_ref[:, sl], k_heads[g], v_heads[g], allowed,
                    (sink_ref[2 * p], sink_ref[2 * p + 1]))
        o_ref[:, sl] = o.astype(BF16)

    xc = xc_ref[...]
    tail = jnp.where(has_prev, xt_ref[...], 0.0)
    full = jnp.concatenate([tail, xc], axis=0)
    hist = tail.shape[0]
    pos = i * BLK + lax.broadcasted_iota(jnp.int32, (BLK, 1), 0)
    cnt_of = lambda w: jnp.minimum(pos + 1, w).astype(F32)
    pooled = _pool_mix(full, xc, cnt_of, pw_ref, ps_ref, hist)
    gd = d_pool // len(POOL_WINDOWS)
    for g, y in enumerate(pooled):
        o_ref[:, d_attn + g * gd:d_attn + (g + 1) * gd] = y.astype(BF16)

    gs = d_sg // N_SG_GROUPS
    for g in range(N_SG_GROUPS):
        sl = slice(g * gs, (g + 1) * gs)
        s = jnp.dot(_causal_weights(sw_ref, g), vn_ref[:, sl].astype(BF16),
                    preferred_element_type=F32) + sb_ref[:, sl]
        c0 = d_attn + d_pool + g * gs
        o_ref[:, c0:c0 + gs] = (u_ref[:, sl] * s).astype(BF16)


def _mix_prompt(sinks, q, k, v, xp, u, vn, pool_w, pool_scale, sg_w, sg_bias, *, batch, seq):
    nb = seq // BLK
    d_attn, d_kv, d_pool, d_sg = q.shape[1], k.shape[1], xp.shape[1], u.shape[1]
    d_mix = d_attn + d_pool + d_sg
    hist = 2 * SUBLANES
    cur = lambda b, i: (b * nb + i, 0)
    prev = lambda b, i: (b * nb + jnp.maximum(i - 1, 0), 0)
    tail = lambda b, i: (jnp.maximum((b * nb + i) * (BLK // hist) - 1, 0), 0)
    const2 = lambda b, i: (0, 0)
    const3 = lambda b, i: (0, 0, 0)
    return pl.pallas_call(
        functools.partial(_mix_prompt_kernel, d_attn=d_attn, d_pool=d_pool, d_sg=d_sg),
        out_shape=jax.ShapeDtypeStruct((batch * seq, d_mix), BF16),
        grid=(batch, nb),
        in_specs=[
            pl.BlockSpec(memory_space=pltpu.SMEM),
            pl.BlockSpec((BLK, d_attn), cur),
            pl.BlockSpec((BLK, d_kv), cur),
            pl.BlockSpec((BLK, d_kv), prev),
            pl.BlockSpec((BLK, d_kv), cur),
            pl.BlockSpec((BLK, d_kv), prev),
            pl.BlockSpec((BLK, d_pool), cur),
            pl.BlockSpec((hist, d_pool), tail),
            pl.BlockSpec((BLK, d_sg), cur),
            pl.BlockSpec((BLK, d_sg), cur),
            pl.BlockSpec(pool_w.shape, const3),
            pl.BlockSpec((1, d_pool), const2),
            pl.BlockSpec(sg_w.shape, const3),
            pl.BlockSpec((BLK, d_sg), const2),
        ],
        out_specs=pl.BlockSpec((BLK, d_mix), cur),
        compiler_params=_cparams(2),
        name="mix_prompt",
    )(sinks, q, k, k, v, v, xp, xp, u, vn, pool_w, pool_scale, sg_w, sg_bias)


SEQ_PER_STEP = 2


def _mix_sample_kernel(sink_ref, q_ref, kn_ref, vn_new_ref, ck_ref, cv_ref, xn_ref, st_ref,
                       u_ref, vn_ref, pw_ref, ps_ref, sw_ref, sb_ref, o_ref, *, d_attn, d_pool, d_sg):
    sp = SAMPLE_PAD
    n_pairs = d_attn // LANES
    pairs_per_kv = n_pairs // N_KV_HEADS
    m_rows = pairs_per_kv * sp
    qf = q_ref[...].astype(F32)

    ri = lax.broadcasted_iota(jnp.int32, (m_rows, 2 * BLK), 0) % sp
    cj = lax.broadcasted_iota(jnp.int32, (m_rows, 2 * BLK), 1)
    allowed = (cj >= ri) & (cj <= ri + WINDOW)
    row_pair = lax.broadcasted_iota(jnp.int32, (m_rows, 1), 0) // sp

    attn_rows = []
    for s in range(SEQ_PER_STEP):
        rows = slice(s * sp, (s + 1) * sp)
        zpad = jnp.zeros((BLK - sp, kn_ref.shape[1]), F32)
        kall = jnp.concatenate([ck_ref[s * BLK:(s + 1) * BLK, :], kn_ref[rows, :], zpad], axis=0)
        vall = jnp.concatenate([cv_ref[s * BLK:(s + 1) * BLK, :], vn_new_ref[rows, :], zpad], axis=0)
        k_heads = _split_kv(kall)
        v_heads = _split_kv(vall)
        per_pair = []
        for g in range(N_KV_HEADS):
            qst = jnp.concatenate(
                [qf[rows, (g * pairs_per_kv + pl_) * LANES:(g * pairs_per_kv + pl_ + 1) * LANES]
                 for pl_ in range(pairs_per_kv)], axis=0).astype(BF16)
            sink_lo = jnp.zeros((m_rows, 1), F32)
            sink_hi = jnp.zeros((m_rows, 1), F32)
            for pl_ in range(pairs_per_kv):
                h = 2 * (g * pairs_per_kv + pl_)
                sink_lo = jnp.where(row_pair == pl_, sink_ref[h], sink_lo)
                sink_hi = jnp.where(row_pair == pl_, sink_ref[h + 1], sink_hi)
            o = _attend(qst, k_heads[g], v_heads[g], allowed, (sink_lo, sink_hi))
            per_pair.extend(o[pl_ * sp:(pl_ + 1) * sp, :] for pl_ in range(pairs_per_kv))
        attn_rows.append(per_pair)
    for p in range(n_pairs):
        o_ref[:, p * LANES:(p + 1) * LANES] = jnp.concatenate(
            [attn_rows[s][p] for s in range(SEQ_PER_STEP)], axis=0).astype(BF16)

    gd = d_pool // len(POOL_WINDOWS)
    pooled = []
    for s in range(SEQ_PER_STEP):
        rows = slice(s * sp, (s + 1) * sp)
        xc = xn_ref[rows, :]
        full = jnp.concatenate([st_ref[s], xc], axis=0)
        cnt_of = lambda w: float(w)
        pooled.append(_pool_mix(full, xc, cnt_of, pw_ref, ps_ref, st_ref.shape[1]))
    for g in range(len(POOL_WINDOWS)):
        o_ref[:, d_attn + g * gd:d_attn + (g + 1) * gd] = jnp.concatenate(
            [pooled[s][g] for s in range(SEQ_PER_STEP)], axis=0).astype(BF16)

    gs = d_sg // N_SG_GROUPS
    for g in range(N_SG_GROUPS):
        sl = slice(g * gs, (g + 1) * gs)
        wm = _causal_weights(sw_ref, g)
        outs = []
        for s in range(SEQ_PER_STEP):
            rows = slice(s * sp, (s + 1) * sp)
            vpad = jnp.concatenate([vn_ref[rows, sl], jnp.zeros((BLK - sp, gs), F32)], axis=0)
            sg = jnp.dot(wm, vpad.astype(BF16), preferred_element_type=F32)[:sp, :] + sb_ref[:sp, sl]
            outs.append(u_ref[rows, sl] * sg)
        c0 = d_attn + d_pool + g * gs
        o_ref[:, c0:c0 + gs] = jnp.concatenate(outs, axis=0).astype(BF16)


def _mix_sample(sinks, q, k, v, xp, u, vn, cache_k, cache_v, state, pool_w, pool_scale,
                sg_w, sg_bias, *, r_prompt, n_seq):
    d_attn, d_kv, d_pool, d_sg = q.shape[1], k.shape[1], xp.shape[1], u.shape[1]
    d_mix = d_attn + d_pool + d_sg
    rows = SEQ_PER_STEP * SAMPLE_PAD
    base = r_prompt // rows
    new = lambda i: (base + i, 0)
    per_seq = lambda i: (i, 0)
    const2 = lambda i: (0, 0)
    const3 = lambda i: (0, 0, 0)
    return pl.pallas_call(
        functools.partial(_mix_sample_kernel, d_attn=d_attn, d_pool=d_pool, d_sg=d_sg),
        out_shape=jax.ShapeDtypeStruct((n_seq * SAMPLE_PAD, d_mix), BF16),
        grid=(n_seq // SEQ_PER_STEP,),
        in_specs=[
            pl.BlockSpec(memory_space=pltpu.SMEM),
            pl.BlockSpec((rows, d_attn), new),
            pl.BlockSpec((rows, d_kv), new),
            pl.BlockSpec((rows, d_kv), new),
            pl.BlockSpec((SEQ_PER_STEP * BLK, d_kv), per_seq),
            pl.BlockSpec((SEQ_PER_STEP * BLK, d_kv), per_seq),
            pl.BlockSpec((rows, d_pool), new),
            pl.BlockSpec((SEQ_PER_STEP,) + state.shape[1:], lambda i: (i, 0, 0)),
            pl.BlockSpec((rows, d_sg), new),
            pl.BlockSpec((rows, d_sg), new),
            pl.BlockSpec(pool_w.shape, const3),
            pl.BlockSpec((1, d_pool), const2),
            pl.BlockSpec(sg_w.shape, const3),
            pl.BlockSpec((BLK, d_sg), const2),
        ],
        out_specs=pl.BlockSpec((rows, d_mix), per_seq),
        compiler_params=_cparams(1),
        name="mix_sample",
    )(sinks, q, k, v, cache_k, cache_v, xp, state, u, vn, pool_w, pool_scale, sg_w, sg_bias)


ROUTE_E1, ROUTE_E2, ROUTE_G1, ROUTE_G2, ROUTE_R1, ROUTE_R2 = range(6)

SLAB_PITCH = 24


def _lane_pick(rec, k):
    lane = lax.broadcasted_iota(jnp.int32, rec.shape, 1)
    return jnp.sum(jnp.where(lane == k, rec, 0.0), axis=1, keepdims=True)


def _store_slabs(slab_ref, x):
    n, d = x.shape
    slab_ref[...] = jnp.zeros_like(slab_ref)
    for j in range(d // LANES):
        slab_ref[pl.ds(j, n, stride=SLAB_PITCH), :] = x[:, j * LANES:(j + 1) * LANES]


def _load_slab_chunk(slab_ref, j, n):
    return slab_ref[pl.ds(j, n, stride=SLAB_PITCH), :]


def _outproj_kernel(*refs, alpha, with_router, n_prompt_tiles):
    if with_router:
        (mp_ref, ms_ref, x_ref, w_ref, g_ref, b_ref, rwh_ref, rwl_ref, rb_ref,
         x1_ref, x1s_ref, route_ref, cnt_ref, carry_ref) = refs
    else:
        mp_ref, ms_ref, x_ref, w_ref, g_ref, b_ref, x1_ref, x1b_ref = refs
    from_prompt = pl.program_id(0) < n_prompt_tiles
    tm = x_ref.shape[0]
    n_sub = 2 if tm % (4 * SUBLANES) == 0 else 1
    parts = []
    for h in range(n_sub):
        rows = slice(h * (tm // n_sub), (h + 1) * (tm // n_sub))
        mixed = jnp.where(from_prompt, mp_ref[rows, :], ms_ref[rows, :])
        y = alpha * x_ref[rows, :] + jnp.dot(mixed, w_ref[...], preferred_element_type=F32)
        part = _layer_norm(y, g_ref[...], b_ref[...])
        x1_ref[rows, :] = part
        if not with_router:
            x1b_ref[rows, :] = part.astype(BF16)
        parts.append(part)
    if not with_router:
        return
    x1 = jnp.concatenate(parts, axis=0)
    hi = x1.astype(BF16)
    _store_slabs(x1s_ref, x1)

    lo = (x1 - hi.astype(F32)).astype(BF16)
    logits = (jnp.dot(hi, rwh_ref[...], preferred_element_type=F32)
              + jnp.dot(lo, rwh_ref[...], preferred_element_type=F32)
              + jnp.dot(hi, rwl_ref[...], preferred_element_type=F32) + rb_ref[...])
    lane_i = lax.broadcasted_iota(jnp.int32, logits.shape, 1)
    lane = lane_i.astype(F32)
    logits = jnp.where(lane_i < N_EXPERTS, logits, NEG_INF)
    m1 = jnp.max(logits, axis=1, keepdims=True)
    i1 = jnp.min(jnp.where(logits == m1, lane, float(LANES)), axis=1, keepdims=True)
    first = lane == i1
    rest = jnp.where(first, NEG_INF, logits)
    m2 = jnp.max(rest, axis=1, keepdims=True)
    i2 = jnp.min(jnp.where(rest == m2, lane, float(LANES)), axis=1, keepdims=True)
    second = lane == i2
    e = jnp.exp(m2 - m1)
    g1 = 1.0 / (1.0 + e)
    g2 = e * g1

    @pl.when(pl.program_id(0) == 0)
    def _():
        carry_ref[...] = jnp.zeros_like(carry_ref)

    sel = jnp.where(first, 1.0, 0.0) + jnp.where(second, 1.0, 0.0)
    tm = sel.shape[0]
    ri = lax.broadcasted_iota(jnp.int32, (tm, tm), 0)
    ci = lax.broadcasted_iota(jnp.int32, (tm, tm), 1)
    earlier = jnp.where(ri > ci, 1.0, 0.0).astype(BF16)
    ranks = jnp.dot(earlier, sel.astype(BF16), preferred_element_type=F32) + carry_ref[...]
    r1 = jnp.sum(jnp.where(first, ranks, 0.0), axis=1, keepdims=True)
    r2 = jnp.sum(jnp.where(second, ranks, 0.0), axis=1, keepdims=True)
    carry_ref[...] += jnp.sum(sel, axis=0, keepdims=True)
    cnt_ref[...] = carry_ref[...]

    rec = jnp.zeros_like(logits)
    for k, val in ((ROUTE_E1, i1), (ROUTE_E2, i2), (ROUTE_G1, g1), (ROUTE_G2, g2),
                   (ROUTE_R1, r1), (ROUTE_R2, r2)):
        rec = jnp.where(lane_i == k, val, rec)
    route_ref[...] = rec


def _outproj(mixed_p, mixed_s, x, w, g, b, router=None, *, layer, tm, alpha):
    r, d = x.shape
    assert mixed_p.shape[0] % tm == 0 and mixed_s.shape[0] % tm == 0
    n_p = mixed_p.shape[0] // tm
    row = lambda i: (i, 0)
    const = lambda i: (0, 0)
    in_specs = [
        pl.BlockSpec((tm, mixed_p.shape[1]), lambda i: (jnp.minimum(i, n_p - 1), 0)),
        pl.BlockSpec((tm, mixed_s.shape[1]), lambda i: (jnp.maximum(i - n_p, 0), 0)),
        pl.BlockSpec((tm, d), row),
        pl.BlockSpec((None,) + w.shape[1:], lambda i: (layer, 0, 0), pipeline_mode=pl.Buffered(1)),
        pl.BlockSpec((1, d), const),
        pl.BlockSpec((1, d), const),
    ]
    args = [mixed_p, mixed_s, x, w, g, b]
    scratch = []
    if router is None:
        outs = [jax.ShapeDtypeStruct((r, d), F32), jax.ShapeDtypeStruct((r, d), BF16)]
        out_specs = [pl.BlockSpec((tm, d), row), pl.BlockSpec((tm, d), row)]
    else:
        in_specs += [pl.BlockSpec((d, LANES), const), pl.BlockSpec((d, LANES), const),
                     pl.BlockSpec((1, LANES), const)]
        args += list(router)
        outs = [jax.ShapeDtypeStruct((r, d), F32),
                jax.ShapeDtypeStruct((r * SLAB_PITCH, LANES), F32),
                jax.ShapeDtypeStruct((r, LANES), F32),
                jax.ShapeDtypeStruct((1, LANES), F32)]
        out_specs = [pl.BlockSpec((tm, d), row),
                     pl.BlockSpec((tm * SLAB_PITCH, LANES), row),
                     pl.BlockSpec((tm, LANES), row),
                     pl.BlockSpec((1, LANES), const)]
        scratch = [pltpu.VMEM((1, LANES), F32)]
    return pl.pallas_call(
        functools.partial(_outproj_kernel, alpha=alpha, with_router=router is not None,
                          n_prompt_tiles=n_p),
        out_shape=tuple(outs),
        grid=(r // tm,),
        in_specs=in_specs,
        out_specs=tuple(out_specs),
        scratch_shapes=scratch,
        compiler_params=_cparams(1),
        name="outproj_router" if router is not None else "outproj",
    )(*args)


def _swiglu(xb, wg_ref, wu_ref):
    a = jnp.dot(xb, wg_ref[...], preferred_element_type=F32)
    c = jnp.dot(xb, wu_ref[...], preferred_element_type=F32)
    return (a * (1.0 / (1.0 + jnp.exp(-a))) * c).astype(BF16)


def _ffn_kernel(*refs, alpha, n_cast):
    xb_ref, x1_ref, wg_ref, wu_ref, wd_ref, g_ref, b_ref = refs[:7]
    cast_src = refs[7:7 + n_cast]
    o_ref = refs[7 + n_cast]
    cast_dst = refs[8 + n_cast:]
    f = pl.program_id(1)

    @pl.when(f == 0)
    def _():
        o_ref[...] = alpha * x1_ref[...]

    o_ref[...] += jnp.dot(_swiglu(xb_ref[...], wg_ref, wu_ref), wd_ref[...],
                          preferred_element_type=F32)

    @pl.when(f == pl.num_programs(1) - 1)
    def _():
        o_ref[...] = _layer_norm(o_ref[...], g_ref[...], b_ref[...])

    for src, dst in zip(cast_src, cast_dst):
        dst[...] = src[...].astype(BF16)


def _cast_dst_map(i, f, *, nf, last):
    return (jnp.minimum(i * nf + f, last), 0)


def _cast_src_map(i, f, *, layer, nf, last):
    return (layer,) + _cast_dst_map(i, f, nf=nf, last=last)


def _cast_chunk_rows(rows, steps):
    for t in range(2 * SUBLANES, rows + 1, 2 * SUBLANES):
        if rows % t == 0 and rows // t <= steps:
            return t
    return rows


def _ffn(xb, x1, wg, wu, wd, g, b, cast_jobs=(), *, layer, tm, tf, alpha):
    r, d = x1.shape
    ff = wg.shape[2]
    nf = ff // tf
    steps = (r // tm) * nf
    row = lambda i, f: (i, 0)
    const = lambda i, f: (0, 0)
    in_specs = [
        pl.BlockSpec((tm, d), row),
        pl.BlockSpec((tm, d), row, pipeline_mode=pl.Buffered(1)),
        pl.BlockSpec((None, d, tf), lambda i, f: (layer, 0, f)),
        pl.BlockSpec((None, d, tf), lambda i, f: (layer, 0, f)),
        pl.BlockSpec((None, tf, d), lambda i, f: (layer, f, 0)),
        pl.BlockSpec((1, d), const),
        pl.BlockSpec((1, d), const),
    ]
    out_shape = [jax.ShapeDtypeStruct((r, d), F32)]
    out_specs = [pl.BlockSpec((tm, d), row)]
    args = [xb, x1, wg, wu, wd, g, b]
    for arr, li in cast_jobs:
        _, rows, cols = arr.shape
        cr = _cast_chunk_rows(rows, steps)
        src_map = functools.partial(_cast_src_map, layer=li, nf=nf, last=rows // cr - 1)
        dst_map = functools.partial(_cast_dst_map, nf=nf, last=rows // cr - 1)
        in_specs.append(pl.BlockSpec((None, cr, cols), src_map))
        out_shape.append(jax.ShapeDtypeStruct((rows, cols), BF16))
        out_specs.append(pl.BlockSpec((cr, cols), dst_map))
        args.append(arr)
    outs = pl.pallas_call(
        functools.partial(_ffn_kernel, alpha=alpha, n_cast=len(cast_jobs)),
        out_shape=tuple(out_shape),
        grid=(r // tm, nf),
        in_specs=in_specs,
        out_specs=tuple(out_specs),
        compiler_params=_cparams(2),
        name="ffn",
    )(*args)
    return outs[0], outs[1:]


MOE_BLOCK = 512
DMA_RING = 128


def _ring_depth(n):
    return 1 << (min(DMA_RING, n).bit_length() - 1)


def _routing_tables(route, counts, *, n_work):
    e1 = route[:, ROUTE_E1].astype(jnp.int32)
    e2 = route[:, ROUTE_E2].astype(jnp.int32)
    r1 = route[:, ROUTE_R1].astype(jnp.int32)
    r2 = route[:, ROUTE_R2].astype(jnp.int32)
    cnt = counts[0, :N_EXPERTS].astype(jnp.int32)
    nblk = (cnt + MOE_BLOCK - 1) // MOE_BLOCK
    blk_end = jnp.cumsum(nblk)
    start = (blk_end - nblk) * MOE_BLOCK
    slot1 = jnp.take(start, e1) + r1
    slot2 = jnp.take(start, e2) + r2
    n_active = blk_end[-1]
    blk = jnp.minimum(jnp.arange(n_work, dtype=jnp.int32), n_active - 1)
    work_exp = jnp.sum(blk[:, None] >= blk_end[None, :], axis=1).astype(jnp.int32)
    return slot1, slot2, work_exp, n_active.reshape(1)


def _ring_copies(n, ring, make_copies):
    def start(i):
        for c in make_copies(i):
            c.start()

    def wait(i):
        for c in make_copies(i):
            c.wait()

    def fill(i, carry):
        start(i)
        return carry

    def steady(i, carry):
        wait(i - ring)
        start(i)
        return carry

    def drain(i, carry):
        wait(i)
        return carry

    lax.fori_loop(0, ring, fill, 0)
    lax.fori_loop(ring, n, steady, 0)
    lax.fori_loop(n - ring, n, drain, 0)


SLAB_ROWS = 16


def _slab(ref, token):
    return ref.at[pl.ds(pl.multiple_of(token * SLAB_PITCH, SUBLANES), SLAB_ROWS), :]


def _dispatch_kernel(s1_ref, s2_ref, x1s_ref, xs_init, xs_hbm, sem):
    del xs_init
    tm = x1s_ref.shape[0] // SLAB_PITCH
    base = pl.program_id(0) * tm
    ring = sem.shape[1]

    def copies(r):
        k = r & (ring - 1)
        src = _slab(x1s_ref, r)
        return (pltpu.make_async_copy(src, _slab(xs_hbm, s1_ref[base + r]), sem.at[0, k]),
                pltpu.make_async_copy(src, _slab(xs_hbm, s2_ref[base + r]), sem.at[1, k]))

    _ring_copies(tm, ring, copies)


def _moe_dispatch(slot1, slot2, x1s, *, tm, n_slots):
    r = x1s.shape[0] // SLAB_PITCH
    return pl.pallas_call(
        _dispatch_kernel,
        out_shape=jax.ShapeDtypeStruct((n_slots * SLAB_PITCH, LANES), F32),
        grid_spec=pltpu.PrefetchScalarGridSpec(
            num_scalar_prefetch=2,
            grid=(r // tm,),
            in_specs=[pl.BlockSpec((tm * SLAB_PITCH, LANES), lambda i, s1, s2: (i, 0)),
                      pl.BlockSpec(memory_space=pl.ANY)],
            out_specs=pl.BlockSpec(memory_space=pl.ANY),
            scratch_shapes=[pltpu.SemaphoreType.DMA((2, _ring_depth(tm)))],
        ),
        input_output_aliases={3: 0},
        compiler_params=_cparams(1),
        name="moe_dispatch",
    )(slot1, slot2, x1s, jnp.zeros((n_slots * SLAB_PITCH, LANES), F32))


def _moe_ffn_kernel(we_ref, na_ref, xs_ref, wg_ref, wu_ref, wd_ref, ys_ref, xb_ref, acc_ref):
    del we_ref
    w = pl.program_id(0)
    f = pl.program_id(1)
    n, d = acc_ref.shape

    @pl.when(f == 0)
    def _():
        for j in range(d // LANES):
            xb_ref[:, j * LANES:(j + 1) * LANES] = _load_slab_chunk(xs_ref, j, n).astype(BF16)
        acc_ref[...] = jnp.zeros_like(acc_ref)

    @pl.when(w < na_ref[0])
    def _():
        acc_ref[...] += jnp.dot(_swiglu(xb_ref[...], wg_ref, wu_ref), wd_ref[...],
                                preferred_element_type=F32)

    @pl.when(f == pl.num_programs(1) - 1)
    def _():
        _store_slabs(ys_ref, acc_ref[...])


def _moe_ffn(work_exp, n_active, xs, wg, wu, wd, *, layer, tf):
    d = wg.shape[2]
    ff = wg.shape[3]
    nf = ff // tf
    rows = lambda w, f, we, na: (w, 0)
    chunk = lambda w, f, na: jnp.where(w < na[0], f, nf - 1)
    return pl.pallas_call(
        _moe_ffn_kernel,
        out_shape=jax.ShapeDtypeStruct(xs.shape, F32),
        grid_spec=pltpu.PrefetchScalarGridSpec(
            num_scalar_prefetch=2,
            grid=(xs.shape[0] // (MOE_BLOCK * SLAB_PITCH), nf),
            in_specs=[
                pl.BlockSpec((MOE_BLOCK * SLAB_PITCH, LANES), rows),
                pl.BlockSpec((None, None, d, tf),
                             lambda w, f, we, na: (layer, we[w], 0, chunk(w, f, na))),
                pl.BlockSpec((None, None, d, tf),
                             lambda w, f, we, na: (layer, we[w], 0, chunk(w, f, na))),
                pl.BlockSpec((None, None, tf, d),
                             lambda w, f, we, na: (layer, we[w], chunk(w, f, na), 0)),
            ],
            out_specs=pl.BlockSpec((MOE_BLOCK * SLAB_PITCH, LANES), rows),
            scratch_shapes=[pltpu.VMEM((MOE_BLOCK, d), BF16), pltpu.VMEM((MOE_BLOCK, d), F32)],
        ),
        compiler_params=_cparams(2),
        name="moe_ffn",
    )(work_exp, n_active, xs, wg, wu, wd)


def _combine_kernel(s1_ref, s2_ref, ys_hbm, x1_ref, route_ref, g_ref, b_ref, o_ref,
                    y1_ref, y2_ref, sem, *, alpha):
    tm, d = o_ref.shape
    base = pl.program_id(0) * tm
    ring = sem.shape[1]

    def copies(r):
        k = r & (ring - 1)
        return (pltpu.make_async_copy(_slab(ys_hbm, s1_ref[base + r]), _slab(y1_ref, r), sem.at[0, k]),
                pltpu.make_async_copy(_slab(ys_hbm, s2_ref[base + r]), _slab(y2_ref, r), sem.at[1, k]))

    _ring_copies(tm, ring, copies)
    route = route_ref[...]
    g1 = _lane_pick(route, ROUTE_G1)
    g2 = _lane_pick(route, ROUTE_G2)
    for j in range(d // LANES):
        sl = slice(j * LANES, (j + 1) * LANES)
        o_ref[:, sl] = (alpha * x1_ref[:, sl] + g1 * _load_slab_chunk(y1_ref, j, tm)
                        + g2 * _load_slab_chunk(y2_ref, j, tm))
    o_ref[...] = _layer_norm(o_ref[...], g_ref[...], b_ref[...])


def _moe_combine(slot1, slot2, ys, x1, route, g, b, *, tm, alpha):
    r, d = x1.shape
    row = lambda i, s1, s2: (i, 0)
    const = lambda i, s1, s2: (0, 0)
    return pl.pallas_call(
        functools.partial(_combine_kernel, alpha=alpha),
        out_shape=jax.ShapeDtypeStruct((r, d), F32),
        grid_spec=pltpu.PrefetchScalarGridSpec(
            num_scalar_prefetch=2,
            grid=(r // tm,),
            in_specs=[
                pl.BlockSpec(memory_space=pl.ANY),
                pl.BlockSpec((tm, d), row),
                pl.BlockSpec((tm, LANES), row),
                pl.BlockSpec((1, d), const),
                pl.BlockSpec((1, d), const),
            ],
            out_specs=pl.BlockSpec((tm, d), row),
            scratch_shapes=[pltpu.VMEM((tm * SLAB_PITCH, LANES), F32),
                            pltpu.VMEM((tm * SLAB_PITCH, LANES), F32),
                            pltpu.SemaphoreType.DMA((2, _ring_depth(tm)))],
        ),
        compiler_params=_cparams(1),
        name="moe_combine",
    )(slot1, slot2, ys, x1, route, g, b)


def _moe(x1, x1s, route, counts, wg, wu, wd, g, b, *, layer, tm, tf, alpha):
    r = x1.shape[0]
    n_work = (2 * r + MOE_BLOCK - 1) // MOE_BLOCK + N_EXPERTS
    slot1, slot2, work_exp, n_active = _routing_tables(route, counts, n_work=n_work)
    xs = _moe_dispatch(slot1, slot2, x1s, tm=tm, n_slots=n_work * MOE_BLOCK)
    ys = _moe_ffn(work_exp, n_active, xs, wg, wu, wd, layer=layer, tf=tf)
    return _moe_combine(slot1, slot2, ys, x1, route, g, b, tm=tm, alpha=alpha)


def _rope_tables(pos):
    half = HEAD_DIM // 2
    inv = ROPE_THETA ** (-jnp.arange(half, dtype=F32) / half)
    ang = pos.astype(F32)[:, None] * inv[None, :]
    cos = jnp.cos(ang)
    sin = jnp.sin(ang)
    cos2 = jnp.concatenate([cos, cos, cos, cos], axis=1)
    sin2 = jnp.concatenate([-sin, sin, -sin, sin], axis=1)
    return cos2, sin2


def _row_tile(r, cap):
    best = 16
    for t in range(16, cap + 1, 16):
        if r % t == 0:
            best = t
    return best


def kernel(x_prompt, x_sample, cache_k, cache_v, state_pool, w_in, w_out, attn_sinks, pool_w,
           pool_scale, sg_w, sg_b, sg_norm_g, sg_norm_b, ln1_g, ln1_b, ln2_g, ln2_b,
           ffn_w_gate, ffn_w_up, ffn_w_down, router_w, router_b, moe_w_gate, moe_w_up, moe_w_down):
    bp, tp, d = x_prompt.shape
    bs, ts, _ = x_sample.shape
    depth = w_in.shape[0]
    d_kv = cache_k.shape[3] * cache_k.shape[4]
    d_pool = pool_scale.shape[1]
    d_sg = sg_norm_g.shape[1]
    d_attn = w_in.shape[2] - 2 * d_kv - d_pool - 2 * d_sg
    win_buf = cache_k.shape[2]
    assert win_buf == WINDOW == BLK and tp % BLK == 0 and tp >= BLK and ts <= SAMPLE_PAD
    assert bs % SEQ_PER_STEP == 0 and state_pool.shape[2] == POOL_STATE
    assert cache_k.shape[3] == N_KV_HEADS and cache_k.shape[4] == HEAD_DIM
    assert d == SLAB_ROWS * LANES
    alpha = (2 * depth) ** 0.25

    r_p = bp * tp
    r_s = bs * SAMPLE_PAD
    r = r_p + r_s
    tm = _row_tile(r, 768)
    tm_ffn = _row_tile(r, 704)
    tm_out = _row_tile(math.gcd(r_p, r_s), 256)
    tm_comb = _row_tile(r, 256)

    xs_pad = jnp.pad(x_sample, ((0, 0), (0, SAMPLE_PAD - ts), (0, 0)))
    x = jnp.concatenate([x_prompt.reshape(r_p, d), xs_pad.reshape(r_s, d)], axis=0)

    pos = jnp.concatenate([
        jnp.tile(jnp.arange(tp, dtype=jnp.int32), bp),
        jnp.tile(PAST_LEN + jnp.arange(SAMPLE_PAD, dtype=jnp.int32), bs)])
    cos, sin = _rope_tables(pos)

    w_in_b = w_in.astype(BF16)
    w_out_b = w_out.astype(BF16)
    pool_w_b = pool_w.astype(BF16)
    ffn_g, ffn_u, ffn_d = (w.astype(BF16) for w in (ffn_w_gate, ffn_w_up, ffn_w_down))
    moe_bf16 = {}
    rw_pad = jnp.pad(router_w, ((0, 0), (0, 0), (0, LANES - router_w.shape[2])))
    rw_hi = rw_pad.astype(BF16)
    rw_lo = (rw_pad - rw_hi.astype(F32)).astype(BF16)
    rb_pad = jnp.pad(router_b, ((0, 0), (0, LANES - router_b.shape[1])))[:, None, :]
    sg_bias = jnp.repeat(jnp.swapaxes(sg_b, 1, 2), d_sg // N_SG_GROUPS, axis=2)
    state_pad = jnp.pad(state_pool, ((0, 0), (0, 0), (1, 0), (0, 0)))

    kp_l, vp_l, pp_l, ks_l, vs_l, ps_l, sg_l = [], [], [], [], [], [], []
    for l in range(depth):
        q, k, v, xp, u, vn = _inproj(
            x, w_in_b, cos, sin, sg_norm_g[l][None], sg_norm_b[l][None],
            layer=l, tm=tm, d_attn=d_attn, d_kv=d_kv, d_pool=d_pool, d_sg=d_sg)
        mixed_p = _mix_prompt(attn_sinks[l], q, k, v, xp, u, vn, pool_w_b[l], pool_scale[l][None],
                              sg_w[l], sg_bias[l], batch=bp, seq=tp)
        mixed_s = _mix_sample(attn_sinks[l], q, k, v, xp, u, vn,
                              cache_k[l].reshape(bs * win_buf, d_kv), cache_v[l].reshape(bs * win_buf, d_kv),
                              state_pad[l], pool_w_b[l], pool_scale[l][None], sg_w[l], sg_bias[l],
                              r_prompt=r_p, n_seq=bs)
        i = l // 2
        if l % 2 == 0:
            x1, x1b = _outproj(mixed_p, mixed_s, x, w_out_b, ln1_g[l][None], ln1_b[l][None],
                               layer=l, tm=tm_out, alpha=alpha)
            jobs = ()
            if l + 1 < depth:
                jobs = tuple((w.reshape(w.shape[0], -1, w.shape[-1]), (l + 1) // 2)
                             for w in (moe_w_gate, moe_w_up, moe_w_down))
            x, casted = _ffn(x1b, x1, ffn_g, ffn_u, ffn_d, ln2_g[l][None], ln2_b[l][None], jobs,
                             layer=i, tm=tm_ffn, tf=512, alpha=alpha)
            if jobs:
                moe_bf16[(l + 1) // 2] = tuple(
                    c.reshape((1,) + w.shape[1:])
                    for c, w in zip(casted, (moe_w_gate, moe_w_up, moe_w_down)))
        else:
            x1, x1s, route, counts = _outproj(
                mixed_p, mixed_s, x, w_out_b, ln1_g[l][None], ln1_b[l][None],
                router=(rw_hi[i], rw_lo[i], rb_pad[i]), layer=l, tm=tm_out, alpha=alpha)
            if i not in moe_bf16:
                moe_bf16[i] = tuple(w[i][None].astype(BF16)
                                    for w in (moe_w_gate, moe_w_up, moe_w_down))
            moe_g, moe_u, moe_d = moe_bf16[i]
            x = _moe(x1, x1s, route, counts, moe_g, moe_u, moe_d, ln2_g[l][None], ln2_b[l][None],
                     layer=0, tm=tm_comb, tf=512, alpha=alpha)

        kh = (N_KV_HEADS, HEAD_DIM)
        kp_l.append(k[:r_p].reshape(bp, tp, *kh)[:, -WINDOW:])
        vp_l.append(v[:r_p].reshape(bp, tp, *kh)[:, -WINDOW:])
        pp_l.append(xp[:r_p].reshape(bp, tp, d_pool)[:, -POOL_STATE:])
        k_new = k[r_p:].reshape(bs, SAMPLE_PAD, *kh)[:, :ts]
        v_new = v[r_p:].reshape(bs, SAMPLE_PAD, *kh)[:, :ts]
        ks_l.append(jnp.concatenate([cache_k[l], k_new], axis=1)[:, -win_buf:])
        vs_l.append(jnp.concatenate([cache_v[l], v_new], axis=1)[:, -win_buf:])
        xp_new = xp[r_p:].reshape(bs, SAMPLE_PAD, d_pool)[:, :ts]
        ps_l.append(jnp.concatenate([state_pool[l], xp_new], axis=1)[:, -POOL_STATE:])
        sg_l.append(vn[r_p:].reshape(bs, SAMPLE_PAD, d_sg)[:, :ts])

    y_prompt = x[:r_p].reshape(bp, tp, d)
    y_sample = x[r_p:].reshape(bs, SAMPLE_PAD, d)[:, :ts]
    return (y_prompt, y_sample, jnp.stack(kp_l), jnp.stack(vp_l), jnp.stack(pp_l),
            jnp.stack(ks_l), jnp.stack(vs_l), jnp.stack(ps_l), jnp.stack(sg_l))
```

```python
import functools
import math

import jax
import jax.numpy as jnp
from jax import lax
from jax.experimental import pallas as pl
from jax.experimental.pallas import tpu as pltpu

F32 = jnp.float32
BF16 = jnp.bfloat16

PAST_LEN = 16384
HEAD_DIM = 64
N_KV_HEADS = 2
WINDOW = 128
BLK = 128
ROPE_THETA = 10000.0
POOL_WINDOWS = (2, 4, 8, 16)
POOL_STATE = 15
N_SG_GROUPS = 4
N_EXPERTS = 8
LN_EPS = 1e-5

LANES = 128
SUBLANES = 8
SAMPLE_PAD = SUBLANES
VMEM_LIMIT = 56 * 1024 * 1024

NEG_INF = float("-inf")


def _cparams(n_axes):
    return pltpu.CompilerParams(
        dimension_semantics=("arbitrary",) * n_axes, vmem_limit_bytes=VMEM_LIMIT)


def _layer_norm(x, g, b):
    mu = jnp.mean(x, axis=-1, keepdims=True)
    xc = x - mu
    var = jnp.mean(xc * xc, axis=-1, keepdims=True)
    return xc * lax.rsqrt(var + LN_EPS) * g + b


def _gelu_tanh(x):
    return 0.5 * x * (1.0 + jnp.tanh(0.7978845608028654 * (x + 0.044715 * (x * x * x))))


def _inproj_kernel(x_ref, w_ref, cos_ref, sin_ref, g_ref, b_ref,
                   q_ref, k_ref, v_ref, xp_ref, u_ref, vn_ref, *, d_attn, d_kv, d_pool, d_sg):
    xb = x_ref[...].astype(BF16)
    cos = cos_ref[...]
    sin = sin_ref[...]
    tm = xb.shape[0]
    lane = lax.broadcasted_iota(jnp.int32, (tm, LANES), 1)
    first_half = (lane % HEAD_DIM) < (HEAD_DIM // 2)

    def rope(z):
        rot = jnp.where(first_half, pltpu.roll(z, LANES - HEAD_DIM // 2, 1),
                        pltpu.roll(z, HEAD_DIM // 2, 1))
        return z * cos + rot * sin

    c0 = 0
    zq = jnp.dot(xb, w_ref[:, c0:c0 + d_attn], preferred_element_type=F32)
    for j in range(d_attn // LANES):
        sl = slice(j * LANES, (j + 1) * LANES)
        q_ref[:, sl] = (rope(zq[:, sl]) * (HEAD_DIM ** -0.5)).astype(BF16)
    c0 += d_attn
    zkv = jnp.dot(xb, w_ref[:, c0:c0 + 2 * d_kv], preferred_element_type=F32)
    k_ref[...] = rope(zkv[:, :d_kv])
    v_ref[...] = zkv[:, d_kv:]
    c0 += 2 * d_kv
    xp_ref[...] = jnp.dot(xb, w_ref[:, c0:c0 + d_pool], preferred_element_type=F32)
    c0 += d_pool
    u_ref[...] = _gelu_tanh(jnp.dot(xb, w_ref[:, c0:c0 + d_sg], preferred_element_type=F32))
    c0 += d_sg
    vg = _gelu_tanh(jnp.dot(xb, w_ref[:, c0:c0 + d_sg], preferred_element_type=F32))
    gd = d_sg // N_SG_GROUPS
    for g in range(N_SG_GROUPS):
        sl = slice(g * gd, (g + 1) * gd)
        vn_ref[:, sl] = _layer_norm(vg[:, sl], g_ref[:, sl], b_ref[:, sl])


def _inproj(x, w, cos, sin, sg_g, sg_b, *, layer, tm, d_attn, d_kv, d_pool, d_sg):
    r, d = x.shape
    d_in = w.shape[2]
    row = lambda i: (i, 0)
    const = lambda i: (0, 0)
    outs = (
        jax.ShapeDtypeStruct((r, d_attn), BF16),
        jax.ShapeDtypeStruct((r, d_kv), F32),
        jax.ShapeDtypeStruct((r, d_kv), F32),
        jax.ShapeDtypeStruct((r, d_pool), F32),
        jax.ShapeDtypeStruct((r, d_sg), F32),
        jax.ShapeDtypeStruct((r, d_sg), F32),
    )
    return pl.pallas_call(
        functools.partial(_inproj_kernel, d_attn=d_attn, d_kv=d_kv, d_pool=d_pool, d_sg=d_sg),
        out_shape=outs,
        grid=(r // tm,),
        in_specs=[
            pl.BlockSpec((tm, d), row),
            pl.BlockSpec((None, d, d_in), lambda i: (layer, 0, 0), pipeline_mode=pl.Buffered(1)),
            pl.BlockSpec((tm, LANES), row),
            pl.BlockSpec((tm, LANES), row),
            pl.BlockSpec((1, d_sg), const),
            pl.BlockSpec((1, d_sg), const),
        ],
        out_specs=(
            pl.BlockSpec((tm, d_attn), row),
            pl.BlockSpec((tm, d_kv), row),
            pl.BlockSpec((tm, d_kv), row),
            pl.BlockSpec((tm, d_pool), row),
            pl.BlockSpec((tm, d_sg), row),
            pl.BlockSpec((tm, d_sg), row),
        ),
        compiler_params=_cparams(1),
        name="inproj",
    )(x, w, cos, sin, sg_g, sg_b)


def _split_kv(kv):
    lane = lax.broadcasted_iota(jnp.int32, kv.shape, 1)
    low = lane < HEAD_DIM
    a = kv.astype(BF16)
    b = pltpu.roll(kv, HEAD_DIM, 1).astype(BF16)
    zero = jnp.zeros_like(a)
    head0 = (jnp.where(low, a, zero), jnp.where(low, zero, b))
    head1 = (jnp.where(low, b, zero), jnp.where(low, zero, a))
    return head0, head1


def _attend(q, k_pair, v_pair, allowed, sinks):
    out = None
    for kx, vx, sk in zip(k_pair, v_pair, sinks):
        s = lax.dot_general(q, kx, (((1,), (1,)), ((), ())), preferred_element_type=F32)
        s = jnp.where(allowed, s, NEG_INF)
        m = jnp.maximum(jnp.max(s, axis=1, keepdims=True), sk)
        p = jnp.exp(s - m)
        denom = jnp.sum(p, axis=1, keepdims=True) + jnp.exp(sk - m)
        o = jnp.dot(p.astype(BF16), vx, preferred_element_type=F32) * (1.0 / denom)
        out = o if out is None else out + o
    return out


def _window_sums(full):
    sums = {1: full}
    w = 1
    while w < max(POOL_WINDOWS):
        sums[2 * w] = sums[w] + pltpu.roll(sums[w], w, 0)
        w *= 2
    return sums


def _pool_mix(full, cur, cnt_of, pw_ref, ps_ref, row0):
    n = cur.shape[0]
    gd = cur.shape[1] // len(POOL_WINDOWS)
    outs = []
    for g, w in enumerate(POOL_WINDOWS):
        sl = slice(g * gd, (g + 1) * gd)
        sums = _window_sums(full[:, sl])[w]
        pooled = sums[row0:row0 + n, :] / cnt_of(w)
        d = (pooled - cur[:, sl]).astype(BF16)
        outs.append(jnp.dot(d, pw_ref[g], preferred_element_type=F32) * ps_ref[:, sl])
    return outs


def _causal_weights(sw_ref, g):
    n = sw_ref.shape[1]
    ri = lax.broadcasted_iota(jnp.int32, (n, n), 0)
    ci = lax.broadcasted_iota(jnp.int32, (n, n), 1)
    return jnp.where(ri >= ci, sw_ref[g], 0.0).astype(BF16)


def _mix_prompt_kernel(sink_ref, q_ref, kc_ref, kp_ref, vc_ref, vp_ref, xc_ref, xt_ref, u_ref,
                       vn_ref, pw_ref, ps_ref, sw_ref, sb_ref, o_ref, *, d_attn, d_pool, d_sg):
    i = pl.program_id(1)
    has_prev = i > 0

    kall = jnp.concatenate([kp_ref[...], kc_ref[...]], axis=0)
    vall = jnp.concatenate([vp_ref[...], vc_ref[...]], axis=0)
    k_heads = _split_kv(kall)
    v_heads = _split_kv(vall)
    ri = lax.broadcasted_iota(jnp.int32, (BLK, 2 * BLK), 0)
    cj = lax.broadcasted_iota(jnp.int32, (BLK, 2 * BLK), 1)
    first_key = jnp.where(has_prev, 0, BLK)
    allowed = (cj >= ri) & (cj <= ri + WINDOW) & (cj >= first_key)
    n_pairs = d_attn // LANES
    pairs_per_kv = n_pairs // N_KV_HEADS
    for p in range(n_pairs):
        g = p // pairs_per_kv
        sl = slice(p * LANES, (p + 1) * LANES)
        o = _attend(q_ref[:, sl], k_heads[g], v_heads[g], allowed,
                    (sink_ref[2 * p], sink_ref[2 * p + 1]))
        o_ref[:, sl] = o.astype(BF16)

    xc = xc_ref[...]
    tail = jnp.where(has_prev, xt_ref[...], 0.0)
    full = jnp.concatenate([tail, xc], axis=0)
    hist = tail.shape[0]
    pos = i * BLK + lax.broadcasted_iota(jnp.int32, (BLK, 1), 0)
    cnt_of = lambda w: jnp.minimum(pos + 1, w).astype(F32)
    pooled = _pool_mix(full, xc, cnt_of, pw_ref, ps_ref, hist)
    gd = d_pool // len(POOL_WINDOWS)
    for g, y in enumerate(pooled):
        o_ref[:, d_attn + g * gd:d_attn + (g + 1) * gd] = y.astype(BF16)

    gs = d_sg // N_SG_GROUPS
    for g in range(N_SG_GROUPS):
        sl = slice(g * gs, (g + 1) * gs)
        s = jnp.dot(_causal_weights(sw_ref, g), vn_ref[:, sl].astype(BF16),
                    preferred_element_type=F32) + sb_ref[:, sl]
        c0 = d_attn + d_pool + g * gs
        o_ref[:, c0:c0 + gs] = (u_ref[:, sl] * s).astype(BF16)


def _mix_prompt(sinks, q, k, v, xp, u, vn, pool_w, pool_scale, sg_w, sg_bias, *, batch, seq):
    nb = seq // BLK
    d_attn, d_kv, d_pool, d_sg = q.shape[1], k.shape[1], xp.shape[1], u.shape[1]
    d_mix = d_attn + d_pool + d_sg
    hist = 2 * SUBLANES
    cur = lambda b, i: (b * nb + i, 0)
    prev = lambda b, i: (b * nb + jnp.maximum(i - 1, 0), 0)
    tail = lambda b, i: (jnp.maximum((b * nb + i) * (BLK // hist) - 1, 0), 0)
    const2 = lambda b, i: (0, 0)
    const3 = lambda b, i: (0, 0, 0)
    return pl.pallas_call(
        functools.partial(_mix_prompt_kernel, d_attn=d_attn, d_pool=d_pool, d_sg=d_sg),
        out_shape=jax.ShapeDtypeStruct((batch * seq, d_mix), BF16),
        grid=(batch, nb),
        in_specs=[
            pl.BlockSpec(memory_space=pltpu.SMEM),
            pl.BlockSpec((BLK, d_attn), cur),
            pl.BlockSpec((BLK, d_kv), cur),
            pl.BlockSpec((BLK, d_kv), prev),
            pl.BlockSpec((BLK, d_kv), cur),
            pl.BlockSpec((BLK, d_kv), prev),
            pl.BlockSpec((BLK, d_pool), cur),
            pl.BlockSpec((hist, d_pool), tail),
            pl.BlockSpec((BLK, d_sg), cur),
            pl.BlockSpec((BLK, d_sg), cur),
            pl.BlockSpec(pool_w.shape, const3),
            pl.BlockSpec((1, d_pool), const2),
            pl.BlockSpec(sg_w.shape, const3),
            pl.BlockSpec((BLK, d_sg), const2),
        ],
        out_specs=pl.BlockSpec((BLK, d_mix), cur),
        compiler_params=_cparams(2),
        name="mix_prompt",
    )(sinks, q, k, k, v, v, xp, xp, u, vn, pool_w, pool_scale, sg_w, sg_bias)


SEQ_PER_STEP = 2


def _mix_sample_kernel(sink_ref, q_ref, kn_ref, vn_new_ref, ck_ref, cv_ref, xn_ref, st_ref,
                       u_ref, vn_ref, pw_ref, ps_ref, sw_ref, sb_ref, o_ref, *, d_attn, d_pool, d_sg):
    sp = SAMPLE_PAD
    n_pairs = d_attn // LANES
    pairs_per_kv = n_pairs // N_KV_HEADS
    m_rows = pairs_per_kv * sp
    qf = q_ref[...].astype(F32)

    ri = lax.broadcasted_iota(jnp.int32, (m_rows, 2 * BLK), 0) % sp
    cj = lax.broadcasted_iota(jnp.int32, (m_rows, 2 * BLK), 1)
    allowed = (cj >= ri) & (cj <= ri + WINDOW)
    row_pair = lax.broadcasted_iota(jnp.int32, (m_rows, 1), 0) // sp

    attn_rows = []
    for s in range(SEQ_PER_STEP):
        rows = slice(s * sp, (s + 1) * sp)
        zpad = jnp.zeros((BLK - sp, kn_ref.shape[1]), F32)
        kall = jnp.concatenate([ck_ref[s * BLK:(s + 1) * BLK, :], kn_ref[rows, :], zpad], axis=0)
        vall = jnp.concatenate([cv_ref[s * BLK:(s + 1) * BLK, :], vn_new_ref[rows, :], zpad], axis=0)
        k_heads = _split_kv(kall)
        v_heads = _split_kv(vall)
        per_pair = []
        for g in range(N_KV_HEADS):
            qst = jnp.concatenate(
                [qf[rows, (g * pairs_per_kv + pl_) * LANES:(g * pairs_per_kv + pl_ + 1) * LANES]
                 for pl_ in range(pairs_per_kv)], axis=0).astype(BF16)
            sink_lo = jnp.zeros((m_rows, 1), F32)
            sink_hi = jnp.zeros((m_rows, 1), F32)
            for pl_ in range(pairs_per_kv):
                h = 2 * (g * pairs_per_kv + pl_)
                sink_lo = jnp.where(row_pair == pl_, sink_ref[h], sink_lo)
                sink_hi = jnp.where(row_pair == pl_, sink_ref[h + 1], sink_hi)
            o = _attend(qst, k_heads[g], v_heads[g], allowed, (sink_lo, sink_hi))
            per_pair.extend(o[pl_ * sp:(pl_ + 1) * sp, :] for pl_ in range(pairs_per_kv))
        attn_rows.append(per_pair)
    for p in range(n_pairs):
        o_ref[:, p * LANES:(p + 1) * LANES] = jnp.concatenate(
            [attn_rows[s][p] for s in range(SEQ_PER_STEP)], axis=0).astype(BF16)

    gd = d_pool // len(POOL_WINDOWS)
    pooled = []
    for s in range(SEQ_PER_STEP):
        rows = slice(s * sp, (s + 1) * sp)
        xc = xn_ref[rows, :]
        full = jnp.concatenate([st_ref[s], xc], axis=0)
        cnt_of = lambda w: float(w)
        pooled.append(_pool_mix(full, xc, cnt_of, pw_ref, ps_ref, st_ref.shape[1]))
    for g in range(len(POOL_WINDOWS)):
        o_ref[:, d_attn + g * gd:d_attn + (g + 1) * gd] = jnp.concatenate(
            [pooled[s][g] for s in range(SEQ_PER_STEP)], axis=0).astype(BF16)

    gs = d_sg // N_SG_GROUPS
    for g in range(N_SG_GROUPS):
        sl = slice(g * gs, (g + 1) * gs)
        wm = _causal_weights(sw_ref, g)
        outs = []
        for s in range(SEQ_PER_STEP):
            rows = slice(s * sp, (s + 1) * sp)
            vpad = jnp.concatenate([vn_ref[rows, sl], jnp.zeros((BLK - sp, gs), F32)], axis=0)
            sg = jnp.dot(wm, vpad.astype(BF16), preferred_element_type=F32)[:sp, :] + sb_ref[:sp, sl]
            outs.append(u_ref[rows, sl] * sg)
        c0 = d_attn + d_pool + g * gs
        o_ref[:, c0:c0 + gs] = jnp.concatenate(outs, axis=0).astype(BF16)


def _mix_sample(sinks, q, k, v, xp, u, vn, cache_k, cache_v, state, pool_w, pool_scale,
                sg_w, sg_bias, *, layer, r_prompt, n_seq):
    d_attn, d_kv, d_pool, d_sg = q.shape[1], k.shape[1], xp.shape[1], u.shape[1]
    d_mix = d_attn + d_pool + d_sg
    rows = SEQ_PER_STEP * SAMPLE_PAD
    base = r_prompt // rows
    n_steps = n_seq // SEQ_PER_STEP
    new = lambda i: (base + i, 0)
    per_seq = lambda i: (i, 0)
    carried = lambda i: (layer * n_steps + i, 0)
    const2 = lambda i: (0, 0)
    const3 = lambda i: (0, 0, 0)
    return pl.pallas_call(
        functools.partial(_mix_sample_kernel, d_attn=d_attn, d_pool=d_pool, d_sg=d_sg),
        out_shape=jax.ShapeDtypeStruct((n_seq * SAMPLE_PAD, d_mix), BF16),
        grid=(n_seq // SEQ_PER_STEP,),
        in_specs=[
            pl.BlockSpec(memory_space=pltpu.SMEM),
            pl.BlockSpec((rows, d_attn), new),
            pl.BlockSpec((rows, d_kv), new),
            pl.BlockSpec((rows, d_kv), new),
            pl.BlockSpec((SEQ_PER_STEP * BLK, d_kv), carried),
            pl.BlockSpec((SEQ_PER_STEP * BLK, d_kv), carried),
            pl.BlockSpec((rows, d_pool), new),
            pl.BlockSpec((SEQ_PER_STEP,) + state.shape[1:], lambda i: (layer * n_steps + i, 0, 0)),
            pl.BlockSpec((rows, d_sg), new),
            pl.BlockSpec((rows, d_sg), new),
            pl.BlockSpec(pool_w.shape, const3),
            pl.BlockSpec((1, d_pool), const2),
            pl.BlockSpec(sg_w.shape, const3),
            pl.BlockSpec((BLK, d_sg), const2),
        ],
        out_specs=pl.BlockSpec((rows, d_mix), per_seq),
        compiler_params=_cparams(1),
        name="mix_sample",
    )(sinks, q, k, v, cache_k, cache_v, xp, state, u, vn, pool_w, pool_scale, sg_w, sg_bias)


ROUTE_E1, ROUTE_E2, ROUTE_G1, ROUTE_G2, ROUTE_R1, ROUTE_R2 = range(6)

SLAB_PITCH = 24


def _lane_pick(rec, k):
    lane = lax.broadcasted_iota(jnp.int32, rec.shape, 1)
    return jnp.sum(jnp.where(lane == k, rec, 0.0), axis=1, keepdims=True)


def _store_slabs(slab_ref, x):
    n, d = x.shape
    slab_ref[...] = jnp.zeros_like(slab_ref)
    for j in range(d // LANES):
        slab_ref[pl.ds(j, n, stride=SLAB_PITCH), :] = x[:, j * LANES:(j + 1) * LANES]


def _load_slab_chunk(slab_ref, j, n):
    return slab_ref[pl.ds(j, n, stride=SLAB_PITCH), :]


def _outproj_kernel(*refs, alpha, with_router, n_prompt_tiles):
    if with_router:
        (mp_ref, ms_ref, x_ref, w_ref, g_ref, b_ref, rwh_ref, rwl_ref, rb_ref,
         x1_ref, x1s_ref, route_ref, cnt_ref, carry_ref) = refs
    else:
        mp_ref, ms_ref, x_ref, w_ref, g_ref, b_ref, x1_ref, x1b_ref = refs
    from_prompt = pl.program_id(0) < n_prompt_tiles
    tm = x_ref.shape[0]
    n_sub = 2 if tm % (4 * SUBLANES) == 0 else 1
    parts = []
    for h in range(n_sub):
        rows = slice(h * (tm // n_sub), (h + 1) * (tm // n_sub))
        mixed = jnp.where(from_prompt, mp_ref[rows, :], ms_ref[rows, :])
        y = alpha * x_ref[rows, :] + jnp.dot(mixed, w_ref[...], preferred_element_type=F32)
        part = _layer_norm(y, g_ref[...], b_ref[...])
        x1_ref[rows, :] = part
        if not with_router:
            x1b_ref[rows, :] = part.astype(BF16)
        parts.append(part)
    if not with_router:
        return
    x1 = jnp.concatenate(parts, axis=0)
    hi = x1.astype(BF16)
    _store_slabs(x1s_ref, x1)

    lo = (x1 - hi.astype(F32)).astype(BF16)
    logits = (jnp.dot(hi, rwh_ref[...], preferred_element_type=F32)
              + jnp.dot(lo, rwh_ref[...], preferred_element_type=F32)
              + jnp.dot(hi, rwl_ref[...], preferred_element_type=F32) + rb_ref[...])
    lane_i = lax.broadcasted_iota(jnp.int32, logits.shape, 1)
    lane = lane_i.astype(F32)
    logits = jnp.where(lane_i < N_EXPERTS, logits, NEG_INF)
    m1 = jnp.max(logits, axis=1, keepdims=True)
    i1 = jnp.min(jnp.where(logits == m1, lane, float(LANES)), axis=1, keepdims=True)
    first = lane == i1
    rest = jnp.where(first, NEG_INF, logits)
    m2 = jnp.max(rest, axis=1, keepdims=True)
    i2 = jnp.min(jnp.where(rest == m2, lane, float(LANES)), axis=1, keepdims=True)
    second = lane == i2
    e = jnp.exp(m2 - m1)
    g1 = 1.0 / (1.0 + e)
    g2 = e * g1

    @pl.when(pl.program_id(0) == 0)
    def _():
        carry_ref[...] = jnp.zeros_like(carry_ref)

    sel = jnp.where(first, 1.0, 0.0) + jnp.where(second, 1.0, 0.0)
    tm = sel.shape[0]
    ri = lax.broadcasted_iota(jnp.int32, (tm, tm), 0)
    ci = lax.broadcasted_iota(jnp.int32, (tm, tm), 1)
    earlier = jnp.where(ri > ci, 1.0, 0.0).astype(BF16)
    ranks = jnp.dot(earlier, sel.astype(BF16), preferred_element_type=F32) + carry_ref[...]
    r1 = jnp.sum(jnp.where(first, ranks, 0.0), axis=1, keepdims=True)
    r2 = jnp.sum(jnp.where(second, ranks, 0.0), axis=1, keepdims=True)
    carry_ref[...] += jnp.sum(sel, axis=0, keepdims=True)
    cnt_ref[...] = carry_ref[...]

    rec = jnp.zeros_like(logits)
    for k, val in ((ROUTE_E1, i1), (ROUTE_E2, i2), (ROUTE_G1, g1), (ROUTE_G2, g2),
                   (ROUTE_R1, r1), (ROUTE_R2, r2)):
        rec = jnp.where(lane_i == k, val, rec)
    route_ref[...] = rec


def _outproj(mixed_p, mixed_s, x, w, g, b, router=None, *, layer, tm, alpha):
    r, d = x.shape
    assert mixed_p.shape[0] % tm == 0 and mixed_s.shape[0] % tm == 0
    n_p = mixed_p.shape[0] // tm
    row = lambda i: (i, 0)
    const = lambda i: (0, 0)
    in_specs = [
        pl.BlockSpec((tm, mixed_p.shape[1]), lambda i: (jnp.minimum(i, n_p - 1), 0)),
        pl.BlockSpec((tm, mixed_s.shape[1]), lambda i: (jnp.maximum(i - n_p, 0), 0)),
        pl.BlockSpec((tm, d), row),
        pl.BlockSpec((None,) + w.shape[1:], lambda i: (layer, 0, 0), pipeline_mode=pl.Buffered(1)),
        pl.BlockSpec((1, d), const),
        pl.BlockSpec((1, d), const),
    ]
    args = [mixed_p, mixed_s, x, w, g, b]
    scratch = []
    if router is None:
        outs = [jax.ShapeDtypeStruct((r, d), F32), jax.ShapeDtypeStruct((r, d), BF16)]
        out_specs = [pl.BlockSpec((tm, d), row), pl.BlockSpec((tm, d), row)]
    else:
        in_specs += [pl.BlockSpec((d, LANES), const), pl.BlockSpec((d, LANES), const),
                     pl.BlockSpec((1, LANES), const)]
        args += list(router)
        outs = [jax.ShapeDtypeStruct((r, d), F32),
                jax.ShapeDtypeStruct((r * SLAB_PITCH, LANES), F32),
                jax.ShapeDtypeStruct((r, LANES), F32),
                jax.ShapeDtypeStruct((1, LANES), F32)]
        out_specs = [pl.BlockSpec((tm, d), row),
                     pl.BlockSpec((tm * SLAB_PITCH, LANES), row),
                     pl.BlockSpec((tm, LANES), row),
                     pl.BlockSpec((1, LANES), const)]
        scratch = [pltpu.VMEM((1, LANES), F32)]
    return pl.pallas_call(
        functools.partial(_outproj_kernel, alpha=alpha, with_router=router is not None,
                          n_prompt_tiles=n_p),
        out_shape=tuple(outs),
        grid=(r // tm,),
        in_specs=in_specs,
        out_specs=tuple(out_specs),
        scratch_shapes=scratch,
        compiler_params=_cparams(1),
        name="outproj_router" if router is not None else "outproj",
    )(*args)


def _swiglu(xb, wg_ref, wu_ref):
    a = jnp.dot(xb, wg_ref[...], preferred_element_type=F32)
    c = jnp.dot(xb, wu_ref[...], preferred_element_type=F32)
    return (a * (1.0 / (1.0 + jnp.exp(-a))) * c).astype(BF16)


def _ffn_kernel(*refs, alpha, n_cast):
    xb_ref, x1_ref, wg_ref, wu_ref, wd_ref, g_ref, b_ref = refs[:7]
    cast_src = refs[7:7 + n_cast]
    o_ref = refs[7 + n_cast]
    cast_dst = refs[8 + n_cast:]
    f = pl.program_id(1)

    @pl.when(f == 0)
    def _():
        o_ref[...] = alpha * x1_ref[...]

    o_ref[...] += jnp.dot(_swiglu(xb_ref[...], wg_ref, wu_ref), wd_ref[...],
                          preferred_element_type=F32)

    @pl.when(f == pl.num_programs(1) - 1)
    def _():
        o_ref[...] = _layer_norm(o_ref[...], g_ref[...], b_ref[...])

    for src, dst in zip(cast_src, cast_dst):
        dst[...] = src[...].astype(BF16)


def _cast_dst_map(i, f, *, nf, last):
    return (jnp.minimum(i * nf + f, last), 0)


def _cast_src_map(i, f, *, layer, nf, last):
    return (layer,) + _cast_dst_map(i, f, nf=nf, last=last)


def _cast_chunk_rows(rows, steps):
    for t in range(2 * SUBLANES, rows + 1, 2 * SUBLANES):
        if rows % t == 0 and rows // t <= steps:
            return t
    return rows


def _ffn(xb, x1, wg, wu, wd, g, b, cast_jobs=(), *, layer, tm, tf, alpha):
    r, d = x1.shape
    ff = wg.shape[2]
    nf = ff // tf
    steps = (r // tm) * nf
    row = lambda i, f: (i, 0)
    const = lambda i, f: (0, 0)
    in_specs = [
        pl.BlockSpec((tm, d), row),
        pl.BlockSpec((tm, d), row, pipeline_mode=pl.Buffered(1)),
        pl.BlockSpec((None, d, tf), lambda i, f: (layer, 0, f)),
        pl.BlockSpec((None, d, tf), lambda i, f: (layer, 0, f)),
        pl.BlockSpec((None, tf, d), lambda i, f: (layer, f, 0)),
        pl.BlockSpec((1, d), const),
        pl.BlockSpec((1, d), const),
    ]
    out_shape = [jax.ShapeDtypeStruct((r, d), F32)]
    out_specs = [pl.BlockSpec((tm, d), row)]
    args = [xb, x1, wg, wu, wd, g, b]
    for arr, li in cast_jobs:
        _, rows, cols = arr.shape
        cr = _cast_chunk_rows(rows, steps)
        src_map = functools.partial(_cast_src_map, layer=li, nf=nf, last=rows // cr - 1)
        dst_map = functools.partial(_cast_dst_map, nf=nf, last=rows // cr - 1)
        in_specs.append(pl.BlockSpec((None, cr, cols), src_map))
        out_shape.append(jax.ShapeDtypeStruct((rows, cols), BF16))
        out_specs.append(pl.BlockSpec((cr, cols), dst_map))
        args.append(arr)
    outs = pl.pallas_call(
        functools.partial(_ffn_kernel, alpha=alpha, n_cast=len(cast_jobs)),
        out_shape=tuple(out_shape),
        grid=(r // tm, nf),
        in_specs=in_specs,
        out_specs=tuple(out_specs),
        compiler_params=_cparams(2),
        name="ffn",
    )(*args)
    return outs[0], outs[1:]


MOE_BLOCK = 512
DMA_RING = 128


def _ring_depth(n):
    return 1 << (min(DMA_RING, n).bit_length() - 1)


def _routing_tables(route, counts, *, n_work):
    e1 = route[:, ROUTE_E1].astype(jnp.int32)
    e2 = route[:, ROUTE_E2].astype(jnp.int32)
    r1 = route[:, ROUTE_R1].astype(jnp.int32)
    r2 = route[:, ROUTE_R2].astype(jnp.int32)
    cnt = counts[0, :N_EXPERTS].astype(jnp.int32)
    nblk = (cnt + MOE_BLOCK - 1) // MOE_BLOCK
    blk_end = jnp.cumsum(nblk)
    start = (blk_end - nblk) * MOE_BLOCK
    slot1 = jnp.take(start, e1) + r1
    slot2 = jnp.take(start, e2) + r2
    n_active = blk_end[-1]
    blk = jnp.minimum(jnp.arange(n_work, dtype=jnp.int32), n_active - 1)
    work_exp = jnp.sum(blk[:, None] >= blk_end[None, :], axis=1).astype(jnp.int32)
    return slot1, slot2, work_exp, n_active.reshape(1)


def _ring_copies(n, ring, make_copies):
    def start(i):
        for c in make_copies(i):
            c.start()

    def wait(i):
        for c in make_copies(i):
            c.wait()

    def fill(i, carry):
        start(i)
        return carry

    def steady(i, carry):
        wait(i - ring)
        start(i)
        return carry

    def drain(i, carry):
        wait(i)
        return carry

    lax.fori_loop(0, ring, fill, 0)
    lax.fori_loop(ring, n, steady, 0)
    lax.fori_loop(n - ring, n, drain, 0)


SLAB_ROWS = 16


def _slab(ref, token):
    return ref.at[pl.ds(pl.multiple_of(token * SLAB_PITCH, SUBLANES), SLAB_ROWS), :]


def _dispatch_kernel(s1_ref, s2_ref, x1s_ref, xs_init, xs_hbm, sem):
    del xs_init
    tm = x1s_ref.shape[0] // SLAB_PITCH
    base = pl.program_id(0) * tm
    ring = sem.shape[1]

    def copies(r):
        k = r & (ring - 1)
        src = _slab(x1s_ref, r)
        return (pltpu.make_async_copy(src, _slab(xs_hbm, s1_ref[base + r]), sem.at[0, k]),
                pltpu.make_async_copy(src, _slab(xs_hbm, s2_ref[base + r]), sem.at[1, k]))

    _ring_copies(tm, ring, copies)


def _moe_dispatch(slot1, slot2, x1s, xs_init, *, tm):
    r = x1s.shape[0] // SLAB_PITCH
    return pl.pallas_call(
        _dispatch_kernel,
        out_shape=jax.ShapeDtypeStruct(xs_init.shape, F32),
        grid_spec=pltpu.PrefetchScalarGridSpec(
            num_scalar_prefetch=2,
            grid=(r // tm,),
            in_specs=[pl.BlockSpec((tm * SLAB_PITCH, LANES), lambda i, s1, s2: (i, 0)),
                      pl.BlockSpec(memory_space=pl.ANY)],
            out_specs=pl.BlockSpec(memory_space=pl.ANY),
            scratch_shapes=[pltpu.SemaphoreType.DMA((2, _ring_depth(tm)))],
        ),
        input_output_aliases={3: 0},
        compiler_params=_cparams(1),
        name="moe_dispatch",
    )(slot1, slot2, x1s, xs_init)


def _moe_ffn_kernel(we_ref, na_ref, xs_ref, wg_ref, wu_ref, wd_ref, ys_ref, xb_ref, acc_ref):
    del we_ref
    w = pl.program_id(0)
    f = pl.program_id(1)
    n, d = acc_ref.shape

    @pl.when(f == 0)
    def _():
        for j in range(d // LANES):
            xb_ref[:, j * LANES:(j + 1) * LANES] = _load_slab_chunk(xs_ref, j, n).astype(BF16)
        acc_ref[...] = jnp.zeros_like(acc_ref)

    @pl.when(w < na_ref[0])
    def _():
        acc_ref[...] += jnp.dot(_swiglu(xb_ref[...], wg_ref, wu_ref), wd_ref[...],
                                preferred_element_type=F32)

    @pl.when(f == pl.num_programs(1) - 1)
    def _():
        _store_slabs(ys_ref, acc_ref[...])


def _moe_ffn(work_exp, n_active, xs, wg, wu, wd, *, layer, tf):
    d = wg.shape[2]
    ff = wg.shape[3]
    nf = ff // tf
    rows = lambda w, f, we, na: (w, 0)
    chunk = lambda w, f, na: jnp.where(w < na[0], f, nf - 1)
    return pl.pallas_call(
        _moe_ffn_kernel,
        out_shape=jax.ShapeDtypeStruct(xs.shape, F32),
        grid_spec=pltpu.PrefetchScalarGridSpec(
            num_scalar_prefetch=2,
            grid=(xs.shape[0] // (MOE_BLOCK * SLAB_PITCH), nf),
            in_specs=[
                pl.BlockSpec((MOE_BLOCK * SLAB_PITCH, LANES), rows),
                pl.BlockSpec((None, None, d, tf),
                             lambda w, f, we, na: (layer, we[w], 0, chunk(w, f, na))),
                pl.BlockSpec((None, None, d, tf),
                             lambda w, f, we, na: (layer, we[w], 0, chunk(w, f, na))),
                pl.BlockSpec((None, None, tf, d),
                             lambda w, f, we, na: (layer, we[w], chunk(w, f, na), 0)),
            ],
            out_specs=pl.BlockSpec((MOE_BLOCK * SLAB_PITCH, LANES), rows),
            scratch_shapes=[pltpu.VMEM((MOE_BLOCK, d), BF16), pltpu.VMEM((MOE_BLOCK, d), F32)],
        ),
        compiler_params=_cparams(2),
        name="moe_ffn",
    )(work_exp, n_active, xs, wg, wu, wd)


def _combine_kernel(s1_ref, s2_ref, ys_hbm, x1_ref, route_ref, g_ref, b_ref, o_ref,
                    y1_ref, y2_ref, sem, *, alpha):
    tm, d = o_ref.shape
    base = pl.program_id(0) * tm
    ring = sem.shape[1]

    def copies(r):
        k = r & (ring - 1)
        return (pltpu.make_async_copy(_slab(ys_hbm, s1_ref[base + r]), _slab(y1_ref, r), sem.at[0, k]),
                pltpu.make_async_copy(_slab(ys_hbm, s2_ref[base + r]), _slab(y2_ref, r), sem.at[1, k]))

    _ring_copies(tm, ring, copies)
    route = route_ref[...]
    g1 = _lane_pick(route, ROUTE_G1)
    g2 = _lane_pick(route, ROUTE_G2)
    for j in range(d // LANES):
        sl = slice(j * LANES, (j + 1) * LANES)
        o_ref[:, sl] = (alpha * x1_ref[:, sl] + g1 * _load_slab_chunk(y1_ref, j, tm)
                        + g2 * _load_slab_chunk(y2_ref, j, tm))
    o_ref[...] = _layer_norm(o_ref[...], g_ref[...], b_ref[...])


def _moe_combine(slot1, slot2, ys, x1, route, g, b, *, tm, alpha):
    r, d = x1.shape
    row = lambda i, s1, s2: (i, 0)
    const = lambda i, s1, s2: (0, 0)
    return pl.pallas_call(
        functools.partial(_combine_kernel, alpha=alpha),
        out_shape=jax.ShapeDtypeStruct((r, d), F32),
        grid_spec=pltpu.PrefetchScalarGridSpec(
            num_scalar_prefetch=2,
            grid=(r // tm,),
            in_specs=[
                pl.BlockSpec(memory_space=pl.ANY),
                pl.BlockSpec((tm, d), row),
                pl.BlockSpec((tm, LANES), row),
                pl.BlockSpec((1, d), const),
                pl.BlockSpec((1, d), const),
            ],
            out_specs=pl.BlockSpec((tm, d), row),
            scratch_shapes=[pltpu.VMEM((tm * SLAB_PITCH, LANES), F32),
                            pltpu.VMEM((tm * SLAB_PITCH, LANES), F32),
                            pltpu.SemaphoreType.DMA((2, _ring_depth(tm)))],
        ),
        compiler_params=_cparams(1),
        name="moe_combine",
    )(slot1, slot2, ys, x1, route, g, b)


def _moe(x1, x1s, route, counts, wg, wu, wd, g, b, xs_buf=None, *, layer, tm, tf, alpha):
    r = x1.shape[0]
    n_work = (2 * r + MOE_BLOCK - 1) // MOE_BLOCK + N_EXPERTS
    slot1, slot2, work_exp, n_active = _routing_tables(route, counts, n_work=n_work)
    if xs_buf is None:
        xs_buf = jnp.zeros((n_work * MOE_BLOCK * SLAB_PITCH, LANES), F32)
    xs = _moe_dispatch(slot1, slot2, x1s, xs_buf, tm=tm)
    ys = _moe_ffn(work_exp, n_active, xs, wg, wu, wd, layer=layer, tf=tf)
    return _moe_combine(slot1, slot2, ys, x1, route, g, b, tm=tm, alpha=alpha), xs


def _rope_tables(pos):
    half = HEAD_DIM // 2
    inv = ROPE_THETA ** (-jnp.arange(half, dtype=F32) / half)
    ang = pos.astype(F32)[:, None] * inv[None, :]
    cos = jnp.cos(ang)
    sin = jnp.sin(ang)
    cos2 = jnp.concatenate([cos, cos, cos, cos], axis=1)
    sin2 = jnp.concatenate([-sin, sin, -sin, sin], axis=1)
    return cos2, sin2


def _row_tile(r, cap):
    best = 16
    for t in range(16, cap + 1, 16):
        if r % t == 0:
            best = t
    return best


def kernel(x_prompt, x_sample, cache_k, cache_v, state_pool, w_in, w_out, attn_sinks, pool_w,
           pool_scale, sg_w, sg_b, sg_norm_g, sg_norm_b, ln1_g, ln1_b, ln2_g, ln2_b,
           ffn_w_gate, ffn_w_up, ffn_w_down, router_w, router_b, moe_w_gate, moe_w_up, moe_w_down):
    bp, tp, d = x_prompt.shape
    bs, ts, _ = x_sample.shape
    depth = w_in.shape[0]
    d_kv = cache_k.shape[3] * cache_k.shape[4]
    d_pool = pool_scale.shape[1]
    d_sg = sg_norm_g.shape[1]
    d_attn = w_in.shape[2] - 2 * d_kv - d_pool - 2 * d_sg
    win_buf = cache_k.shape[2]
    assert win_buf == WINDOW == BLK and tp % BLK == 0 and tp >= BLK and ts <= SAMPLE_PAD
    assert bs % SEQ_PER_STEP == 0 and state_pool.shape[2] == POOL_STATE
    assert cache_k.shape[3] == N_KV_HEADS and cache_k.shape[4] == HEAD_DIM
    assert d == SLAB_ROWS * LANES
    alpha = (2 * depth) ** 0.25

    r_p = bp * tp
    r_s = bs * SAMPLE_PAD
    r = r_p + r_s
    tm = _row_tile(r, 768)
    tm_ffn = _row_tile(r, 704)
    tm_out = _row_tile(math.gcd(r_p, r_s), 256)
    tm_comb = _row_tile(r, 384)

    xs_pad = jnp.pad(x_sample, ((0, 0), (0, SAMPLE_PAD - ts), (0, 0)))
    x = jnp.concatenate([x_prompt.reshape(r_p, d), xs_pad.reshape(r_s, d)], axis=0)

    pos = jnp.concatenate([
        jnp.tile(jnp.arange(tp, dtype=jnp.int32), bp),
        jnp.tile(PAST_LEN + jnp.arange(SAMPLE_PAD, dtype=jnp.int32), bs)])
    cos, sin = _rope_tables(pos)

    w_in_b = w_in.astype(BF16)
    w_out_b = w_out.astype(BF16)
    pool_w_b = pool_w.astype(BF16)
    ffn_g, ffn_u, ffn_d = (w.astype(BF16) for w in (ffn_w_gate, ffn_w_up, ffn_w_down))
    moe_bf16 = {}
    rw_pad = jnp.pad(router_w, ((0, 0), (0, 0), (0, LANES - router_w.shape[2])))
    rw_hi = rw_pad.astype(BF16)
    rw_lo = (rw_pad - rw_hi.astype(F32)).astype(BF16)
    rb_pad = jnp.pad(router_b, ((0, 0), (0, LANES - router_b.shape[1])))[:, None, :]
    sg_bias = jnp.repeat(jnp.swapaxes(sg_b, 1, 2), d_sg // N_SG_GROUPS, axis=2)
    state_pad = jnp.pad(state_pool, ((0, 0), (0, 0), (1, 0), (0, 0)))
    state_rows = state_pad.reshape((depth * bs,) + state_pad.shape[2:])
    ck_rows = cache_k.reshape(depth * bs * win_buf, d_kv)
    cv_rows = cache_v.reshape(depth * bs * win_buf, d_kv)

    k_l, v_l, xp_l, vn_l = [], [], [], []
    xs_buf = None
    for l in range(depth):
        q, k, v, xp, u, vn = _inproj(
            x, w_in_b, cos, sin, sg_norm_g[l][None], sg_norm_b[l][None],
            layer=l, tm=tm, d_attn=d_attn, d_kv=d_kv, d_pool=d_pool, d_sg=d_sg)
        mixed_p = _mix_prompt(attn_sinks[l], q, k, v, xp, u, vn, pool_w_b[l], pool_scale[l][None],
                              sg_w[l], sg_bias[l], batch=bp, seq=tp)
        mixed_s = _mix_sample(attn_sinks[l], q, k, v, xp, u, vn, ck_rows, cv_rows, state_rows,
                              pool_w_b[l], pool_scale[l][None], sg_w[l], sg_bias[l],
                              layer=l, r_prompt=r_p, n_seq=bs)
        i = l // 2
        if l % 2 == 0:
            x1, x1b = _outproj(mixed_p, mixed_s, x, w_out_b, ln1_g[l][None], ln1_b[l][None],
                               layer=l, tm=tm_out, alpha=alpha)
            jobs = ()
            if l + 1 < depth:
                jobs = tuple((w.reshape(w.shape[0], -1, w.shape[-1]), (l + 1) // 2)
                             for w in (moe_w_gate, moe_w_up, moe_w_down))
            x, casted = _ffn(x1b, x1, ffn_g, ffn_u, ffn_d, ln2_g[l][None], ln2_b[l][None], jobs,
                             layer=i, tm=tm_ffn, tf=512, alpha=alpha)
            if jobs:
                moe_bf16[(l + 1) // 2] = tuple(
                    c.reshape((1,) + w.shape[1:])
                    for c, w in zip(casted, (moe_w_gate, moe_w_up, moe_w_down)))
        else:
            x1, x1s, route, counts = _outproj(
                mixed_p, mixed_s, x, w_out_b, ln1_g[l][None], ln1_b[l][None],
                router=(rw_hi[i], rw_lo[i], rb_pad[i]), layer=l, tm=tm_out, alpha=alpha)
            if i not in moe_bf16:
                moe_bf16[i] = tuple(w[i][None].astype(BF16)
                                    for w in (moe_w_gate, moe_w_up, moe_w_down))
            moe_g, moe_u, moe_d = moe_bf16[i]
            x, xs_buf = _moe(x1, x1s, route, counts, moe_g, moe_u, moe_d, ln2_g[l][None],
                             ln2_b[l][None], xs_buf, layer=0, tm=tm_comb, tf=512, alpha=alpha)

        tail = max(WINDOW, POOL_STATE)
        keep = lambda a: jnp.concatenate(
            [a[:r_p].reshape(bp, tp, -1)[:, -tail:].reshape(bp * tail, -1), a[r_p:]], axis=0)
        k_l.append(keep(k))
        v_l.append(keep(v))
        xp_l.append(keep(xp))
        vn_l.append(vn[r_p:])

    kh = (N_KV_HEADS, HEAD_DIM)
    n_tail = bp * max(WINDOW, POOL_STATE)
    ks, vs, xps, vns = (jnp.stack(a) for a in (k_l, v_l, xp_l, vn_l))
    prompt_tail = lambda a, n: a[:, :n_tail].reshape(depth, bp, -1, a.shape[-1])[:, :, -n:]
    sample_new = lambda a: a[:, -r_s:].reshape(depth, bs, SAMPLE_PAD, a.shape[-1])[:, :, :ts]
    new_k_p = prompt_tail(ks, WINDOW).reshape(depth, bp, WINDOW, *kh)
    new_v_p = prompt_tail(vs, WINDOW).reshape(depth, bp, WINDOW, *kh)
    new_pool_p = prompt_tail(xps, POOL_STATE)
    new_k_s = jnp.concatenate(
        [cache_k, sample_new(ks).reshape(depth, bs, ts, *kh)], axis=2)[:, :, -win_buf:]
    new_v_s = jnp.concatenate(
        [cache_v, sample_new(vs).reshape(depth, bs, ts, *kh)], axis=2)[:, :, -win_buf:]
    new_pool_s = jnp.concatenate([state_pool, sample_new(xps)], axis=2)[:, :, -POOL_STATE:]
    new_sg_s = vns.reshape(depth, bs, SAMPLE_PAD, d_sg)[:, :, :ts]

    y_prompt = x[:r_p].reshape(bp, tp, d)
    y_sample = x[r_p:].reshape(bs, SAMPLE_PAD, d)[:, :ts]
    return (y_prompt, y_sample, new_k_p, new_v_p, new_pool_p, new_k_s, new_v_s, new_pool_s, new_sg_s)
```

```python
import functools
import math

import jax
import jax.numpy as jnp
from jax import lax
from jax.experimental import pallas as pl
from jax.experimental.pallas import tpu as pltpu

F32 = jnp.float32
BF16 = jnp.bfloat16

PAST_LEN = 16384
HEAD_DIM = 64
N_KV_HEADS = 2
WINDOW = 128
BLK = 128
ROPE_THETA = 10000.0
POOL_WINDOWS = (2, 4, 8, 16)
POOL_STATE = 15
N_SG_GROUPS = 4
N_EXPERTS = 8
LN_EPS = 1e-5

LANES = 128
SUBLANES = 8
SAMPLE_PAD = SUBLANES
VMEM_LIMIT = 56 * 1024 * 1024

NEG_INF = float("-inf")


def _cparams(n_axes):
    return pltpu.CompilerParams(
        dimension_semantics=("arbitrary",) * n_axes, vmem_limit_bytes=VMEM_LIMIT)


def _layer_norm(x, g, b):
    mu = jnp.mean(x, axis=-1, keepdims=True)
    xc = x - mu
    var = jnp.mean(xc * xc, axis=-1, keepdims=True)
    return xc * lax.rsqrt(var + LN_EPS) * g + b


def _gelu_tanh(x):
    return 0.5 * x * (1.0 + jnp.tanh(0.7978845608028654 * (x + 0.044715 * (x * x * x))))


def _inproj_kernel(x_ref, w_ref, cos_ref, sin_ref, g_ref, b_ref,
                   q_ref, k_ref, v_ref, xp_ref, u_ref, vn_ref, *, d_attn, d_kv, d_pool, d_sg):
    xb = x_ref[...].astype(BF16)
    cos = cos_ref[...]
    sin = sin_ref[...]
    tm = xb.shape[0]
    lane = lax.broadcasted_iota(jnp.int32, (tm, LANES), 1)
    first_half = (lane % HEAD_DIM) < (HEAD_DIM // 2)

    def rope(z):
        rot = jnp.where(first_half, pltpu.roll(z, LANES - HEAD_DIM // 2, 1),
                        pltpu.roll(z, HEAD_DIM // 2, 1))
        return z * cos + rot * sin

    c0 = 0
    zq = jnp.dot(xb, w_ref[:, c0:c0 + d_attn], preferred_element_type=F32)
    for j in range(d_attn // LANES):
        sl = slice(j * LANES, (j + 1) * LANES)
        q_ref[:, sl] = (rope(zq[:, sl]) * (HEAD_DIM ** -0.5)).astype(BF16)
    c0 += d_attn
    zkv = jnp.dot(xb, w_ref[:, c0:c0 + 2 * d_kv], preferred_element_type=F32)
    k_ref[...] = rope(zkv[:, :d_kv])
    v_ref[...] = zkv[:, d_kv:]
    c0 += 2 * d_kv
    xp_ref[...] = jnp.dot(xb, w_ref[:, c0:c0 + d_pool], preferred_element_type=F32)
    c0 += d_pool
    u_ref[...] = _gelu_tanh(jnp.dot(xb, w_ref[:, c0:c0 + d_sg], preferred_element_type=F32))
    c0 += d_sg
    vg = _gelu_tanh(jnp.dot(xb, w_ref[:, c0:c0 + d_sg], preferred_element_type=F32))
    gd = d_sg // N_SG_GROUPS
    for g in range(N_SG_GROUPS):
        sl = slice(g * gd, (g + 1) * gd)
        vn_ref[:, sl] = _layer_norm(vg[:, sl], g_ref[:, sl], b_ref[:, sl])


def _inproj(x, w, cos, sin, sg_g, sg_b, *, layer, tm, d_attn, d_kv, d_pool, d_sg):
    r, d = x.shape
    d_in = w.shape[2]
    row = lambda i: (i, 0)
    const = lambda i: (0, 0)
    outs = (
        jax.ShapeDtypeStruct((r, d_attn), BF16),
        jax.ShapeDtypeStruct((r, d_kv), F32),
        jax.ShapeDtypeStruct((r, d_kv), F32),
        jax.ShapeDtypeStruct((r, d_pool), F32),
        jax.ShapeDtypeStruct((r, d_sg), F32),
        jax.ShapeDtypeStruct((r, d_sg), F32),
    )
    return pl.pallas_call(
        functools.partial(_inproj_kernel, d_attn=d_attn, d_kv=d_kv, d_pool=d_pool, d_sg=d_sg),
        out_shape=outs,
        grid=(r // tm,),
        in_specs=[
            pl.BlockSpec((tm, d), row),
            pl.BlockSpec((None, d, d_in), lambda i: (layer, 0, 0), pipeline_mode=pl.Buffered(1)),
            pl.BlockSpec((tm, LANES), row),
            pl.BlockSpec((tm, LANES), row),
            pl.BlockSpec((1, d_sg), const),
            pl.BlockSpec((1, d_sg), const),
        ],
        out_specs=(
            pl.BlockSpec((tm, d_attn), row),
            pl.BlockSpec((tm, d_kv), row),
            pl.BlockSpec((tm, d_kv), row),
            pl.BlockSpec((tm, d_pool), row),
            pl.BlockSpec((tm, d_sg), row),
            pl.BlockSpec((tm, d_sg), row),
        ),
        compiler_params=_cparams(1),
        name="inproj",
    )(x, w, cos, sin, sg_g, sg_b)


def _split_kv(kv):
    lane = lax.broadcasted_iota(jnp.int32, kv.shape, 1)
    low = lane < HEAD_DIM
    a = kv.astype(BF16)
    b = pltpu.roll(kv, HEAD_DIM, 1).astype(BF16)
    zero = jnp.zeros_like(a)
    head0 = (jnp.where(low, a, zero), jnp.where(low, zero, b))
    head1 = (jnp.where(low, b, zero), jnp.where(low, zero, a))
    return head0, head1


def _attend(q, k_pair, v_pair, allowed, sinks):
    out = None
    for kx, vx, sk in zip(k_pair, v_pair, sinks):
        s = lax.dot_general(q, kx, (((1,), (1,)), ((), ())), preferred_element_type=F32)
        s = jnp.where(allowed, s, NEG_INF)
        m = jnp.maximum(jnp.max(s, axis=1, keepdims=True), sk)
        p = jnp.exp(s - m)
        denom = jnp.sum(p, axis=1, keepdims=True) + jnp.exp(sk - m)
        o = jnp.dot(p.astype(BF16), vx, preferred_element_type=F32) * (1.0 / denom)
        out = o if out is None else out + o
    return out


def _window_sums(full):
    sums = {1: full}
    w = 1
    while w < max(POOL_WINDOWS):
        sums[2 * w] = sums[w] + pltpu.roll(sums[w], w, 0)
        w *= 2
    return sums


def _pool_mix(full, cur, cnt_of, pw_ref, ps_ref, row0):
    n = cur.shape[0]
    gd = cur.shape[1] // len(POOL_WINDOWS)
    outs = []
    for g, w in enumerate(POOL_WINDOWS):
        sl = slice(g * gd, (g + 1) * gd)
        sums = _window_sums(full[:, sl])[w]
        pooled = sums[row0:row0 + n, :] / cnt_of(w)
        d = (pooled - cur[:, sl]).astype(BF16)
        outs.append(jnp.dot(d, pw_ref[g], preferred_element_type=F32) * ps_ref[:, sl])
    return outs


def _causal_weights(sw_ref, g):
    n = sw_ref.shape[1]
    ri = lax.broadcasted_iota(jnp.int32, (n, n), 0)
    ci = lax.broadcasted_iota(jnp.int32, (n, n), 1)
    return jnp.where(ri >= ci, sw_ref[g], 0.0).astype(BF16)


def _prompt_mixers(i, sink_ref, q_ref, kc_ref, kp_ref, vc_ref, vp_ref, xc_ref, xt_ref, u_ref,
                   vn_ref, pw_ref, ps_ref, sw_ref, sb_ref, o_ref, *, d_attn, d_pool, d_sg):
    has_prev = i > 0

    kall = jnp.concatenate([kp_ref[...], kc_ref[...]], axis=0)
    vall = jnp.concatenate([vp_ref[...], vc_ref[...]], axis=0)
    k_heads = _split_kv(kall)
    v_heads = _split_kv(vall)
    ri = lax.broadcasted_iota(jnp.int32, (BLK, 2 * BLK), 0)
    cj = lax.broadcasted_iota(jnp.int32, (BLK, 2 * BLK), 1)
    first_key = jnp.where(has_prev, 0, BLK)
    allowed = (cj >= ri) & (cj <= ri + WINDOW) & (cj >= first_key)
    n_pairs = d_attn // LANES
    pairs_per_kv = n_pairs // N_KV_HEADS
    for p in range(n_pairs):
        g = p // pairs_per_kv
        sl = slice(p * LANES, (p + 1) * LANES)
        o = _attend(q_ref[:, sl], k_heads[g], v_heads[g], allowed,
                    (sink_ref[2 * p], sink_ref[2 * p + 1]))
        o_ref[:, sl] = o.astype(BF16)

    xc = xc_ref[...]
    tail = jnp.where(has_prev, xt_ref[...], 0.0)
    full = jnp.concatenate([tail, xc], axis=0)
    hist = tail.shape[0]
    pos = i * BLK + lax.broadcasted_iota(jnp.int32, (BLK, 1), 0)
    cnt_of = lambda w: jnp.minimum(pos + 1, w).astype(F32)
    pooled = _pool_mix(full, xc, cnt_of, pw_ref, ps_ref, hist)
    gd = d_pool // len(POOL_WINDOWS)
    for g, y in enumerate(pooled):
        o_ref[:, d_attn + g * gd:d_attn + (g + 1) * gd] = y.astype(BF16)

    gs = d_sg // N_SG_GROUPS
    for g in range(N_SG_GROUPS):
        sl = slice(g * gs, (g + 1) * gs)
        s = jnp.dot(_causal_weights(sw_ref, g), vn_ref[:, sl].astype(BF16),
                    preferred_element_type=F32) + sb_ref[:, sl]
        c0 = d_attn + d_pool + g * gs
        o_ref[:, c0:c0 + gs] = (u_ref[:, sl] * s).astype(BF16)


SEQ_PER_STEP = 2


def _mix_sample_kernel(sink_ref, q_ref, kn_ref, vn_new_ref, ck_ref, cv_ref, xn_ref, st_ref,
                       u_ref, vn_ref, pw_ref, ps_ref, sw_ref, sb_ref, o_ref, *, d_attn, d_pool, d_sg):
    sp = SAMPLE_PAD
    n_pairs = d_attn // LANES
    pairs_per_kv = n_pairs // N_KV_HEADS
    m_rows = pairs_per_kv * sp
    qf = q_ref[...].astype(F32)

    ri = lax.broadcasted_iota(jnp.int32, (m_rows, 2 * BLK), 0) % sp
    cj = lax.broadcasted_iota(jnp.int32, (m_rows, 2 * BLK), 1)
    allowed = (cj >= ri) & (cj <= ri + WINDOW)
    row_pair = lax.broadcasted_iota(jnp.int32, (m_rows, 1), 0) // sp

    attn_rows = []
    for s in range(SEQ_PER_STEP):
        rows = slice(s * sp, (s + 1) * sp)
        zpad = jnp.zeros((BLK - sp, kn_ref.shape[1]), F32)
        kall = jnp.concatenate([ck_ref[s * BLK:(s + 1) * BLK, :], kn_ref[rows, :], zpad], axis=0)
        vall = jnp.concatenate([cv_ref[s * BLK:(s + 1) * BLK, :], vn_new_ref[rows, :], zpad], axis=0)
        k_heads = _split_kv(kall)
        v_heads = _split_kv(vall)
        per_pair = []
        for g in range(N_KV_HEADS):
            qst = jnp.concatenate(
                [qf[rows, (g * pairs_per_kv + pl_) * LANES:(g * pairs_per_kv + pl_ + 1) * LANES]
                 for pl_ in range(pairs_per_kv)], axis=0).astype(BF16)
            sink_lo = jnp.zeros((m_rows, 1), F32)
            sink_hi = jnp.zeros((m_rows, 1), F32)
            for pl_ in range(pairs_per_kv):
                h = 2 * (g * pairs_per_kv + pl_)
                sink_lo = jnp.where(row_pair == pl_, sink_ref[h], sink_lo)
                sink_hi = jnp.where(row_pair == pl_, sink_ref[h + 1], sink_hi)
            o = _attend(qst, k_heads[g], v_heads[g], allowed, (sink_lo, sink_hi))
            per_pair.extend(o[pl_ * sp:(pl_ + 1) * sp, :] for pl_ in range(pairs_per_kv))
        attn_rows.append(per_pair)
    for p in range(n_pairs):
        o_ref[:, p * LANES:(p + 1) * LANES] = jnp.concatenate(
            [attn_rows[s][p] for s in range(SEQ_PER_STEP)], axis=0).astype(BF16)

    gd = d_pool // len(POOL_WINDOWS)
    pooled = []
    for s in range(SEQ_PER_STEP):
        rows = slice(s * sp, (s + 1) * sp)
        xc = xn_ref[rows, :]
        full = jnp.concatenate([st_ref[s], xc], axis=0)
        cnt_of = lambda w: float(w)
        pooled.append(_pool_mix(full, xc, cnt_of, pw_ref, ps_ref, st_ref.shape[1]))
    for g in range(len(POOL_WINDOWS)):
        o_ref[:, d_attn + g * gd:d_attn + (g + 1) * gd] = jnp.concatenate(
            [pooled[s][g] for s in range(SEQ_PER_STEP)], axis=0).astype(BF16)

    gs = d_sg // N_SG_GROUPS
    for g in range(N_SG_GROUPS):
        sl = slice(g * gs, (g + 1) * gs)
        wm = _causal_weights(sw_ref, g)
        outs = []
        for s in range(SEQ_PER_STEP):
            rows = slice(s * sp, (s + 1) * sp)
            vpad = jnp.concatenate([vn_ref[rows, sl], jnp.zeros((BLK - sp, gs), F32)], axis=0)
            sg = jnp.dot(wm, vpad.astype(BF16), preferred_element_type=F32)[:sp, :] + sb_ref[:sp, sl]
            outs.append(u_ref[rows, sl] * sg)
        c0 = d_attn + d_pool + g * gs
        o_ref[:, c0:c0 + gs] = jnp.concatenate(outs, axis=0).astype(BF16)


def _mix_sample(sinks, q, k, v, xp, u, vn, cache_k, cache_v, state, pool_w, pool_scale,
                sg_w, sg_bias, *, layer, r_prompt, n_seq):
    d_attn, d_kv, d_pool, d_sg = q.shape[1], k.shape[1], xp.shape[1], u.shape[1]
    d_mix = d_attn + d_pool + d_sg
    rows = SEQ_PER_STEP * SAMPLE_PAD
    base = r_prompt // rows
    n_steps = n_seq // SEQ_PER_STEP
    new = lambda i: (base + i, 0)
    per_seq = lambda i: (i, 0)
    carried = lambda i: (layer * n_steps + i, 0)
    const2 = lambda i: (0, 0)
    const3 = lambda i: (0, 0, 0)
    return pl.pallas_call(
        functools.partial(_mix_sample_kernel, d_attn=d_attn, d_pool=d_pool, d_sg=d_sg),
        out_shape=jax.ShapeDtypeStruct((n_seq * SAMPLE_PAD, d_mix), BF16),
        grid=(n_seq // SEQ_PER_STEP,),
        in_specs=[
            pl.BlockSpec(memory_space=pltpu.SMEM),
            pl.BlockSpec((rows, d_attn), new),
            pl.BlockSpec((rows, d_kv), new),
            pl.BlockSpec((rows, d_kv), new),
            pl.BlockSpec((SEQ_PER_STEP * BLK, d_kv), carried),
            pl.BlockSpec((SEQ_PER_STEP * BLK, d_kv), carried),
            pl.BlockSpec((rows, d_pool), new),
            pl.BlockSpec((SEQ_PER_STEP,) + state.shape[1:], lambda i: (layer * n_steps + i, 0, 0)),
            pl.BlockSpec((rows, d_sg), new),
            pl.BlockSpec((rows, d_sg), new),
            pl.BlockSpec(pool_w.shape, const3),
            pl.BlockSpec((1, d_pool), const2),
            pl.BlockSpec(sg_w.shape, const3),
            pl.BlockSpec((BLK, d_sg), const2),
        ],
        out_specs=pl.BlockSpec((rows, d_mix), per_seq),
        compiler_params=_cparams(1),
        name="mix_sample",
    )(sinks, q, k, v, cache_k, cache_v, xp, state, u, vn, pool_w, pool_scale, sg_w, sg_bias)


ROUTE_E1, ROUTE_E2, ROUTE_G1, ROUTE_G2, ROUTE_R1, ROUTE_R2 = range(6)

SLAB_PITCH = 24


def _lane_pick(rec, k):
    lane = lax.broadcasted_iota(jnp.int32, rec.shape, 1)
    return jnp.sum(jnp.where(lane == k, rec, 0.0), axis=1, keepdims=True)


def _store_slabs(slab_ref, x):
    n, d = x.shape
    slab_ref[...] = jnp.zeros_like(slab_ref)
    for j in range(d // LANES):
        slab_ref[pl.ds(j, n, stride=SLAB_PITCH), :] = x[:, j * LANES:(j + 1) * LANES]


def _load_slab_chunk(slab_ref, j, n):
    return slab_ref[pl.ds(j, n, stride=SLAB_PITCH), :]


def _route(x1, counted, rwh_ref, rwl_ref, rb_ref, route_ref, cnt_ref, carry_ref):
    hi = x1.astype(BF16)
    lo = (x1 - hi.astype(F32)).astype(BF16)
    logits = (jnp.dot(hi, rwh_ref[...], preferred_element_type=F32)
              + jnp.dot(lo, rwh_ref[...], preferred_element_type=F32)
              + jnp.dot(hi, rwl_ref[...], preferred_element_type=F32) + rb_ref[...])
    lane_i = lax.broadcasted_iota(jnp.int32, logits.shape, 1)
    lane = lane_i.astype(F32)
    logits = jnp.where(lane_i < N_EXPERTS, logits, NEG_INF)
    m1 = jnp.max(logits, axis=1, keepdims=True)
    i1 = jnp.min(jnp.where(logits == m1, lane, float(LANES)), axis=1, keepdims=True)
    first = lane == i1
    rest = jnp.where(first, NEG_INF, logits)
    m2 = jnp.max(rest, axis=1, keepdims=True)
    i2 = jnp.min(jnp.where(rest == m2, lane, float(LANES)), axis=1, keepdims=True)
    second = lane == i2
    e = jnp.exp(m2 - m1)
    g1 = 1.0 / (1.0 + e)
    g2 = e * g1

    sel = (jnp.where(first, 1.0, 0.0) + jnp.where(second, 1.0, 0.0)) * counted
    tm = sel.shape[0]
    ri = lax.broadcasted_iota(jnp.int32, (tm, tm), 0)
    ci = lax.broadcasted_iota(jnp.int32, (tm, tm), 1)
    earlier = jnp.where(ri > ci, 1.0, 0.0).astype(BF16)
    ranks = jnp.dot(earlier, sel.astype(BF16), preferred_element_type=F32) + carry_ref[...]
    r1 = jnp.sum(jnp.where(first, ranks, 0.0), axis=1, keepdims=True)
    r2 = jnp.sum(jnp.where(second, ranks, 0.0), axis=1, keepdims=True)
    carry_ref[...] += jnp.sum(sel, axis=0, keepdims=True)
    cnt_ref[...] = carry_ref[...]

    rec = jnp.zeros_like(logits)
    for k, val in ((ROUTE_E1, i1), (ROUTE_E2, i2), (ROUTE_G1, g1), (ROUTE_G2, g2),
                   (ROUTE_R1, r1), (ROUTE_R2, r2)):
        rec = jnp.where(lane_i == k, val, rec)
    route_ref[...] = rec


N_MIX_REFS = 15


def _mix_out_kernel(*refs, alpha, with_router, n_prompt_blocks, blocks_per_seq, d_attn, d_pool, d_sg):
    mix_refs = refs[:N_MIX_REFS - 1]
    ms_ref, x_ref, w_ref, g_ref, b_ref = refs[N_MIX_REFS - 1:N_MIX_REFS + 4]
    rest = refs[N_MIX_REFS + 4:]
    if with_router:
        (rwh_ref, rwl_ref, rb_ref, x1_ref, x1s_ref, route_ref, cnt_ref,
         prev_ref, next_ref, carry_ref) = rest
    else:
        x1_ref, x1b_ref, prev_ref, next_ref = rest
    s = pl.program_id(0)

    @pl.when(s == 0)
    def _():
        prev_ref[...] = jnp.zeros_like(prev_ref)
        if with_router:
            carry_ref[...] = jnp.zeros_like(carry_ref)

    def project():
        y = alpha * x_ref[...] + jnp.dot(prev_ref[...], w_ref[...], preferred_element_type=F32)
        x1 = _layer_norm(y, g_ref[...], b_ref[...])
        x1_ref[...] = x1
        if with_router:
            _store_slabs(x1s_ref, x1)
            counted = jnp.where(s > 0, 1.0, 0.0)
            _route(x1, counted, rwh_ref, rwl_ref, rb_ref, route_ref, cnt_ref, carry_ref)
        else:
            x1b_ref[...] = x1.astype(BF16)

    @pl.when(s < n_prompt_blocks)
    def _():
        project()
        _prompt_mixers(lax.rem(s, blocks_per_seq), *mix_refs, next_ref,
                       d_attn=d_attn, d_pool=d_pool, d_sg=d_sg)

    @pl.when(s >= n_prompt_blocks)
    def _():
        project()
        next_ref[...] = ms_ref[...]

    prev_ref[...] = next_ref[...]


def _mix_out(sinks, q, k, v, xp, u, vn, pool_w, pool_scale, sg_w, sg_bias, mixed_s, x, w, g, b,
             router=None, *, layer, n_prompt, seq, alpha):
    r, d = x.shape
    d_attn, d_kv, d_pool, d_sg = q.shape[1], k.shape[1], xp.shape[1], u.shape[1]
    d_mix = d_attn + d_pool + d_sg
    assert n_prompt % BLK == 0 and (r - n_prompt) % BLK == 0 and mixed_s.shape[0] == r - n_prompt
    n_p = n_prompt // BLK
    n_s = (r - n_prompt) // BLK
    hist = 2 * SUBLANES
    pblk = lambda s: jnp.minimum(s, n_p - 1)
    cur = lambda s: (pblk(s), 0)
    prev = lambda s: (jnp.maximum(pblk(s) - 1, 0), 0)
    tail = lambda s: (jnp.maximum(pblk(s) * (BLK // hist) - 1, 0), 0)
    samp = lambda s: (jnp.clip(s - n_p, 0, n_s - 1), 0)
    row = lambda s: (jnp.maximum(s - 1, 0), 0)
    const2 = lambda s: (0, 0)
    const3 = lambda s: (0, 0, 0)
    in_specs = [
        pl.BlockSpec(memory_space=pltpu.SMEM),
        pl.BlockSpec((BLK, d_attn), cur),
        pl.BlockSpec((BLK, d_kv), cur),
        pl.BlockSpec((BLK, d_kv), prev),
        pl.BlockSpec((BLK, d_kv), cur),
        pl.BlockSpec((BLK, d_kv), prev),
        pl.BlockSpec((BLK, d_pool), cur),
        pl.BlockSpec((hist, d_pool), tail),
        pl.BlockSpec((BLK, d_sg), cur),
        pl.BlockSpec((BLK, d_sg), cur),
        pl.BlockSpec(pool_w.shape, const3),
        pl.BlockSpec((1, d_pool), const2),
        pl.BlockSpec(sg_w.shape, const3),
        pl.BlockSpec((BLK, d_sg), const2),
        pl.BlockSpec((BLK, d_mix), samp),
        pl.BlockSpec((BLK, d), row),
        pl.BlockSpec((None,) + w.shape[1:], lambda s: (layer, 0, 0), pipeline_mode=pl.Buffered(1)),
        pl.BlockSpec((1, d), const2),
        pl.BlockSpec((1, d), const2),
    ]
    args = [sinks, q, k, k, v, v, xp, xp, u, vn, pool_w, pool_scale, sg_w, sg_bias, mixed_s, x, w, g, b]
    assert len(args) == N_MIX_REFS + 4
    scratch = [pltpu.VMEM((BLK, d_mix), BF16), pltpu.VMEM((BLK, d_mix), BF16)]
    if router is None:
        outs = [jax.ShapeDtypeStruct((r, d), F32), jax.ShapeDtypeStruct((r, d), BF16)]
        out_specs = [pl.BlockSpec((BLK, d), row), pl.BlockSpec((BLK, d), row)]
    else:
        in_specs += [pl.BlockSpec((d, LANES), const2), pl.BlockSpec((d, LANES), const2),
                     pl.BlockSpec((1, LANES), const2)]
        args += list(router)
        outs = [jax.ShapeDtypeStruct((r, d), F32),
                jax.ShapeDtypeStruct((r * SLAB_PITCH, LANES), F32),
                jax.ShapeDtypeStruct((r, LANES), F32),
                jax.ShapeDtypeStruct((1, LANES), F32)]
        out_specs = [pl.BlockSpec((BLK, d), row),
                     pl.BlockSpec((BLK * SLAB_PITCH, LANES), row),
                     pl.BlockSpec((BLK, LANES), row),
                     pl.BlockSpec((1, LANES), const2)]
        scratch.append(pltpu.VMEM((1, LANES), F32))
    return pl.pallas_call(
        functools.partial(_mix_out_kernel, alpha=alpha, with_router=router is not None,
                          n_prompt_blocks=n_p, blocks_per_seq=seq // BLK,
                          d_attn=d_attn, d_pool=d_pool, d_sg=d_sg),
        out_shape=tuple(outs),
        grid=(r // BLK + 1,),
        in_specs=in_specs,
        out_specs=tuple(out_specs),
        scratch_shapes=scratch,
        compiler_params=_cparams(1),
        name="mix_out_router" if router is not None else "mix_out",
    )(*args)


def _swiglu(xb, wg_ref, wu_ref):
    a = jnp.dot(xb, wg_ref[...], preferred_element_type=F32)
    c = jnp.dot(xb, wu_ref[...], preferred_element_type=F32)
    return (a * (1.0 / (1.0 + jnp.exp(-a))) * c).astype(BF16)


def _ffn_kernel(*refs, alpha, n_cast):
    xb_ref, x1_ref, wg_ref, wu_ref, wd_ref, g_ref, b_ref = refs[:7]
    cast_src = refs[7:7 + n_cast]
    o_ref = refs[7 + n_cast]
    cast_dst = refs[8 + n_cast:]
    f = pl.program_id(1)

    @pl.when(f == 0)
    def _():
        o_ref[...] = alpha * x1_ref[...]

    o_ref[...] += jnp.dot(_swiglu(xb_ref[...], wg_ref, wu_ref), wd_ref[...],
                          preferred_element_type=F32)

    @pl.when(f == pl.num_programs(1) - 1)
    def _():
        o_ref[...] = _layer_norm(o_ref[...], g_ref[...], b_ref[...])

    for src, dst in zip(cast_src, cast_dst):
        dst[...] = src[...].astype(BF16)


def _cast_dst_map(i, f, *, nf, last):
    return (jnp.minimum(i * nf + f, last), 0)


def _cast_src_map(i, f, *, layer, nf, last):
    return (layer,) + _cast_dst_map(i, f, nf=nf, last=last)


def _cast_chunk_rows(rows, steps):
    for t in range(2 * SUBLANES, rows + 1, 2 * SUBLANES):
        if rows % t == 0 and rows // t <= steps:
            return t
    return rows


def _ffn(xb, x1, wg, wu, wd, g, b, cast_jobs=(), *, layer, tm, tf, alpha):
    r, d = x1.shape
    ff = wg.shape[2]
    nf = ff // tf
    steps = (r // tm) * nf
    row = lambda i, f: (i, 0)
    const = lambda i, f: (0, 0)
    in_specs = [
        pl.BlockSpec((tm, d), row),
        pl.BlockSpec((tm, d), row, pipeline_mode=pl.Buffered(1)),
        pl.BlockSpec((None, d, tf), lambda i, f: (layer, 0, f)),
        pl.BlockSpec((None, d, tf), lambda i, f: (layer, 0, f)),
        pl.BlockSpec((None, tf, d), lambda i, f: (layer, f, 0)),
        pl.BlockSpec((1, d), const),
        pl.BlockSpec((1, d), const),
    ]
    out_shape = [jax.ShapeDtypeStruct((r, d), F32)]
    out_specs = [pl.BlockSpec((tm, d), row)]
    args = [xb, x1, wg, wu, wd, g, b]
    for arr, li in cast_jobs:
        _, rows, cols = arr.shape
        cr = _cast_chunk_rows(rows, steps)
        src_map = functools.partial(_cast_src_map, layer=li, nf=nf, last=rows // cr - 1)
        dst_map = functools.partial(_cast_dst_map, nf=nf, last=rows // cr - 1)
        in_specs.append(pl.BlockSpec((None, cr, cols), src_map))
        out_shape.append(jax.ShapeDtypeStruct((rows, cols), BF16))
        out_specs.append(pl.BlockSpec((cr, cols), dst_map))
        args.append(arr)
    outs = pl.pallas_call(
        functools.partial(_ffn_kernel, alpha=alpha, n_cast=len(cast_jobs)),
        out_shape=tuple(out_shape),
        grid=(r // tm, nf),
        in_specs=in_specs,
        out_specs=tuple(out_specs),
        compiler_params=_cparams(2),
        name="ffn",
    )(*args)
    return outs[0], outs[1:]


MOE_BLOCK = 512
DMA_RING = 128


def _ring_depth(n):
    return 1 << (min(DMA_RING, n).bit_length() - 1)


def _routing_tables(route, counts, *, n_work):
    e1 = route[:, ROUTE_E1].astype(jnp.int32)
    e2 = route[:, ROUTE_E2].astype(jnp.int32)
    r1 = route[:, ROUTE_R1].astype(jnp.int32)
    r2 = route[:, ROUTE_R2].astype(jnp.int32)
    cnt = counts[0, :N_EXPERTS].astype(jnp.int32)
    nblk = (cnt + MOE_BLOCK - 1) // MOE_BLOCK
    blk_end = jnp.cumsum(nblk)
    start = (blk_end - nblk) * MOE_BLOCK
    slot1 = jnp.take(start, e1) + r1
    slot2 = jnp.take(start, e2) + r2
    n_active = blk_end[-1]
    blk = jnp.minimum(jnp.arange(n_work, dtype=jnp.int32), n_active - 1)
    work_exp = jnp.sum(blk[:, None] >= blk_end[None, :], axis=1).astype(jnp.int32)
    return slot1, slot2, work_exp, n_active.reshape(1)


def _ring_copies(n, ring, make_copies):
    def start(i):
        for c in make_copies(i):
            c.start()

    def wait(i):
        for c in make_copies(i):
            c.wait()

    def fill(i, carry):
        start(i)
        return carry

    def steady(i, carry):
        wait(i - ring)
        start(i)
        return carry

    def drain(i, carry):
        wait(i)
        return carry

    lax.fori_loop(0, ring, fill, 0)
    lax.fori_loop(ring, n, steady, 0)
    lax.fori_loop(n - ring, n, drain, 0)


SLAB_ROWS = 16


def _slab(ref, token):
    return ref.at[pl.ds(pl.multiple_of(token * SLAB_PITCH, SUBLANES), SLAB_ROWS), :]


def _dispatch_kernel(s1_ref, s2_ref, x1s_ref, xs_init, xs_hbm, sem):
    del xs_init
    tm = x1s_ref.shape[0] // SLAB_PITCH
    base = pl.program_id(0) * tm
    ring = sem.shape[1]

    def copies(r):
        k = r & (ring - 1)
        src = _slab(x1s_ref, r)
        return (pltpu.make_async_copy(src, _slab(xs_hbm, s1_ref[base + r]), sem.at[0, k]),
                pltpu.make_async_copy(src, _slab(xs_hbm, s2_ref[base + r]), sem.at[1, k]))

    _ring_copies(tm, ring, copies)


def _moe_dispatch(slot1, slot2, x1s, xs_init, *, tm):
    r = x1s.shape[0] // SLAB_PITCH
    return pl.pallas_call(
        _dispatch_kernel,
        out_shape=jax.ShapeDtypeStruct(xs_init.shape, F32),
        grid_spec=pltpu.PrefetchScalarGridSpec(
            num_scalar_prefetch=2,
            grid=(r // tm,),
            in_specs=[pl.BlockSpec((tm * SLAB_PITCH, LANES), lambda i, s1, s2: (i, 0)),
                      pl.BlockSpec(memory_space=pl.ANY)],
            out_specs=pl.BlockSpec(memory_space=pl.ANY),
            scratch_shapes=[pltpu.SemaphoreType.DMA((2, _ring_depth(tm)))],
        ),
        input_output_aliases={3: 0},
        compiler_params=_cparams(1),
        name="moe_dispatch",
    )(slot1, slot2, x1s, xs_init)


def _moe_ffn_kernel(we_ref, na_ref, xs_ref, wg_ref, wu_ref, wd_ref, ys_ref, xb_ref, acc_ref):
    del we_ref
    w = pl.program_id(0)
    f = pl.program_id(1)
    n, d = acc_ref.shape

    @pl.when(f == 0)
    def _():
        for j in range(d // LANES):
            xb_ref[:, j * LANES:(j + 1) * LANES] = _load_slab_chunk(xs_ref, j, n).astype(BF16)
        acc_ref[...] = jnp.zeros_like(acc_ref)

    @pl.when(w < na_ref[0])
    def _():
        acc_ref[...] += jnp.dot(_swiglu(xb_ref[...], wg_ref, wu_ref), wd_ref[...],
                                preferred_element_type=F32)

    @pl.when(f == pl.num_programs(1) - 1)
    def _():
        _store_slabs(ys_ref, acc_ref[...])


def _moe_ffn(work_exp, n_active, xs, wg, wu, wd, *, layer, tf):
    d = wg.shape[2]
    ff = wg.shape[3]
    nf = ff // tf
    rows = lambda w, f, we, na: (w, 0)
    chunk = lambda w, f, na: jnp.where(w < na[0], f, nf - 1)
    return pl.pallas_call(
        _moe_ffn_kernel,
        out_shape=jax.ShapeDtypeStruct(xs.shape, F32),
        grid_spec=pltpu.PrefetchScalarGridSpec(
            num_scalar_prefetch=2,
            grid=(xs.shape[0] // (MOE_BLOCK * SLAB_PITCH), nf),
            in_specs=[
                pl.BlockSpec((MOE_BLOCK * SLAB_PITCH, LANES), rows),
                pl.BlockSpec((None, None, d, tf),
                             lambda w, f, we, na: (layer, we[w], 0, chunk(w, f, na))),
                pl.BlockSpec((None, None, d, tf),
                             lambda w, f, we, na: (layer, we[w], 0, chunk(w, f, na))),
                pl.BlockSpec((None, None, tf, d),
                             lambda w, f, we, na: (layer, we[w], chunk(w, f, na), 0)),
            ],
            out_specs=pl.BlockSpec((MOE_BLOCK * SLAB_PITCH, LANES), rows),
            scratch_shapes=[pltpu.VMEM((MOE_BLOCK, d), BF16), pltpu.VMEM((MOE_BLOCK, d), F32)],
        ),
        compiler_params=_cparams(2),
        name="moe_ffn",
    )(work_exp, n_active, xs, wg, wu, wd)


def _combine_kernel(s1_ref, s2_ref, ys_hbm, x1_ref, route_ref, g_ref, b_ref, o_ref,
                    y1_ref, y2_ref, sem, *, alpha):
    tm, d = o_ref.shape
    base = pl.program_id(0) * tm
    ring = sem.shape[1]

    def copies(r):
        k = r & (ring - 1)
        return (pltpu.make_async_copy(_slab(ys_hbm, s1_ref[base + r]), _slab(y1_ref, r), sem.at[0, k]),
                pltpu.make_async_copy(_slab(ys_hbm, s2_ref[base + r]), _slab(y2_ref, r), sem.at[1, k]))

    _ring_copies(tm, ring, copies)
    route = route_ref[...]
    g1 = _lane_pick(route, ROUTE_G1)
    g2 = _lane_pick(route, ROUTE_G2)
    for j in range(d // LANES):
        sl = slice(j * LANES, (j + 1) * LANES)
        o_ref[:, sl] = (alpha * x1_ref[:, sl] + g1 * _load_slab_chunk(y1_ref, j, tm)
                        + g2 * _load_slab_chunk(y2_ref, j, tm))
    o_ref[...] = _layer_norm(o_ref[...], g_ref[...], b_ref[...])


def _moe_combine(slot1, slot2, ys, x1, route, g, b, *, tm, alpha):
    r, d = x1.shape
    row = lambda i, s1, s2: (i, 0)
    const = lambda i, s1, s2: (0, 0)
    return pl.pallas_call(
        functools.partial(_combine_kernel, alpha=alpha),
        out_shape=jax.ShapeDtypeStruct((r, d), F32),
        grid_spec=pltpu.PrefetchScalarGridSpec(
            num_scalar_prefetch=2,
            grid=(r // tm,),
            in_specs=[
                pl.BlockSpec(memory_space=pl.ANY),
                pl.BlockSpec((tm, d), row),
                pl.BlockSpec((tm, LANES), row),
                pl.BlockSpec((1, d), const),
                pl.BlockSpec((1, d), const),
            ],
            out_specs=pl.BlockSpec((tm, d), row),
            scratch_shapes=[pltpu.VMEM((tm * SLAB_PITCH, LANES), F32),
                            pltpu.VMEM((tm * SLAB_PITCH, LANES), F32),
                            pltpu.SemaphoreType.DMA((2, _ring_depth(tm)))],
        ),
        compiler_params=_cparams(1),
        name="moe_combine",
    )(slot1, slot2, ys, x1, route, g, b)


def _moe(x1, x1s, route, counts, wg, wu, wd, g, b, xs_buf=None, *, layer, tm, tf, alpha):
    r = x1.shape[0]
    n_work = (2 * r + MOE_BLOCK - 1) // MOE_BLOCK + N_EXPERTS
    slot1, slot2, work_exp, n_active = _routing_tables(route, counts, n_work=n_work)
    if xs_buf is None:
        xs_buf = jnp.zeros((n_work * MOE_BLOCK * SLAB_PITCH, LANES), F32)
    xs = _moe_dispatch(slot1, slot2, x1s, xs_buf, tm=tm)
    ys = _moe_ffn(work_exp, n_active, xs, wg, wu, wd, layer=layer, tf=tf)
    return _moe_combine(slot1, slot2, ys, x1, route, g, b, tm=tm, alpha=alpha), xs


def _rope_tables(pos):
    half = HEAD_DIM // 2
    inv = ROPE_THETA ** (-jnp.arange(half, dtype=F32) / half)
    ang = pos.astype(F32)[:, None] * inv[None, :]
    cos = jnp.cos(ang)
    sin = jnp.sin(ang)
    cos2 = jnp.concatenate([cos, cos, cos, cos], axis=1)
    sin2 = jnp.concatenate([-sin, sin, -sin, sin], axis=1)
    return cos2, sin2


def _row_tile(r, cap):
    best = 16
    for t in range(16, cap + 1, 16):
        if r % t == 0:
            best = t
    return best


def kernel(x_prompt, x_sample, cache_k, cache_v, state_pool, w_in, w_out, attn_sinks, pool_w,
           pool_scale, sg_w, sg_b, sg_norm_g, sg_norm_b, ln1_g, ln1_b, ln2_g, ln2_b,
           ffn_w_gate, ffn_w_up, ffn_w_down, router_w, router_b, moe_w_gate, moe_w_up, moe_w_down):
    bp, tp, d = x_prompt.shape
    bs, ts, _ = x_sample.shape
    depth = w_in.shape[0]
    d_kv = cache_k.shape[3] * cache_k.shape[4]
    d_pool = pool_scale.shape[1]
    d_sg = sg_norm_g.shape[1]
    d_attn = w_in.shape[2] - 2 * d_kv - d_pool - 2 * d_sg
    win_buf = cache_k.shape[2]
    assert win_buf == WINDOW == BLK and tp % BLK == 0 and tp >= BLK and ts <= SAMPLE_PAD
    assert bs % SEQ_PER_STEP == 0 and state_pool.shape[2] == POOL_STATE
    assert cache_k.shape[3] == N_KV_HEADS and cache_k.shape[4] == HEAD_DIM
    assert d == SLAB_ROWS * LANES
    alpha = (2 * depth) ** 0.25

    r_p = bp * tp
    r_s = bs * SAMPLE_PAD
    r = r_p + r_s
    tm = _row_tile(r, 768)
    tm_ffn = _row_tile(r, 704)
    tm_comb = _row_tile(r, 384)

    xs_pad = jnp.pad(x_sample, ((0, 0), (0, SAMPLE_PAD - ts), (0, 0)))
    x = jnp.concatenate([x_prompt.reshape(r_p, d), xs_pad.reshape(r_s, d)], axis=0)

    pos = jnp.concatenate([
        jnp.tile(jnp.arange(tp, dtype=jnp.int32), bp),
        jnp.tile(PAST_LEN + jnp.arange(SAMPLE_PAD, dtype=jnp.int32), bs)])
    cos, sin = _rope_tables(pos)

    w_in_b = w_in.astype(BF16)
    w_out_b = w_out.astype(BF16)
    pool_w_b = pool_w.astype(BF16)
    ffn_g, ffn_u, ffn_d = (w.astype(BF16) for w in (ffn_w_gate, ffn_w_up, ffn_w_down))
    moe_bf16 = {}
    rw_pad = jnp.pad(router_w, ((0, 0), (0, 0), (0, LANES - router_w.shape[2])))
    rw_hi = rw_pad.astype(BF16)
    rw_lo = (rw_pad - rw_hi.astype(F32)).astype(BF16)
    rb_pad = jnp.pad(router_b, ((0, 0), (0, LANES - router_b.shape[1])))[:, None, :]
    sg_bias = jnp.repeat(jnp.swapaxes(sg_b, 1, 2), d_sg // N_SG_GROUPS, axis=2)
    state_pad = jnp.pad(state_pool, ((0, 0), (0, 0), (1, 0), (0, 0)))
    state_rows = state_pad.reshape((depth * bs,) + state_pad.shape[2:])
    ck_rows = cache_k.reshape(depth * bs * win_buf, d_kv)
    cv_rows = cache_v.reshape(depth * bs * win_buf, d_kv)

    k_l, v_l, xp_l, vn_l = [], [], [], []
    xs_buf = None
    for l in range(depth):
        q, k, v, xp, u, vn = _inproj(
            x, w_in_b, cos, sin, sg_norm_g[l][None], sg_norm_b[l][None],
            layer=l, tm=tm, d_attn=d_attn, d_kv=d_kv, d_pool=d_pool, d_sg=d_sg)
        mixed_s = _mix_sample(attn_sinks[l], q, k, v, xp, u, vn, ck_rows, cv_rows, state_rows,
                              pool_w_b[l], pool_scale[l][None], sg_w[l], sg_bias[l],
                              layer=l, r_prompt=r_p, n_seq=bs)
        mix_out = functools.partial(
            _mix_out, attn_sinks[l], q, k, v, xp, u, vn, pool_w_b[l], pool_scale[l][None], sg_w[l],
            sg_bias[l], mixed_s, x, w_out_b, ln1_g[l][None], ln1_b[l][None],
            layer=l, n_prompt=r_p, seq=tp, alpha=alpha)
        i = l // 2
        if l % 2 == 0:
            x1, x1b = mix_out()
            jobs = ()
            if l + 1 < depth:
                jobs = tuple((w.reshape(w.shape[0], -1, w.shape[-1]), (l + 1) // 2)
                             for w in (moe_w_gate, moe_w_up, moe_w_down))
            x, casted = _ffn(x1b, x1, ffn_g, ffn_u, ffn_d, ln2_g[l][None], ln2_b[l][None], jobs,
                             layer=i, tm=tm_ffn, tf=512, alpha=alpha)
            if jobs:
                moe_bf16[(l + 1) // 2] = tuple(
                    c.reshape((1,) + w.shape[1:])
                    for c, w in zip(casted, (moe_w_gate, moe_w_up, moe_w_down)))
        else:
            x1, x1s, route, counts = mix_out(router=(rw_hi[i], rw_lo[i], rb_pad[i]))
            if i not in moe_bf16:
                moe_bf16[i] = tuple(w[i][None].astype(BF16)
                                    for w in (moe_w_gate, moe_w_up, moe_w_down))
            moe_g, moe_u, moe_d = moe_bf16[i]
            x, xs_buf = _moe(x1, x1s, route, counts, moe_g, moe_u, moe_d, ln2_g[l][None],
                             ln2_b[l][None], xs_buf, layer=0, tm=tm_comb, tf=512, alpha=alpha)

        tail = max(WINDOW, POOL_STATE)
        keep = lambda a: jnp.concatenate(
            [a[:r_p].reshape(bp, tp, -1)[:, -tail:].reshape(bp * tail, -1), a[r_p:]], axis=0)
        k_l.append(keep(k))
        v_l.append(keep(v))
        xp_l.append(keep(xp))
        vn_l.append(vn[r_p:])

    kh = (N_KV_HEADS, HEAD_DIM)
    n_tail = bp * max(WINDOW, POOL_STATE)
    ks, vs, xps, vns = (jnp.stack(a) for a in (k_l, v_l, xp_l, vn_l))
    prompt_tail = lambda a, n: a[:, :n_tail].reshape(depth, bp, -1, a.shape[-1])[:, :, -n:]
    sample_new = lambda a: a[:, -r_s:].reshape(depth, bs, SAMPLE_PAD, a.shape[-1])[:, :, :ts]
    new_k_p = prompt_tail(ks, WINDOW).reshape(depth, bp, WINDOW, *kh)
    new_v_p = prompt_tail(vs, WINDOW).reshape(depth, bp, WINDOW, *kh)
    new_pool_p = prompt_tail(xps, POOL_STATE)
    new_k_s = jnp.concatenate(
        [cache_k, sample_new(ks).reshape(depth, bs, ts, *kh)], axis=2)[:, :, -win_buf:]
    new_v_s = jnp.concatenate(
        [cache_v, sample_new(vs).reshape(depth, bs, ts, *kh)], axis=2)[:, :, -win_buf:]
    new_pool_s = jnp.concatenate([state_pool, sample_new(xps)], axis=2)[:, :, -POOL_STATE:]
    new_sg_s = vns.reshape(depth, bs, SAMPLE_PAD, d_sg)[:, :, :ts]

    y_prompt = x[:r_p].reshape(bp, tp, d)
    y_sample = x[r_p:].reshape(bs, SAMPLE_PAD, d)[:, :ts]
    return (y_prompt, y_sample, new_k_p, new_v_p, new_pool_p, new_k_s, new_v_s, new_pool_s, new_sg_s)
```

```python
import functools
import math

import jax
import jax.numpy as jnp
from jax import lax
from jax.experimental import pallas as pl
from jax.experimental.pallas import tpu as pltpu

F32 = jnp.float32
BF16 = jnp.bfloat16

PAST_LEN = 16384
HEAD_DIM = 64
N_KV_HEADS = 2
WINDOW = 128
BLK = 128
ROPE_THETA = 10000.0
POOL_WINDOWS = (2, 4, 8, 16)
POOL_STATE = 15
N_SG_GROUPS = 4
N_EXPERTS = 8
LN_EPS = 1e-5

LANES = 128
SUBLANES = 8
SAMPLE_PAD = SUBLANES
VMEM_LIMIT = 56 * 1024 * 1024

NEG_INF = float("-inf")


def _cparams(n_axes):
    return pltpu.CompilerParams(
        dimension_semantics=("arbitrary",) * n_axes, vmem_limit_bytes=VMEM_LIMIT)


def _layer_norm(x, g, b):
    mu = jnp.mean(x, axis=-1, keepdims=True)
    xc = x - mu
    var = jnp.mean(xc * xc, axis=-1, keepdims=True)
    return xc * lax.rsqrt(var + LN_EPS) * g + b


def _gelu_tanh(x):
    return 0.5 * x * (1.0 + jnp.tanh(0.7978845608028654 * (x + 0.044715 * (x * x * x))))


def _inproj_kernel(x_ref, w_ref, cos_ref, sin_ref, g_ref, b_ref,
                   q_ref, k_ref, v_ref, xp_ref, u_ref, vn_ref, *, d_attn, d_kv, d_pool, d_sg):
    xb = x_ref[...].astype(BF16)
    cos = cos_ref[...]
    sin = sin_ref[...]
    tm = xb.shape[0]
    lane = lax.broadcasted_iota(jnp.int32, (tm, LANES), 1)
    first_half = (lane % HEAD_DIM) < (HEAD_DIM // 2)

    def rope(z):
        rot = jnp.where(first_half, pltpu.roll(z, LANES - HEAD_DIM // 2, 1),
                        pltpu.roll(z, HEAD_DIM // 2, 1))
        return z * cos + rot * sin

    c0 = 0
    zq = jnp.dot(xb, w_ref[:, c0:c0 + d_attn], preferred_element_type=F32)
    for j in range(d_attn // LANES):
        sl = slice(j * LANES, (j + 1) * LANES)
        q_ref[:, sl] = (rope(zq[:, sl]) * (HEAD_DIM ** -0.5)).astype(BF16)
    c0 += d_attn
    zkv = jnp.dot(xb, w_ref[:, c0:c0 + 2 * d_kv], preferred_element_type=F32)
    k_ref[...] = rope(zkv[:, :d_kv])
    v_ref[...] = zkv[:, d_kv:]
    c0 += 2 * d_kv
    xp_ref[...] = jnp.dot(xb, w_ref[:, c0:c0 + d_pool], preferred_element_type=F32)
    c0 += d_pool
    u_ref[...] = _gelu_tanh(jnp.dot(xb, w_ref[:, c0:c0 + d_sg], preferred_element_type=F32))
    c0 += d_sg
    vg = _gelu_tanh(jnp.dot(xb, w_ref[:, c0:c0 + d_sg], preferred_element_type=F32))
    gd = d_sg // N_SG_GROUPS
    for g in range(N_SG_GROUPS):
        sl = slice(g * gd, (g + 1) * gd)
        vn_ref[:, sl] = _layer_norm(vg[:, sl], g_ref[:, sl], b_ref[:, sl])


def _inproj(x, w, cos, sin, sg_g, sg_b, *, layer, tm, d_attn, d_kv, d_pool, d_sg):
    r, d = x.shape
    d_in = w.shape[2]
    row = lambda i: (i, 0)
    const = lambda i: (0, 0)
    outs = (
        jax.ShapeDtypeStruct((r, d_attn), BF16),
        jax.ShapeDtypeStruct((r, d_kv), F32),
        jax.ShapeDtypeStruct((r, d_kv), F32),
        jax.ShapeDtypeStruct((r, d_pool), F32),
        jax.ShapeDtypeStruct((r, d_sg), F32),
        jax.ShapeDtypeStruct((r, d_sg), F32),
    )
    return pl.pallas_call(
        functools.partial(_inproj_kernel, d_attn=d_attn, d_kv=d_kv, d_pool=d_pool, d_sg=d_sg),
        out_shape=outs,
        grid=(r // tm,),
        in_specs=[
            pl.BlockSpec((tm, d), row),
            pl.BlockSpec((None, d, d_in), lambda i: (layer, 0, 0), pipeline_mode=pl.Buffered(1)),
            pl.BlockSpec((tm, LANES), row),
            pl.BlockSpec((tm, LANES), row),
            pl.BlockSpec((1, d_sg), const),
            pl.BlockSpec((1, d_sg), const),
        ],
        out_specs=(
            pl.BlockSpec((tm, d_attn), row),
            pl.BlockSpec((tm, d_kv), row),
            pl.BlockSpec((tm, d_kv), row),
            pl.BlockSpec((tm, d_pool), row),
            pl.BlockSpec((tm, d_sg), row),
            pl.BlockSpec((tm, d_sg), row),
        ),
        compiler_params=_cparams(1),
        name="inproj",
    )(x, w, cos, sin, sg_g, sg_b)


def _split_kv(kv):
    lane = lax.broadcasted_iota(jnp.int32, kv.shape, 1)
    low = lane < HEAD_DIM
    a = kv.astype(BF16)
    b = pltpu.roll(kv, HEAD_DIM, 1).astype(BF16)
    zero = jnp.zeros_like(a)
    head0 = (jnp.where(low, a, zero), jnp.where(low, zero, b))
    head1 = (jnp.where(low, b, zero), jnp.where(low, zero, a))
    return head0, head1


def _attend(q, k_pair, v_pair, allowed, sinks):
    out = None
    for kx, vx, sk in zip(k_pair, v_pair, sinks):
        s = lax.dot_general(q, kx, (((1,), (1,)), ((), ())), preferred_element_type=F32)
        s = jnp.where(allowed, s, NEG_INF)
        m = jnp.maximum(jnp.max(s, axis=1, keepdims=True), sk)
        p = jnp.exp(s - m)
        denom = jnp.sum(p, axis=1, keepdims=True) + jnp.exp(sk - m)
        o = jnp.dot(p.astype(BF16), vx, preferred_element_type=F32) * (1.0 / denom)
        out = o if out is None else out + o
    return out


def _window_sums(full):
    sums = {1: full}
    w = 1
    while w < max(POOL_WINDOWS):
        sums[2 * w] = sums[w] + pltpu.roll(sums[w], w, 0)
        w *= 2
    return sums


def _pool_mix(full, cur, cnt_of, pw_ref, ps_ref, row0):
    n = cur.shape[0]
    gd = cur.shape[1] // len(POOL_WINDOWS)
    outs = []
    for g, w in enumerate(POOL_WINDOWS):
        sl = slice(g * gd, (g + 1) * gd)
        sums = _window_sums(full[:, sl])[w]
        pooled = sums[row0:row0 + n, :] / cnt_of(w)
        d = (pooled - cur[:, sl]).astype(BF16)
        outs.append(jnp.dot(d, pw_ref[g], preferred_element_type=F32) * ps_ref[:, sl])
    return outs


def _causal_weights(sw_ref, g):
    n = sw_ref.shape[1]
    ri = lax.broadcasted_iota(jnp.int32, (n, n), 0)
    ci = lax.broadcasted_iota(jnp.int32, (n, n), 1)
    return jnp.where(ri >= ci, sw_ref[g], 0.0).astype(BF16)


def _prompt_mixers(i, sink_ref, q_ref, kc_ref, kp_ref, vc_ref, vp_ref, xc_ref, xt_ref, u_ref,
                   vn_ref, pw_ref, ps_ref, sw_ref, sb_ref, o_ref, *, d_attn, d_pool, d_sg):
    has_prev = i > 0

    kall = jnp.concatenate([kp_ref[...], kc_ref[...]], axis=0)
    vall = jnp.concatenate([vp_ref[...], vc_ref[...]], axis=0)
    k_heads = _split_kv(kall)
    v_heads = _split_kv(vall)
    ri = lax.broadcasted_iota(jnp.int32, (BLK, 2 * BLK), 0)
    cj = lax.broadcasted_iota(jnp.int32, (BLK, 2 * BLK), 1)
    first_key = jnp.where(has_prev, 0, BLK)
    allowed = (cj >= ri) & (cj <= ri + WINDOW) & (cj >= first_key)
    n_pairs = d_attn // LANES
    pairs_per_kv = n_pairs // N_KV_HEADS
    for p in range(n_pairs):
        g = p // pairs_per_kv
        sl = slice(p * LANES, (p + 1) * LANES)
        o = _attend(q_ref[:, sl], k_heads[g], v_heads[g], allowed,
                    (sink_ref[2 * p], sink_ref[2 * p + 1]))
        o_ref[:, sl] = o.astype(BF16)

    xc = xc_ref[...]
    tail = jnp.where(has_prev, xt_ref[...], 0.0)
    full = jnp.concatenate([tail, xc], axis=0)
    hist = tail.shape[0]
    pos = i * BLK + lax.broadcasted_iota(jnp.int32, (BLK, 1), 0)
    cnt_of = lambda w: jnp.minimum(pos + 1, w).astype(F32)
    pooled = _pool_mix(full, xc, cnt_of, pw_ref, ps_ref, hist)
    gd = d_pool // len(POOL_WINDOWS)
    for g, y in enumerate(pooled):
        o_ref[:, d_attn + g * gd:d_attn + (g + 1) * gd] = y.astype(BF16)

    gs = d_sg // N_SG_GROUPS
    for g in range(N_SG_GROUPS):
        sl = slice(g * gs, (g + 1) * gs)
        s = jnp.dot(_causal_weights(sw_ref, g), vn_ref[:, sl].astype(BF16),
                    preferred_element_type=F32) + sb_ref[:, sl]
        c0 = d_attn + d_pool + g * gs
        o_ref[:, c0:c0 + gs] = (u_ref[:, sl] * s).astype(BF16)


SEQ_PER_STEP = 2


def _mix_sample_kernel(sink_ref, q_ref, kn_ref, vn_new_ref, ck_ref, cv_ref, xn_ref, st_ref,
                       u_ref, vn_ref, pw_ref, ps_ref, sw_ref, sb_ref, o_ref, *, d_attn, d_pool, d_sg):
    sp = SAMPLE_PAD
    n_pairs = d_attn // LANES
    pairs_per_kv = n_pairs // N_KV_HEADS
    m_rows = pairs_per_kv * sp
    qf = q_ref[...].astype(F32)

    ri = lax.broadcasted_iota(jnp.int32, (m_rows, 2 * BLK), 0) % sp
    cj = lax.broadcasted_iota(jnp.int32, (m_rows, 2 * BLK), 1)
    allowed = (cj >= ri) & (cj <= ri + WINDOW)
    row_pair = lax.broadcasted_iota(jnp.int32, (m_rows, 1), 0) // sp

    attn_rows = []
    for s in range(SEQ_PER_STEP):
        rows = slice(s * sp, (s + 1) * sp)
        zpad = jnp.zeros((BLK - sp, kn_ref.shape[1]), F32)
        kall = jnp.concatenate([ck_ref[s * BLK:(s + 1) * BLK, :], kn_ref[rows, :], zpad], axis=0)
        vall = jnp.concatenate([cv_ref[s * BLK:(s + 1) * BLK, :], vn_new_ref[rows, :], zpad], axis=0)
        k_heads = _split_kv(kall)
        v_heads = _split_kv(vall)
        per_pair = []
        for g in range(N_KV_HEADS):
            qst = jnp.concatenate(
                [qf[rows, (g * pairs_per_kv + pl_) * LANES:(g * pairs_per_kv + pl_ + 1) * LANES]
                 for pl_ in range(pairs_per_kv)], axis=0).astype(BF16)
            sink_lo = jnp.zeros((m_rows, 1), F32)
            sink_hi = jnp.zeros((m_rows, 1), F32)
            for pl_ in range(pairs_per_kv):
                h = 2 * (g * pairs_per_kv + pl_)
                sink_lo = jnp.where(row_pair == pl_, sink_ref[h], sink_lo)
                sink_hi = jnp.where(row_pair == pl_, sink_ref[h + 1], sink_hi)
            o = _attend(qst, k_heads[g], v_heads[g], allowed, (sink_lo, sink_hi))
            per_pair.extend(o[pl_ * sp:(pl_ + 1) * sp, :] for pl_ in range(pairs_per_kv))
        attn_rows.append(per_pair)
    for p in range(n_pairs):
        o_ref[:, p * LANES:(p + 1) * LANES] = jnp.concatenate(
            [attn_rows[s][p] for s in range(SEQ_PER_STEP)], axis=0).astype(BF16)

    gd = d_pool // len(POOL_WINDOWS)
    pooled = []
    for s in range(SEQ_PER_STEP):
        rows = slice(s * sp, (s + 1) * sp)
        xc = xn_ref[rows, :]
        full = jnp.concatenate([st_ref[s], xc], axis=0)
        cnt_of = lambda w: float(w)
        pooled.append(_pool_mix(full, xc, cnt_of, pw_ref, ps_ref, st_ref.shape[1]))
    for g in range(len(POOL_WINDOWS)):
        o_ref[:, d_attn + g * gd:d_attn + (g + 1) * gd] = jnp.concatenate(
            [pooled[s][g] for s in range(SEQ_PER_STEP)], axis=0).astype(BF16)

    gs = d_sg // N_SG_GROUPS
    for g in range(N_SG_GROUPS):
        sl = slice(g * gs, (g + 1) * gs)
        wm = _causal_weights(sw_ref, g)
        outs = []
        for s in range(SEQ_PER_STEP):
            rows = slice(s * sp, (s + 1) * sp)
            vpad = jnp.concatenate([vn_ref[rows, sl], jnp.zeros((BLK - sp, gs), F32)], axis=0)
            sg = jnp.dot(wm, vpad.astype(BF16), preferred_element_type=F32)[:sp, :] + sb_ref[:sp, sl]
            outs.append(u_ref[rows, sl] * sg)
        c0 = d_attn + d_pool + g * gs
        o_ref[:, c0:c0 + gs] = jnp.concatenate(outs, axis=0).astype(BF16)


def _mix_sample(sinks, q, k, v, xp, u, vn, cache_k, cache_v, state, pool_w, pool_scale,
                sg_w, sg_bias, *, layer, r_prompt, n_seq):
    d_attn, d_kv, d_pool, d_sg = q.shape[1], k.shape[1], xp.shape[1], u.shape[1]
    d_mix = d_attn + d_pool + d_sg
    rows = SEQ_PER_STEP * SAMPLE_PAD
    base = r_prompt // rows
    n_steps = n_seq // SEQ_PER_STEP
    new = lambda i: (base + i, 0)
    per_seq = lambda i: (i, 0)
    carried = lambda i: (layer * n_steps + i, 0)
    const2 = lambda i: (0, 0)
    const3 = lambda i: (0, 0, 0)
    return pl.pallas_call(
        functools.partial(_mix_sample_kernel, d_attn=d_attn, d_pool=d_pool, d_sg=d_sg),
        out_shape=jax.ShapeDtypeStruct((n_seq * SAMPLE_PAD, d_mix), BF16),
        grid=(n_seq // SEQ_PER_STEP,),
        in_specs=[
            pl.BlockSpec(memory_space=pltpu.SMEM),
            pl.BlockSpec((rows, d_attn), new),
            pl.BlockSpec((rows, d_kv), new),
            pl.BlockSpec((rows, d_kv), new),
            pl.BlockSpec((SEQ_PER_STEP * BLK, d_kv), carried),
            pl.BlockSpec((SEQ_PER_STEP * BLK, d_kv), carried),
            pl.BlockSpec((rows, d_pool), new),
            pl.BlockSpec((SEQ_PER_STEP,) + state.shape[1:], lambda i: (layer * n_steps + i, 0, 0)),
            pl.BlockSpec((rows, d_sg), new),
            pl.BlockSpec((rows, d_sg), new),
            pl.BlockSpec(pool_w.shape, const3),
            pl.BlockSpec((1, d_pool), const2),
            pl.BlockSpec(sg_w.shape, const3),
            pl.BlockSpec((BLK, d_sg), const2),
        ],
        out_specs=pl.BlockSpec((rows, d_mix), per_seq),
        compiler_params=_cparams(1),
        name="mix_sample",
    )(sinks, q, k, v, cache_k, cache_v, xp, state, u, vn, pool_w, pool_scale, sg_w, sg_bias)


ROUTE_E1, ROUTE_E2, ROUTE_G1, ROUTE_G2, ROUTE_R1, ROUTE_R2 = range(6)

SLAB_PITCH = 24


def _lane_pick(rec, k):
    lane = lax.broadcasted_iota(jnp.int32, rec.shape, 1)
    return jnp.sum(jnp.where(lane == k, rec, 0.0), axis=1, keepdims=True)


def _store_slabs(slab_ref, x):
    n, d = x.shape
    slab_ref[...] = jnp.zeros_like(slab_ref)
    for j in range(d // LANES):
        slab_ref[pl.ds(j, n, stride=SLAB_PITCH), :] = x[:, j * LANES:(j + 1) * LANES]


def _load_slab_chunk(slab_ref, j, n):
    return slab_ref[pl.ds(j, n, stride=SLAB_PITCH), :]


def _route(x1, counted, rwh_ref, rwl_ref, rb_ref, route_ref, cnt_ref, carry_ref):
    hi = x1.astype(BF16)
    lo = (x1 - hi.astype(F32)).astype(BF16)
    logits = (jnp.dot(hi, rwh_ref[...], preferred_element_type=F32)
              + jnp.dot(lo, rwh_ref[...], preferred_element_type=F32)
              + jnp.dot(hi, rwl_ref[...], preferred_element_type=F32) + rb_ref[...])
    lane_i = lax.broadcasted_iota(jnp.int32, logits.shape, 1)
    lane = lane_i.astype(F32)
    logits = jnp.where(lane_i < N_EXPERTS, logits, NEG_INF)
    m1 = jnp.max(logits, axis=1, keepdims=True)
    i1 = jnp.min(jnp.where(logits == m1, lane, float(LANES)), axis=1, keepdims=True)
    first = lane == i1
    rest = jnp.where(first, NEG_INF, logits)
    m2 = jnp.max(rest, axis=1, keepdims=True)
    i2 = jnp.min(jnp.where(rest == m2, lane, float(LANES)), axis=1, keepdims=True)
    second = lane == i2
    e = jnp.exp(m2 - m1)
    g1 = 1.0 / (1.0 + e)
    g2 = e * g1

    sel = (jnp.where(first, 1.0, 0.0) + jnp.where(second, 1.0, 0.0)) * counted
    tm = sel.shape[0]
    ri = lax.broadcasted_iota(jnp.int32, (tm, tm), 0)
    ci = lax.broadcasted_iota(jnp.int32, (tm, tm), 1)
    earlier = jnp.where(ri > ci, 1.0, 0.0).astype(BF16)
    ranks = jnp.dot(earlier, sel.astype(BF16), preferred_element_type=F32) + carry_ref[...]
    r1 = jnp.sum(jnp.where(first, ranks, 0.0), axis=1, keepdims=True)
    r2 = jnp.sum(jnp.where(second, ranks, 0.0), axis=1, keepdims=True)
    carry_ref[...] += jnp.sum(sel, axis=0, keepdims=True)
    cnt_ref[...] = carry_ref[...]

    rec = jnp.zeros_like(logits)
    for k, val in ((ROUTE_E1, i1), (ROUTE_E2, i2), (ROUTE_G1, g1), (ROUTE_G2, g2),
                   (ROUTE_R1, r1), (ROUTE_R2, r2)):
        rec = jnp.where(lane_i == k, val, rec)
    route_ref[...] = rec


N_MIX_REFS = 15


def _mix_out_kernel(*refs, alpha, with_router, n_cast, n_prompt_blocks, blocks_per_seq,
                    d_attn, d_pool, d_sg):
    mix_refs = refs[:N_MIX_REFS - 1]
    ms_ref, x_ref, w_ref, g_ref, b_ref = refs[N_MIX_REFS - 1:N_MIX_REFS + 4]
    rest = list(refs[N_MIX_REFS + 4:])
    if with_router:
        rwh_ref, rwl_ref, rb_ref = rest[:3]
        rest = rest[3:]
    cast_src, rest = rest[:n_cast], rest[n_cast:]
    if with_router:
        x1_ref, x1s_ref, route_ref, cnt_ref = rest[:4]
        rest = rest[4:]
    else:
        x1_ref, x1b_ref = rest[:2]
        rest = rest[2:]
    cast_dst, rest = rest[:n_cast], rest[n_cast:]
    if with_router:
        prev_ref, next_ref, carry_ref = rest
    else:
        prev_ref, next_ref = rest
    s = pl.program_id(0)
    _run_casts(cast_src, cast_dst)

    @pl.when(s == 0)
    def _():
        prev_ref[...] = jnp.zeros_like(prev_ref)
        if with_router:
            carry_ref[...] = jnp.zeros_like(carry_ref)

    def project():
        y = alpha * x_ref[...] + jnp.dot(prev_ref[...], w_ref[...], preferred_element_type=F32)
        x1 = _layer_norm(y, g_ref[...], b_ref[...])
        x1_ref[...] = x1
        if with_router:
            _store_slabs(x1s_ref, x1)
            counted = jnp.where(s > 0, 1.0, 0.0)
            _route(x1, counted, rwh_ref, rwl_ref, rb_ref, route_ref, cnt_ref, carry_ref)
        else:
            x1b_ref[...] = x1.astype(BF16)

    @pl.when(s < n_prompt_blocks)
    def _():
        project()
        _prompt_mixers(lax.rem(s, blocks_per_seq), *mix_refs, next_ref,
                       d_attn=d_attn, d_pool=d_pool, d_sg=d_sg)

    @pl.when(s >= n_prompt_blocks)
    def _():
        project()
        next_ref[...] = ms_ref[...]

    prev_ref[...] = next_ref[...]


def _mix_out(sinks, q, k, v, xp, u, vn, pool_w, pool_scale, sg_w, sg_bias, mixed_s, x, w, g, b,
             router=None, cast_jobs=(), *, layer, n_prompt, seq, alpha):
    r, d = x.shape
    d_attn, d_kv, d_pool, d_sg = q.shape[1], k.shape[1], xp.shape[1], u.shape[1]
    d_mix = d_attn + d_pool + d_sg
    assert n_prompt % BLK == 0 and (r - n_prompt) % BLK == 0 and mixed_s.shape[0] == r - n_prompt
    n_p = n_prompt // BLK
    n_s = (r - n_prompt) // BLK
    hist = 2 * SUBLANES
    pblk = lambda s: jnp.minimum(s, n_p - 1)
    cur = lambda s: (pblk(s), 0)
    prev = lambda s: (jnp.maximum(pblk(s) - 1, 0), 0)
    tail = lambda s: (jnp.maximum(pblk(s) * (BLK // hist) - 1, 0), 0)
    samp = lambda s: (jnp.clip(s - n_p, 0, n_s - 1), 0)
    row = lambda s: (jnp.maximum(s - 1, 0), 0)
    const2 = lambda s: (0, 0)
    const3 = lambda s: (0, 0, 0)
    in_specs = [
        pl.BlockSpec(memory_space=pltpu.SMEM),
        pl.BlockSpec((BLK, d_attn), cur),
        pl.BlockSpec((BLK, d_kv), cur),
        pl.BlockSpec((BLK, d_kv), prev),
        pl.BlockSpec((BLK, d_kv), cur),
        pl.BlockSpec((BLK, d_kv), prev),
        pl.BlockSpec((BLK, d_pool), cur),
        pl.BlockSpec((hist, d_pool), tail),
        pl.BlockSpec((BLK, d_sg), cur),
        pl.BlockSpec((BLK, d_sg), cur),
        pl.BlockSpec(pool_w.shape, const3),
        pl.BlockSpec((1, d_pool), const2),
        pl.BlockSpec(sg_w.shape, const3),
        pl.BlockSpec((BLK, d_sg), const2),
        pl.BlockSpec((BLK, d_mix), samp),
        pl.BlockSpec((BLK, d), row),
        pl.BlockSpec((None,) + w.shape[1:], lambda s: (layer, 0, 0), pipeline_mode=pl.Buffered(1)),
        pl.BlockSpec((1, d), const2),
        pl.BlockSpec((1, d), const2),
    ]
    args = [sinks, q, k, k, v, v, xp, xp, u, vn, pool_w, pool_scale, sg_w, sg_bias, mixed_s, x, w, g, b]
    assert len(args) == N_MIX_REFS + 4
    scratch = [pltpu.VMEM((BLK, d_mix), BF16), pltpu.VMEM((BLK, d_mix), BF16)]
    if router is None:
        outs = [jax.ShapeDtypeStruct((r, d), F32), jax.ShapeDtypeStruct((r, d), BF16)]
        out_specs = [pl.BlockSpec((BLK, d), row), pl.BlockSpec((BLK, d), row)]
    else:
        in_specs += [pl.BlockSpec((d, LANES), const2), pl.BlockSpec((d, LANES), const2),
                     pl.BlockSpec((1, LANES), const2)]
        args += list(router)
        outs = [jax.ShapeDtypeStruct((r, d), F32),
                jax.ShapeDtypeStruct((r * SLAB_PITCH, LANES), F32),
                jax.ShapeDtypeStruct((r, LANES), F32),
                jax.ShapeDtypeStruct((1, LANES), F32)]
        out_specs = [pl.BlockSpec((BLK, d), row),
                     pl.BlockSpec((BLK * SLAB_PITCH, LANES), row),
                     pl.BlockSpec((BLK, LANES), row),
                     pl.BlockSpec((1, LANES), const2)]
        scratch.append(pltpu.VMEM((1, LANES), F32))
    n_steps = r // BLK + 1
    c_in, c_shape, c_out, c_args = _cast_operands(cast_jobs, n_steps, lambda s: s)
    n_main = len(outs)
    res = pl.pallas_call(
        functools.partial(_mix_out_kernel, alpha=alpha, with_router=router is not None,
                          n_cast=len(cast_jobs), n_prompt_blocks=n_p, blocks_per_seq=seq // BLK,
                          d_attn=d_attn, d_pool=d_pool, d_sg=d_sg),
        out_shape=tuple(outs + c_shape),
        grid=(n_steps,),
        in_specs=in_specs + c_in,
        out_specs=tuple(out_specs + c_out),
        scratch_shapes=scratch,
        compiler_params=_cparams(1),
        name="mix_out_router" if router is not None else "mix_out",
    )(*args, *c_args)
    return res[:n_main], res[n_main:]


def _swiglu(xb, wg_ref, wu_ref):
    a = jnp.dot(xb, wg_ref[...], preferred_element_type=F32)
    c = jnp.dot(xb, wu_ref[...], preferred_element_type=F32)
    return (a * (1.0 / (1.0 + jnp.exp(-a))) * c).astype(BF16)


def _ffn_kernel(*refs, alpha, n_cast, with_zeros):
    xb_ref, x1_ref, wg_ref, wu_ref, wd_ref, g_ref, b_ref = refs[:7]
    cast_src = refs[7:7 + n_cast]
    o_ref = refs[7 + n_cast]
    cast_dst = refs[8 + n_cast:8 + 2 * n_cast]
    f = pl.program_id(1)
    if with_zeros:
        zero_ref = refs[8 + 2 * n_cast]
        zero_ref[...] = jnp.zeros_like(zero_ref)

    @pl.when(f == 0)
    def _():
        o_ref[...] = alpha * x1_ref[...]

    o_ref[...] += jnp.dot(_swiglu(xb_ref[...], wg_ref, wu_ref), wd_ref[...],
                          preferred_element_type=F32)

    @pl.when(f == pl.num_programs(1) - 1)
    def _():
        o_ref[...] = _layer_norm(o_ref[...], g_ref[...], b_ref[...])

    _run_casts(cast_src, cast_dst)


def _run_casts(src_refs, dst_refs):
    for src, dst in zip(src_refs, dst_refs):
        dst[...] = src[...].astype(BF16)


def _chunk_dst_map(*ids, step_of, last):
    return (jnp.minimum(step_of(*ids), last), 0)


def _chunk_src_map(*ids, step_of, layer, last):
    return (layer,) + _chunk_dst_map(*ids, step_of=step_of, last=last)


def _chunk_rows(rows, steps):
    for t in range(2 * SUBLANES, rows + 1, 2 * SUBLANES):
        if rows % t == 0 and rows // t <= steps:
            return t
    return rows


def _cast_operands(jobs, steps, step_of):
    in_specs, out_shape, out_specs, args = [], [], [], []
    for arr, li in jobs:
        _, rows, cols = arr.shape
        cr = _chunk_rows(rows, steps)
        last = rows // cr - 1
        in_specs.append(pl.BlockSpec(
            (None, cr, cols), functools.partial(_chunk_src_map, step_of=step_of, layer=li, last=last)))
        out_shape.append(jax.ShapeDtypeStruct((rows, cols), BF16))
        out_specs.append(pl.BlockSpec(
            (cr, cols), functools.partial(_chunk_dst_map, step_of=step_of, last=last)))
        args.append(arr)
    return in_specs, out_shape, out_specs, args


def _ffn(xb, x1, wg, wu, wd, g, b, cast_jobs=(), zero_rows=0, *, layer, tm, tf, alpha):
    r, d = x1.shape
    ff = wg.shape[2]
    nf = ff // tf
    steps = (r // tm) * nf
    step_of = lambda i, f: i * nf + f
    row = lambda i, f: (i, 0)
    const = lambda i, f: (0, 0)
    in_specs = [
        pl.BlockSpec((tm, d), row),
        pl.BlockSpec((tm, d), row, pipeline_mode=pl.Buffered(1)),
        pl.BlockSpec((None, d, tf), lambda i, f: (layer, 0, f)),
        pl.BlockSpec((None, d, tf), lambda i, f: (layer, 0, f)),
        pl.BlockSpec((None, tf, d), lambda i, f: (layer, f, 0)),
        pl.BlockSpec((1, d), const),
        pl.BlockSpec((1, d), const),
    ]
    c_in, c_shape, c_out, c_args = _cast_operands(cast_jobs, steps, step_of)
    out_shape = [jax.ShapeDtypeStruct((r, d), F32)] + c_shape
    out_specs = [pl.BlockSpec((tm, d), row)] + c_out
    if zero_rows:
        zr = _chunk_rows(zero_rows, steps)
        out_shape.append(jax.ShapeDtypeStruct((zero_rows, LANES), F32))
        out_specs.append(pl.BlockSpec(
            (zr, LANES), functools.partial(_chunk_dst_map, step_of=step_of, last=zero_rows // zr - 1)))
    outs = pl.pallas_call(
        functools.partial(_ffn_kernel, alpha=alpha, n_cast=len(cast_jobs), with_zeros=bool(zero_rows)),
        out_shape=tuple(out_shape),
        grid=(r // tm, nf),
        in_specs=in_specs + c_in,
        out_specs=tuple(out_specs),
        compiler_params=_cparams(2),
        name="ffn",
    )(xb, x1, wg, wu, wd, g, b, *c_args)
    n = len(cast_jobs)
    return outs[0], outs[1:1 + n], (outs[1 + n] if zero_rows else None)


MOE_BLOCK = 512
DMA_RING = 128


def _ring_depth(n):
    return 1 << (min(DMA_RING, n).bit_length() - 1)


def _routing_tables(route, counts, *, n_work):
    e1 = route[:, ROUTE_E1].astype(jnp.int32)
    e2 = route[:, ROUTE_E2].astype(jnp.int32)
    r1 = route[:, ROUTE_R1].astype(jnp.int32)
    r2 = route[:, ROUTE_R2].astype(jnp.int32)
    cnt = counts[0, :N_EXPERTS].astype(jnp.int32)
    nblk = (cnt + MOE_BLOCK - 1) // MOE_BLOCK
    blk_end = jnp.cumsum(nblk)
    start = (blk_end - nblk) * MOE_BLOCK
    slot1 = jnp.take(start, e1) + r1
    slot2 = jnp.take(start, e2) + r2
    n_active = blk_end[-1]
    blk = jnp.minimum(jnp.arange(n_work, dtype=jnp.int32), n_active - 1)
    work_exp = jnp.sum(blk[:, None] >= blk_end[None, :], axis=1).astype(jnp.int32)
    return slot1, slot2, work_exp, n_active.reshape(1)


def _ring_copies(n, ring, make_copies):
    def start(i):
        for c in make_copies(i):
            c.start()

    def wait(i):
        for c in make_copies(i):
            c.wait()

    def fill(i, carry):
        start(i)
        return carry

    def steady(i, carry):
        wait(i - ring)
        start(i)
        return carry

    def drain(i, carry):
        wait(i)
        return carry

    lax.fori_loop(0, ring, fill, 0)
    lax.fori_loop(ring, n, steady, 0)
    lax.fori_loop(n - ring, n, drain, 0)


SLAB_ROWS = 16


def _slab(ref, token):
    return ref.at[pl.ds(pl.multiple_of(token * SLAB_PITCH, SUBLANES), SLAB_ROWS), :]


def _dispatch_kernel(s1_ref, s2_ref, x1s_ref, xs_init, xs_hbm, sem):
    del xs_init
    tm = x1s_ref.shape[0] // SLAB_PITCH
    base = pl.program_id(0) * tm
    ring = sem.shape[1]

    def copies(r):
        k = r & (ring - 1)
        src = _slab(x1s_ref, r)
        return (pltpu.make_async_copy(src, _slab(xs_hbm, s1_ref[base + r]), sem.at[0, k]),
                pltpu.make_async_copy(src, _slab(xs_hbm, s2_ref[base + r]), sem.at[1, k]))

    _ring_copies(tm, ring, copies)


def _moe_dispatch(slot1, slot2, x1s, xs_init, *, tm):
    r = x1s.shape[0] // SLAB_PITCH
    return pl.pallas_call(
        _dispatch_kernel,
        out_shape=jax.ShapeDtypeStruct(xs_init.shape, F32),
        grid_spec=pltpu.PrefetchScalarGridSpec(
            num_scalar_prefetch=2,
            grid=(r // tm,),
            in_specs=[pl.BlockSpec((tm * SLAB_PITCH, LANES), lambda i, s1, s2: (i, 0)),
                      pl.BlockSpec(memory_space=pl.ANY)],
            out_specs=pl.BlockSpec(memory_space=pl.ANY),
            scratch_shapes=[pltpu.SemaphoreType.DMA((2, _ring_depth(tm)))],
        ),
        input_output_aliases={3: 0},
        compiler_params=_cparams(1),
        name="moe_dispatch",
    )(slot1, slot2, x1s, xs_init)


def _moe_ffn_kernel(we_ref, na_ref, xs_ref, wg_ref, wu_ref, wd_ref, ys_ref, xb_ref, acc_ref):
    del we_ref
    w = pl.program_id(0)
    f = pl.program_id(1)
    n, d = acc_ref.shape

    @pl.when(f == 0)
    def _():
        for j in range(d // LANES):
            xb_ref[:, j * LANES:(j + 1) * LANES] = _load_slab_chunk(xs_ref, j, n).astype(BF16)
        acc_ref[...] = jnp.zeros_like(acc_ref)

    @pl.when(w < na_ref[0])
    def _():
        acc_ref[...] += jnp.dot(_swiglu(xb_ref[...], wg_ref, wu_ref), wd_ref[...],
                                preferred_element_type=F32)

    @pl.when(f == pl.num_programs(1) - 1)
    def _():
        _store_slabs(ys_ref, acc_ref[...])


def _moe_ffn(work_exp, n_active, xs, wg, wu, wd, *, layer, tf):
    d = wg.shape[2]
    ff = wg.shape[3]
    nf = ff // tf
    rows = lambda w, f, we, na: (w, 0)
    chunk = lambda w, f, na: jnp.where(w < na[0], f, nf - 1)
    return pl.pallas_call(
        _moe_ffn_kernel,
        out_shape=jax.ShapeDtypeStruct(xs.shape, F32),
        grid_spec=pltpu.PrefetchScalarGridSpec(
            num_scalar_prefetch=2,
            grid=(xs.shape[0] // (MOE_BLOCK * SLAB_PITCH), nf),
            in_specs=[
                pl.BlockSpec((MOE_BLOCK * SLAB_PITCH, LANES), rows),
                pl.BlockSpec((None, None, d, tf),
                             lambda w, f, we, na: (layer, we[w], 0, chunk(w, f, na))),
                pl.BlockSpec((None, None, d, tf),
                             lambda w, f, we, na: (layer, we[w], 0, chunk(w, f, na))),
                pl.BlockSpec((None, None, tf, d),
                             lambda w, f, we, na: (layer, we[w], chunk(w, f, na), 0)),
            ],
            out_specs=pl.BlockSpec((MOE_BLOCK * SLAB_PITCH, LANES), rows),
            scratch_shapes=[pltpu.VMEM((MOE_BLOCK, d), BF16), pltpu.VMEM((MOE_BLOCK, d), F32)],
        ),
        compiler_params=_cparams(2),
        name="moe_ffn",
    )(work_exp, n_active, xs, wg, wu, wd)


def _combine_kernel(s1_ref, s2_ref, ys_hbm, x1_ref, route_ref, g_ref, b_ref, o_ref,
                    y1_ref, y2_ref, sem, *, alpha):
    tm, d = o_ref.shape
    base = pl.program_id(0) * tm
    ring = sem.shape[1]

    def copies(r):
        k = r & (ring - 1)
        return (pltpu.make_async_copy(_slab(ys_hbm, s1_ref[base + r]), _slab(y1_ref, r), sem.at[0, k]),
                pltpu.make_async_copy(_slab(ys_hbm, s2_ref[base + r]), _slab(y2_ref, r), sem.at[1, k]))

    _ring_copies(tm, ring, copies)
    route = route_ref[...]
    g1 = _lane_pick(route, ROUTE_G1)
    g2 = _lane_pick(route, ROUTE_G2)
    for j in range(d // LANES):
        sl = slice(j * LANES, (j + 1) * LANES)
        o_ref[:, sl] = (alpha * x1_ref[:, sl] + g1 * _load_slab_chunk(y1_ref, j, tm)
                        + g2 * _load_slab_chunk(y2_ref, j, tm))
    o_ref[...] = _layer_norm(o_ref[...], g_ref[...], b_ref[...])


def _moe_combine(slot1, slot2, ys, x1, route, g, b, *, tm, alpha):
    r, d = x1.shape
    row = lambda i, s1, s2: (i, 0)
    const = lambda i, s1, s2: (0, 0)
    return pl.pallas_call(
        functools.partial(_combine_kernel, alpha=alpha),
        out_shape=jax.ShapeDtypeStruct((r, d), F32),
        grid_spec=pltpu.PrefetchScalarGridSpec(
            num_scalar_prefetch=2,
            grid=(r // tm,),
            in_specs=[
                pl.BlockSpec(memory_space=pl.ANY),
                pl.BlockSpec((tm, d), row),
                pl.BlockSpec((tm, LANES), row),
                pl.BlockSpec((1, d), const),
                pl.BlockSpec((1, d), const),
            ],
            out_specs=pl.BlockSpec((tm, d), row),
            scratch_shapes=[pltpu.VMEM((tm * SLAB_PITCH, LANES), F32),
                            pltpu.VMEM((tm * SLAB_PITCH, LANES), F32),
                            pltpu.SemaphoreType.DMA((2, _ring_depth(tm)))],
        ),
        compiler_params=_cparams(1),
        name="moe_combine",
    )(slot1, slot2, ys, x1, route, g, b)


def _moe_work_items(r):
    return (2 * r + MOE_BLOCK - 1) // MOE_BLOCK + N_EXPERTS


def _moe(x1, x1s, route, counts, wg, wu, wd, g, b, xs_buf=None, *, layer, tm, tf, alpha):
    r = x1.shape[0]
    n_work = _moe_work_items(r)
    slot1, slot2, work_exp, n_active = _routing_tables(route, counts, n_work=n_work)
    if xs_buf is None:
        xs_buf = jnp.zeros((n_work * MOE_BLOCK * SLAB_PITCH, LANES), F32)
    xs = _moe_dispatch(slot1, slot2, x1s, xs_buf, tm=tm)
    ys = _moe_ffn(work_exp, n_active, xs, wg, wu, wd, layer=layer, tf=tf)
    return _moe_combine(slot1, slot2, ys, x1, route, g, b, tm=tm, alpha=alpha), xs


def _rope_tables(pos):
    half = HEAD_DIM // 2
    inv = ROPE_THETA ** (-jnp.arange(half, dtype=F32) / half)
    ang = pos.astype(F32)[:, None] * inv[None, :]
    cos = jnp.cos(ang)
    sin = jnp.sin(ang)
    cos2 = jnp.concatenate([cos, cos, cos, cos], axis=1)
    sin2 = jnp.concatenate([-sin, sin, -sin, sin], axis=1)
    return cos2, sin2


def _row_tile(r, cap):
    best = 16
    for t in range(16, cap + 1, 16):
        if r % t == 0:
            best = t
    return best


def kernel(x_prompt, x_sample, cache_k, cache_v, state_pool, w_in, w_out, attn_sinks, pool_w,
           pool_scale, sg_w, sg_b, sg_norm_g, sg_norm_b, ln1_g, ln1_b, ln2_g, ln2_b,
           ffn_w_gate, ffn_w_up, ffn_w_down, router_w, router_b, moe_w_gate, moe_w_up, moe_w_down):
    bp, tp, d = x_prompt.shape
    bs, ts, _ = x_sample.shape
    depth = w_in.shape[0]
    d_kv = cache_k.shape[3] * cache_k.shape[4]
    d_pool = pool_scale.shape[1]
    d_sg = sg_norm_g.shape[1]
    d_attn = w_in.shape[2] - 2 * d_kv - d_pool - 2 * d_sg
    win_buf = cache_k.shape[2]
    assert win_buf == WINDOW == BLK and tp % BLK == 0 and tp >= BLK and ts <= SAMPLE_PAD
    assert bs % SEQ_PER_STEP == 0 and state_pool.shape[2] == POOL_STATE
    assert cache_k.shape[3] == N_KV_HEADS and cache_k.shape[4] == HEAD_DIM
    assert d == SLAB_ROWS * LANES
    alpha = (2 * depth) ** 0.25

    r_p = bp * tp
    r_s = bs * SAMPLE_PAD
    r = r_p + r_s
    tm = _row_tile(r, 768)
    tm_ffn = _row_tile(r, 704)
    tm_comb = _row_tile(r, 384)

    xs_pad = jnp.pad(x_sample, ((0, 0), (0, SAMPLE_PAD - ts), (0, 0)))
    x = jnp.concatenate([x_prompt.reshape(r_p, d), xs_pad.reshape(r_s, d)], axis=0)

    pos = jnp.concatenate([
        jnp.tile(jnp.arange(tp, dtype=jnp.int32), bp),
        jnp.tile(PAST_LEN + jnp.arange(SAMPLE_PAD, dtype=jnp.int32), bs)])
    cos, sin = _rope_tables(pos)

    in_w = {0: w_in[0].astype(BF16)}
    out_w = {0: w_out[0].astype(BF16)}
    pool_w_b = pool_w.astype(BF16)
    ffn_bf16 = {}
    moe_bf16 = {}
    rw_pad = jnp.pad(router_w, ((0, 0), (0, 0), (0, LANES - router_w.shape[2])))
    rw_hi = rw_pad.astype(BF16)
    rw_lo = (rw_pad - rw_hi.astype(F32)).astype(BF16)
    rb_pad = jnp.pad(router_b, ((0, 0), (0, LANES - router_b.shape[1])))[:, None, :]
    sg_bias = jnp.repeat(jnp.swapaxes(sg_b, 1, 2), d_sg // N_SG_GROUPS, axis=2)
    state_pad = jnp.pad(state_pool, ((0, 0), (0, 0), (1, 0), (0, 0)))
    state_rows = state_pad.reshape((depth * bs,) + state_pad.shape[2:])
    ck_rows = cache_k.reshape(depth * bs * win_buf, d_kv)
    cv_rows = cache_v.reshape(depth * bs * win_buf, d_kv)

    k_l, v_l, xp_l, vn_l = [], [], [], []
    xs_buf = None
    for l in range(depth):
        q, k, v, xp, u, vn = _inproj(
            x, in_w[l][None], cos, sin, sg_norm_g[l][None], sg_norm_b[l][None],
            layer=0, tm=tm, d_attn=d_attn, d_kv=d_kv, d_pool=d_pool, d_sg=d_sg)
        mixed_s = _mix_sample(attn_sinks[l], q, k, v, xp, u, vn, ck_rows, cv_rows, state_rows,
                              pool_w_b[l], pool_scale[l][None], sg_w[l], sg_bias[l],
                              layer=l, r_prompt=r_p, n_seq=bs)
        i = l // 2
        is_dense = l % 2 == 0
        jobs = []
        if l + 1 < depth:
            jobs += [(w_in, l + 1), (w_out, l + 1)]
        if is_dense:
            jobs += [(ffn_w_gate, i), (ffn_w_up, i), (ffn_w_down, i)]
        main, casted = _mix_out(
            attn_sinks[l], q, k, v, xp, u, vn, pool_w_b[l], pool_scale[l][None], sg_w[l],
            sg_bias[l], mixed_s, x, out_w[l][None], ln1_g[l][None], ln1_b[l][None],
            None if is_dense else (rw_hi[i], rw_lo[i], rb_pad[i]), tuple(jobs),
            layer=0, n_prompt=r_p, seq=tp, alpha=alpha)
        casted = list(casted)
        if l + 1 < depth:
            in_w[l + 1], out_w[l + 1] = casted[:2]
            casted = casted[2:]
        if is_dense:
            x1, x1b = main
            ffn_g, ffn_u, ffn_d = (c[None] for c in casted)
            jobs = ()
            zero_rows = 0
            if l + 1 < depth:
                jobs = tuple((w.reshape(w.shape[0], -1, w.shape[-1]), (l + 1) // 2)
                             for w in (moe_w_gate, moe_w_up, moe_w_down))
                if xs_buf is None:
                    zero_rows = _moe_work_items(r) * MOE_BLOCK * SLAB_PITCH
            x, casted, zeros = _ffn(x1b, x1, ffn_g, ffn_u, ffn_d, ln2_g[l][None], ln2_b[l][None],
                                    jobs, zero_rows, layer=0, tm=tm_ffn, tf=512, alpha=alpha)
            if zero_rows:
                xs_buf = zeros
            if jobs:
                moe_bf16[(l + 1) // 2] = tuple(
                    c.reshape((1,) + w.shape[1:])
                    for c, w in zip(casted, (moe_w_gate, moe_w_up, moe_w_down)))
        else:
            x1, x1s, route, counts = main
            if i not in moe_bf16:
                moe_bf16[i] = tuple(w[i][None].astype(BF16)
                                    for w in (moe_w_gate, moe_w_up, moe_w_down))
            moe_g, moe_u, moe_d = moe_bf16[i]
            x, xs_buf = _moe(x1, x1s, route, counts, moe_g, moe_u, moe_d, ln2_g[l][None],
                             ln2_b[l][None], xs_buf, layer=0, tm=tm_comb, tf=512, alpha=alpha)

        tail = max(WINDOW, POOL_STATE)
        keep = lambda a: jnp.concatenate(
            [a[:r_p].reshape(bp, tp, -1)[:, -tail:].reshape(bp * tail, -1), a[r_p:]], axis=0)
        k_l.append(keep(k))
        v_l.append(keep(v))
        xp_l.append(keep(xp))
        vn_l.append(vn[r_p:])

    kh = (N_KV_HEADS, HEAD_DIM)
    n_tail = bp * max(WINDOW, POOL_STATE)
    ks, vs, xps, vns = (jnp.stack(a) for a in (k_l, v_l, xp_l, vn_l))
    prompt_tail = lambda a, n: a[:, :n_tail].reshape(depth, bp, -1, a.shape[-1])[:, :, -n:]
    sample_new = lambda a: a[:, -r_s:].reshape(depth, bs, SAMPLE_PAD, a.shape[-1])[:, :, :ts]
    new_k_p = prompt_tail(ks, WINDOW).reshape(depth, bp, WINDOW, *kh)
    new_v_p = prompt_tail(vs, WINDOW).reshape(depth, bp, WINDOW, *kh)
    new_pool_p = prompt_tail(xps, POOL_STATE)
    new_k_s = jnp.concatenate(
        [cache_k, sample_new(ks).reshape(depth, bs, ts, *kh)], axis=2)[:, :, -win_buf:]
    new_v_s = jnp.concatenate(
        [cache_v, sample_new(vs).reshape(depth, bs, ts, *kh)], axis=2)[:, :, -win_buf:]
    new_pool_s = jnp.concatenate([state_pool, sample_new(xps)], axis=2)[:, :, -POOL_STATE:]
    new_sg_s = vns.reshape(depth, bs, SAMPLE_PAD, d_sg)[:, :, :ts]

    y_prompt = x[:r_p].reshape(bp, tp, d)
    y_sample = x[r_p:].reshape(bs, SAMPLE_PAD, d)[:, :ts]
    return (y_prompt, y_sample, new_k_p, new_v_p, new_pool_p, new_k_s, new_v_s, new_pool_s, new_sg_s)
```

```python
import functools
import math

import jax
import jax.numpy as jnp
from jax import lax
from jax.experimental import pallas as pl
from jax.experimental.pallas import tpu as pltpu

F32 = jnp.float32
BF16 = jnp.bfloat16

PAST_LEN = 16384
HEAD_DIM = 64
N_KV_HEADS = 2
WINDOW = 128
BLK = 128
ROPE_THETA = 10000.0
POOL_WINDOWS = (2, 4, 8, 16)
POOL_STATE = 15
N_SG_GROUPS = 4
N_EXPERTS = 8
LN_EPS = 1e-5

LANES = 128
SUBLANES = 8
SAMPLE_PAD = SUBLANES
VMEM_LIMIT = 56 * 1024 * 1024

NEG_INF = float("-inf")


def _cparams(n_axes):
    return pltpu.CompilerParams(
        dimension_semantics=("arbitrary",) * n_axes, vmem_limit_bytes=VMEM_LIMIT)


def _layer_norm(x, g, b):
    mu = jnp.mean(x, axis=-1, keepdims=True)
    xc = x - mu
    var = jnp.mean(xc * xc, axis=-1, keepdims=True)
    return xc * lax.rsqrt(var + LN_EPS) * g + b


def _gelu_tanh(x):
    return 0.5 * x * (1.0 + jnp.tanh(0.7978845608028654 * (x + 0.044715 * (x * x * x))))


def _inproj_kernel(x_ref, w_ref, cos_ref, sin_ref, g_ref, b_ref,
                   q_ref, k_ref, v_ref, xp_ref, u_ref, vn_ref, *, d_attn, d_kv, d_pool, d_sg):
    xb = x_ref[...].astype(BF16)
    cos = cos_ref[...]
    sin = sin_ref[...]
    tm = xb.shape[0]
    lane = lax.broadcasted_iota(jnp.int32, (tm, LANES), 1)
    first_half = (lane % HEAD_DIM) < (HEAD_DIM // 2)

    def rope(z):
        rot = jnp.where(first_half, pltpu.roll(z, LANES - HEAD_DIM // 2, 1),
                        pltpu.roll(z, HEAD_DIM // 2, 1))
        return z * cos + rot * sin

    c0 = 0
    zq = jnp.dot(xb, w_ref[:, c0:c0 + d_attn], preferred_element_type=F32)
    for j in range(d_attn // LANES):
        sl = slice(j * LANES, (j + 1) * LANES)
        q_ref[:, sl] = (rope(zq[:, sl]) * (HEAD_DIM ** -0.5)).astype(BF16)
    c0 += d_attn
    zkv = jnp.dot(xb, w_ref[:, c0:c0 + 2 * d_kv], preferred_element_type=F32)
    k_ref[...] = rope(zkv[:, :d_kv])
    v_ref[...] = zkv[:, d_kv:]
    c0 += 2 * d_kv
    xp_ref[...] = jnp.dot(xb, w_ref[:, c0:c0 + d_pool], preferred_element_type=F32)
    c0 += d_pool
    u_ref[...] = _gelu_tanh(jnp.dot(xb, w_ref[:, c0:c0 + d_sg], preferred_element_type=F32))
    c0 += d_sg
    vg = _gelu_tanh(jnp.dot(xb, w_ref[:, c0:c0 + d_sg], preferred_element_type=F32))
    gd = d_sg // N_SG_GROUPS
    for g in range(N_SG_GROUPS):
        sl = slice(g * gd, (g + 1) * gd)
        vn_ref[:, sl] = _layer_norm(vg[:, sl], g_ref[:, sl], b_ref[:, sl])


def _inproj(x, w, cos, sin, sg_g, sg_b, *, layer, tm, d_attn, d_kv, d_pool, d_sg):
    r, d = x.shape
    d_in = w.shape[2]
    row = lambda i: (i, 0)
    const = lambda i: (0, 0)
    outs = (
        jax.ShapeDtypeStruct((r, d_attn), BF16),
        jax.ShapeDtypeStruct((r, d_kv), F32),
        jax.ShapeDtypeStruct((r, d_kv), F32),
        jax.ShapeDtypeStruct((r, d_pool), F32),
        jax.ShapeDtypeStruct((r, d_sg), F32),
        jax.ShapeDtypeStruct((r, d_sg), F32),
    )
    return pl.pallas_call(
        functools.partial(_inproj_kernel, d_attn=d_attn, d_kv=d_kv, d_pool=d_pool, d_sg=d_sg),
        out_shape=outs,
        grid=(r // tm,),
        in_specs=[
            pl.BlockSpec((tm, d), row),
            pl.BlockSpec((None, d, d_in), lambda i: (layer, 0, 0), pipeline_mode=pl.Buffered(1)),
            pl.BlockSpec((tm, LANES), row),
            pl.BlockSpec((tm, LANES), row),
            pl.BlockSpec((1, d_sg), const),
            pl.BlockSpec((1, d_sg), const),
        ],
        out_specs=(
            pl.BlockSpec((tm, d_attn), row),
            pl.BlockSpec((tm, d_kv), row),
            pl.BlockSpec((tm, d_kv), row),
            pl.BlockSpec((tm, d_pool), row),
            pl.BlockSpec((tm, d_sg), row),
            pl.BlockSpec((tm, d_sg), row),
        ),
        compiler_params=_cparams(1),
        name="inproj",
    )(x, w, cos, sin, sg_g, sg_b)


def _split_kv(kv):
    lane = lax.broadcasted_iota(jnp.int32, kv.shape, 1)
    low = lane < HEAD_DIM
    a = kv.astype(BF16)
    b = pltpu.roll(kv, HEAD_DIM, 1).astype(BF16)
    zero = jnp.zeros_like(a)
    head0 = (jnp.where(low, a, zero), jnp.where(low, zero, b))
    head1 = (jnp.where(low, b, zero), jnp.where(low, zero, a))
    return head0, head1


def _attend(q, k_pair, v_pair, allowed, sinks):
    out = None
    for kx, vx, sk in zip(k_pair, v_pair, sinks):
        s = lax.dot_general(q, kx, (((1,), (1,)), ((), ())), preferred_element_type=F32)
        s = jnp.where(allowed, s, NEG_INF)
        m = jnp.maximum(jnp.max(s, axis=1, keepdims=True), sk)
        p = jnp.exp(s - m)
        denom = jnp.sum(p, axis=1, keepdims=True) + jnp.exp(sk - m)
        o = jnp.dot(p.astype(BF16), vx, preferred_element_type=F32) * (1.0 / denom)
        out = o if out is None else out + o
    return out


def _window_sums(full):
    sums = {1: full}
    w = 1
    while w < max(POOL_WINDOWS):
        sums[2 * w] = sums[w] + pltpu.roll(sums[w], w, 0)
        w *= 2
    return sums


def _pool_mix(full, cur, cnt_of, pw_ref, ps_ref, row0):
    n = cur.shape[0]
    gd = cur.shape[1] // len(POOL_WINDOWS)
    outs = []
    for g, w in enumerate(POOL_WINDOWS):
        sl = slice(g * gd, (g + 1) * gd)
        sums = _window_sums(full[:, sl])[w]
        pooled = sums[row0:row0 + n, :] / cnt_of(w)
        d = (pooled - cur[:, sl]).astype(BF16)
        outs.append(jnp.dot(d, pw_ref[g], preferred_element_type=F32) * ps_ref[:, sl])
    return outs


def _causal_weights(sw_ref, g):
    n = sw_ref.shape[1]
    ri = lax.broadcasted_iota(jnp.int32, (n, n), 0)
    ci = lax.broadcasted_iota(jnp.int32, (n, n), 1)
    return jnp.where(ri >= ci, sw_ref[g], 0.0).astype(BF16)


def _prompt_mixers(i, sink_ref, q_ref, kc_ref, kp_ref, vc_ref, vp_ref, xc_ref, xt_ref, u_ref,
                   vn_ref, pw_ref, ps_ref, sw_ref, sb_ref, o_ref, *, d_attn, d_pool, d_sg):
    has_prev = i > 0

    kall = jnp.concatenate([kp_ref[...], kc_ref[...]], axis=0)
    vall = jnp.concatenate([vp_ref[...], vc_ref[...]], axis=0)
    k_heads = _split_kv(kall)
    v_heads = _split_kv(vall)
    ri = lax.broadcasted_iota(jnp.int32, (BLK, 2 * BLK), 0)
    cj = lax.broadcasted_iota(jnp.int32, (BLK, 2 * BLK), 1)
    first_key = jnp.where(has_prev, 0, BLK)
    allowed = (cj >= ri) & (cj <= ri + WINDOW) & (cj >= first_key)
    n_pairs = d_attn // LANES
    pairs_per_kv = n_pairs // N_KV_HEADS
    for p in range(n_pairs):
        g = p // pairs_per_kv
        sl = slice(p * LANES, (p + 1) * LANES)
        o = _attend(q_ref[:, sl], k_heads[g], v_heads[g], allowed,
                    (sink_ref[2 * p], sink_ref[2 * p + 1]))
        o_ref[:, sl] = o.astype(BF16)

    xc = xc_ref[...]
    tail = jnp.where(has_prev, xt_ref[...], 0.0)
    full = jnp.concatenate([tail, xc], axis=0)
    hist = tail.shape[0]
    pos = i * BLK + lax.broadcasted_iota(jnp.int32, (BLK, 1), 0)
    cnt_of = lambda w: jnp.minimum(pos + 1, w).astype(F32)
    pooled = _pool_mix(full, xc, cnt_of, pw_ref, ps_ref, hist)
    gd = d_pool // len(POOL_WINDOWS)
    for g, y in enumerate(pooled):
        o_ref[:, d_attn + g * gd:d_attn + (g + 1) * gd] = y.astype(BF16)

    gs = d_sg // N_SG_GROUPS
    for g in range(N_SG_GROUPS):
        sl = slice(g * gs, (g + 1) * gs)
        s = jnp.dot(_causal_weights(sw_ref, g), vn_ref[:, sl].astype(BF16),
                    preferred_element_type=F32) + sb_ref[:, sl]
        c0 = d_attn + d_pool + g * gs
        o_ref[:, c0:c0 + gs] = (u_ref[:, sl] * s).astype(BF16)


SEQ_PER_STEP = 2


def _mix_sample_kernel(sink_ref, q_ref, kn_ref, vn_new_ref, ck_ref, cv_ref, xn_ref, st_ref,
                       u_ref, vn_ref, pw_ref, ps_ref, sw_ref, sb_ref, o_ref, *, d_attn, d_pool, d_sg):
    sp = SAMPLE_PAD
    n_pairs = d_attn // LANES
    pairs_per_kv = n_pairs // N_KV_HEADS
    m_rows = pairs_per_kv * sp
    qf = q_ref[...].astype(F32)

    ri = lax.broadcasted_iota(jnp.int32, (m_rows, 2 * BLK), 0) % sp
    cj = lax.broadcasted_iota(jnp.int32, (m_rows, 2 * BLK), 1)
    allowed = (cj >= ri) & (cj <= ri + WINDOW)
    row_pair = lax.broadcasted_iota(jnp.int32, (m_rows, 1), 0) // sp

    attn_rows = []
    for s in range(SEQ_PER_STEP):
        rows = slice(s * sp, (s + 1) * sp)
        zpad = jnp.zeros((BLK - sp, kn_ref.shape[1]), F32)
        kall = jnp.concatenate([ck_ref[s * BLK:(s + 1) * BLK, :], kn_ref[rows, :], zpad], axis=0)
        vall = jnp.concatenate([cv_ref[s * BLK:(s + 1) * BLK, :], vn_new_ref[rows, :], zpad], axis=0)
        k_heads = _split_kv(kall)
        v_heads = _split_kv(vall)
        per_pair = []
        for g in range(N_KV_HEADS):
            qst = jnp.concatenate(
                [qf[rows, (g * pairs_per_kv + pl_) * LANES:(g * pairs_per_kv + pl_ + 1) * LANES]
                 for pl_ in range(pairs_per_kv)], axis=0).astype(BF16)
            sink_lo = jnp.zeros((m_rows, 1), F32)
            sink_hi = jnp.zeros((m_rows, 1), F32)
            for pl_ in range(pairs_per_kv):
                h = 2 * (g * pairs_per_kv + pl_)
                sink_lo = jnp.where(row_pair == pl_, sink_ref[h], sink_lo)
                sink_hi = jnp.where(row_pair == pl_, sink_ref[h + 1], sink_hi)
            o = _attend(qst, k_heads[g], v_heads[g], allowed, (sink_lo, sink_hi))
            per_pair.extend(o[pl_ * sp:(pl_ + 1) * sp, :] for pl_ in range(pairs_per_kv))
        attn_rows.append(per_pair)
    for p in range(n_pairs):
        o_ref[:, p * LANES:(p + 1) * LANES] = jnp.concatenate(
            [attn_rows[s][p] for s in range(SEQ_PER_STEP)], axis=0).astype(BF16)

    gd = d_pool // len(POOL_WINDOWS)
    pooled = []
    for s in range(SEQ_PER_STEP):
        rows = slice(s * sp, (s + 1) * sp)
        xc = xn_ref[rows, :]
        full = jnp.concatenate([st_ref[s], xc], axis=0)
        cnt_of = lambda w: float(w)
        pooled.append(_pool_mix(full, xc, cnt_of, pw_ref, ps_ref, st_ref.shape[1]))
    for g in range(len(POOL_WINDOWS)):
        o_ref[:, d_attn + g * gd:d_attn + (g + 1) * gd] = jnp.concatenate(
            [pooled[s][g] for s in range(SEQ_PER_STEP)], axis=0).astype(BF16)

    gs = d_sg // N_SG_GROUPS
    for g in range(N_SG_GROUPS):
        sl = slice(g * gs, (g + 1) * gs)
        wm = _causal_weights(sw_ref, g)
        outs = []
        for s in range(SEQ_PER_STEP):
            rows = slice(s * sp, (s + 1) * sp)
            vpad = jnp.concatenate([vn_ref[rows, sl], jnp.zeros((BLK - sp, gs), F32)], axis=0)
            sg = jnp.dot(wm, vpad.astype(BF16), preferred_element_type=F32)[:sp, :] + sb_ref[:sp, sl]
            outs.append(u_ref[rows, sl] * sg)
        c0 = d_attn + d_pool + g * gs
        o_ref[:, c0:c0 + gs] = jnp.concatenate(outs, axis=0).astype(BF16)


def _mix_sample(sinks, q, k, v, xp, u, vn, cache_k, cache_v, state, pool_w, pool_scale,
                sg_w, sg_bias, *, layer, r_prompt, n_seq):
    d_attn, d_kv, d_pool, d_sg = q.shape[1], k.shape[1], xp.shape[1], u.shape[1]
    d_mix = d_attn + d_pool + d_sg
    rows = SEQ_PER_STEP * SAMPLE_PAD
    base = r_prompt // rows
    n_steps = n_seq // SEQ_PER_STEP
    new = lambda i: (base + i, 0)
    per_seq = lambda i: (i, 0)
    carried = lambda i: (layer * n_steps + i, 0)
    const2 = lambda i: (0, 0)
    const3 = lambda i: (0, 0, 0)
    return pl.pallas_call(
        functools.partial(_mix_sample_kernel, d_attn=d_attn, d_pool=d_pool, d_sg=d_sg),
        out_shape=jax.ShapeDtypeStruct((n_seq * SAMPLE_PAD, d_mix), BF16),
        grid=(n_seq // SEQ_PER_STEP,),
        in_specs=[
            pl.BlockSpec(memory_space=pltpu.SMEM),
            pl.BlockSpec((rows, d_attn), new),
            pl.BlockSpec((rows, d_kv), new),
            pl.BlockSpec((rows, d_kv), new),
            pl.BlockSpec((SEQ_PER_STEP * BLK, d_kv), carried),
            pl.BlockSpec((SEQ_PER_STEP * BLK, d_kv), carried),
            pl.BlockSpec((rows, d_pool), new),
            pl.BlockSpec((SEQ_PER_STEP,) + state.shape[1:], lambda i: (layer * n_steps + i, 0, 0)),
            pl.BlockSpec((rows, d_sg), new),
            pl.BlockSpec((rows, d_sg), new),
            pl.BlockSpec(pool_w.shape, const3),
            pl.BlockSpec((1, d_pool), const2),
            pl.BlockSpec(sg_w.shape, const3),
            pl.BlockSpec((BLK, d_sg), const2),
        ],
        out_specs=pl.BlockSpec((rows, d_mix), per_seq),
        compiler_params=_cparams(1),
        name="mix_sample",
    )(sinks, q, k, v, cache_k, cache_v, xp, state, u, vn, pool_w, pool_scale, sg_w, sg_bias)


ROUTE_E1, ROUTE_E2, ROUTE_G1, ROUTE_G2, ROUTE_R1, ROUTE_R2 = range(6)

SLAB_PITCH = 24


def _lane_pick(rec, k):
    lane = lax.broadcasted_iota(jnp.int32, rec.shape, 1)
    return jnp.sum(jnp.where(lane == k, rec, 0.0), axis=1, keepdims=True)


def _store_slabs(slab_ref, x):
    n, d = x.shape
    slab_ref[...] = jnp.zeros_like(slab_ref)
    for j in range(d // LANES):
        slab_ref[pl.ds(j, n, stride=SLAB_PITCH), :] = x[:, j * LANES:(j + 1) * LANES]


def _load_slab_chunk(slab_ref, j, n):
    return slab_ref[pl.ds(j, n, stride=SLAB_PITCH), :]


def _route(x1, counted, rwh_ref, rwl_ref, rb_ref, route_ref, cnt_ref, carry_ref):
    hi = x1.astype(BF16)
    lo = (x1 - hi.astype(F32)).astype(BF16)
    logits = (jnp.dot(hi, rwh_ref[...], preferred_element_type=F32)
              + jnp.dot(lo, rwh_ref[...], preferred_element_type=F32)
              + jnp.dot(hi, rwl_ref[...], preferred_element_type=F32) + rb_ref[...])
    lane_i = lax.broadcasted_iota(jnp.int32, logits.shape, 1)
    lane = lane_i.astype(F32)
    logits = jnp.where(lane_i < N_EXPERTS, logits, NEG_INF)
    m1 = jnp.max(logits, axis=1, keepdims=True)
    i1 = jnp.min(jnp.where(logits == m1, lane, float(LANES)), axis=1, keepdims=True)
    first = lane == i1
    rest = jnp.where(first, NEG_INF, logits)
    m2 = jnp.max(rest, axis=1, keepdims=True)
    i2 = jnp.min(jnp.where(rest == m2, lane, float(LANES)), axis=1, keepdims=True)
    second = lane == i2
    e = jnp.exp(m2 - m1)
    g1 = 1.0 / (1.0 + e)
    g2 = e * g1

    sel = (jnp.where(first, 1.0, 0.0) + jnp.where(second, 1.0, 0.0)) * counted
    tm = sel.shape[0]
    ri = lax.broadcasted_iota(jnp.int32, (tm, tm), 0)
    ci = lax.broadcasted_iota(jnp.int32, (tm, tm), 1)
    earlier = jnp.where(ri > ci, 1.0, 0.0).astype(BF16)
    ranks = jnp.dot(earlier, sel.astype(BF16), preferred_element_type=F32) + carry_ref[...]
    r1 = jnp.sum(jnp.where(first, ranks, 0.0), axis=1, keepdims=True)
    r2 = jnp.sum(jnp.where(second, ranks, 0.0), axis=1, keepdims=True)
    carry_ref[...] += jnp.sum(sel, axis=0, keepdims=True)
    cnt_ref[...] = carry_ref[...]

    rec = jnp.zeros_like(logits)
    for k, val in ((ROUTE_E1, i1), (ROUTE_E2, i2), (ROUTE_G1, g1), (ROUTE_G2, g2),
                   (ROUTE_R1, r1), (ROUTE_R2, r2)):
        rec = jnp.where(lane_i == k, val, rec)
    route_ref[...] = rec


N_MIX_REFS = 15


def _mix_out_kernel(*refs, alpha, with_router, n_cast, n_prompt_blocks, blocks_per_seq,
                    d_attn, d_pool, d_sg):
    mix_refs = refs[:N_MIX_REFS - 1]
    ms_ref, x_ref, w_ref, g_ref, b_ref = refs[N_MIX_REFS - 1:N_MIX_REFS + 4]
    rest = list(refs[N_MIX_REFS + 4:])
    if with_router:
        rwh_ref, rwl_ref, rb_ref = rest[:3]
        rest = rest[3:]
    cast_src, rest = rest[:n_cast], rest[n_cast:]
    if with_router:
        x1_ref, x1s_ref, route_ref, cnt_ref = rest[:4]
        rest = rest[4:]
    else:
        x1_ref, x1b_ref = rest[:2]
        rest = rest[2:]
    cast_dst, rest = rest[:n_cast], rest[n_cast:]
    if with_router:
        prev_ref, next_ref, carry_ref = rest
    else:
        prev_ref, next_ref = rest
    s = pl.program_id(0)
    _run_casts(cast_src, cast_dst)

    @pl.when(s == 0)
    def _():
        prev_ref[...] = jnp.zeros_like(prev_ref)
        if with_router:
            carry_ref[...] = jnp.zeros_like(carry_ref)

    def project():
        y = alpha * x_ref[...] + jnp.dot(prev_ref[...], w_ref[...], preferred_element_type=F32)
        x1 = _layer_norm(y, g_ref[...], b_ref[...])
        x1_ref[...] = x1
        if with_router:
            _store_slabs(x1s_ref, x1)
            counted = jnp.where(s > 0, 1.0, 0.0)
            _route(x1, counted, rwh_ref, rwl_ref, rb_ref, route_ref, cnt_ref, carry_ref)
        else:
            x1b_ref[...] = x1.astype(BF16)

    @pl.when(s < n_prompt_blocks)
    def _():
        project()
        _prompt_mixers(lax.rem(s, blocks_per_seq), *mix_refs, next_ref,
                       d_attn=d_attn, d_pool=d_pool, d_sg=d_sg)

    @pl.when(s >= n_prompt_blocks)
    def _():
        project()
        next_ref[...] = ms_ref[...]

    prev_ref[...] = next_ref[...]


def _mix_out(sinks, q, k, v, xp, u, vn, pool_w, pool_scale, sg_w, sg_bias, mixed_s, x, w, g, b,
             router=None, cast_jobs=(), *, layer, n_prompt, seq, alpha):
    r, d = x.shape
    d_attn, d_kv, d_pool, d_sg = q.shape[1], k.shape[1], xp.shape[1], u.shape[1]
    d_mix = d_attn + d_pool + d_sg
    assert n_prompt % BLK == 0 and (r - n_prompt) % BLK == 0 and mixed_s.shape[0] == r - n_prompt
    n_p = n_prompt // BLK
    n_s = (r - n_prompt) // BLK
    hist = 2 * SUBLANES
    pblk = lambda s: jnp.minimum(s, n_p - 1)
    cur = lambda s: (pblk(s), 0)
    prev = lambda s: (jnp.maximum(pblk(s) - 1, 0), 0)
    tail = lambda s: (jnp.maximum(pblk(s) * (BLK // hist) - 1, 0), 0)
    samp = lambda s: (jnp.clip(s - n_p, 0, n_s - 1), 0)
    row = lambda s: (jnp.maximum(s - 1, 0), 0)
    const2 = lambda s: (0, 0)
    const3 = lambda s: (0, 0, 0)
    in_specs = [
        pl.BlockSpec(memory_space=pltpu.SMEM),
        pl.BlockSpec((BLK, d_attn), cur),
        pl.BlockSpec((BLK, d_kv), cur),
        pl.BlockSpec((BLK, d_kv), prev),
        pl.BlockSpec((BLK, d_kv), cur),
        pl.BlockSpec((BLK, d_kv), prev),
        pl.BlockSpec((BLK, d_pool), cur),
        pl.BlockSpec((hist, d_pool), tail),
        pl.BlockSpec((BLK, d_sg), cur),
        pl.BlockSpec((BLK, d_sg), cur),
        pl.BlockSpec(pool_w.shape, const3),
        pl.BlockSpec((1, d_pool), const2),
        pl.BlockSpec(sg_w.shape, const3),
        pl.BlockSpec((BLK, d_sg), const2),
        pl.BlockSpec((BLK, d_mix), samp),
        pl.BlockSpec((BLK, d), row),
        pl.BlockSpec((None,) + w.shape[1:], lambda s: (layer, 0, 0), pipeline_mode=pl.Buffered(1)),
        pl.BlockSpec((1, d), const2),
        pl.BlockSpec((1, d), const2),
    ]
    args = [sinks, q, k, k, v, v, xp, xp, u, vn, pool_w, pool_scale, sg_w, sg_bias, mixed_s, x, w, g, b]
    assert len(args) == N_MIX_REFS + 4
    scratch = [pltpu.VMEM((BLK, d_mix), BF16), pltpu.VMEM((BLK, d_mix), BF16)]
    if router is None:
        outs = [jax.ShapeDtypeStruct((r, d), F32), jax.ShapeDtypeStruct((r, d), BF16)]
        out_specs = [pl.BlockSpec((BLK, d), row), pl.BlockSpec((BLK, d), row)]
    else:
        in_specs += [pl.BlockSpec((d, LANES), const2), pl.BlockSpec((d, LANES), const2),
                     pl.BlockSpec((1, LANES), const2)]
        args += list(router)
        outs = [jax.ShapeDtypeStruct((r, d), F32),
                jax.ShapeDtypeStruct((r * SLAB_PITCH, LANES), F32),
                jax.ShapeDtypeStruct((r, LANES), F32),
                jax.ShapeDtypeStruct((1, LANES), F32)]
        out_specs = [pl.BlockSpec((BLK, d), row),
                     pl.BlockSpec((BLK * SLAB_PITCH, LANES), row),
                     pl.BlockSpec((BLK, LANES), row),
                     pl.BlockSpec((1, LANES), const2)]
        scratch.append(pltpu.VMEM((1, LANES), F32))
    n_steps = r // BLK + 1
    c_in, c_shape, c_out, c_args = _cast_operands(cast_jobs, n_steps, lambda s: s)
    n_main = len(outs)
    res = pl.pallas_call(
        functools.partial(_mix_out_kernel, alpha=alpha, with_router=router is not None,
                          n_cast=len(cast_jobs), n_prompt_blocks=n_p, blocks_per_seq=seq // BLK,
                          d_attn=d_attn, d_pool=d_pool, d_sg=d_sg),
        out_shape=tuple(outs + c_shape),
        grid=(n_steps,),
        in_specs=in_specs + c_in,
        out_specs=tuple(out_specs + c_out),
        scratch_shapes=scratch,
        compiler_params=_cparams(1),
        name="mix_out_router" if router is not None else "mix_out",
    )(*args, *c_args)
    return res[:n_main], res[n_main:]


def _swiglu(xb, wg_ref, wu_ref):
    a = jnp.dot(xb, wg_ref[...], preferred_element_type=F32)
    c = jnp.dot(xb, wu_ref[...], preferred_element_type=F32)
    return (a * (1.0 / (1.0 + jnp.exp(-a))) * c).astype(BF16)


def _ffn_kernel(*refs, alpha, n_cast, with_zeros):
    xb_ref, x1_ref, wg_ref, wu_ref, wd_ref, g_ref, b_ref = refs[:7]
    cast_src = refs[7:7 + n_cast]
    o_ref = refs[7 + n_cast]
    cast_dst = refs[8 + n_cast:8 + 2 * n_cast]
    f = pl.program_id(1)
    if with_zeros:
        zero_ref = refs[8 + 2 * n_cast]
        zero_ref[...] = jnp.zeros_like(zero_ref)

    @pl.when(f == 0)
    def _():
        o_ref[...] = alpha * x1_ref[...]

    o_ref[...] += jnp.dot(_swiglu(xb_ref[...], wg_ref, wu_ref), wd_ref[...],
                          preferred_element_type=F32)

    @pl.when(f == pl.num_programs(1) - 1)
    def _():
        o_ref[...] = _layer_norm(o_ref[...], g_ref[...], b_ref[...])

    _run_casts(cast_src, cast_dst)


def _run_casts(src_refs, dst_refs):
    for src, dst in zip(src_refs, dst_refs):
        dst[...] = src[...].astype(BF16)


def _chunk_dst_map(*ids, step_of, last):
    return (jnp.minimum(step_of(*ids), last), 0)


def _chunk_src_map(*ids, step_of, layer, last):
    return (layer,) + _chunk_dst_map(*ids, step_of=step_of, last=last)


def _chunk_rows(rows, steps):
    for t in range(2 * SUBLANES, rows + 1, 2 * SUBLANES):
        if rows % t == 0 and rows // t <= steps:
            return t
    return rows


def _cast_operands(jobs, steps, step_of):
    in_specs, out_shape, out_specs, args = [], [], [], []
    for arr, li in jobs:
        _, rows, cols = arr.shape
        cr = _chunk_rows(rows, steps)
        last = rows // cr - 1
        in_specs.append(pl.BlockSpec(
            (None, cr, cols), functools.partial(_chunk_src_map, step_of=step_of, layer=li, last=last)))
        out_shape.append(jax.ShapeDtypeStruct((rows, cols), BF16))
        out_specs.append(pl.BlockSpec(
            (cr, cols), functools.partial(_chunk_dst_map, step_of=step_of, last=last)))
        args.append(arr)
    return in_specs, out_shape, out_specs, args


def _ffn(xb, x1, wg, wu, wd, g, b, cast_jobs=(), zero_rows=0, *, layer, tm, tf, alpha):
    r, d = x1.shape
    ff = wg.shape[2]
    nf = ff // tf
    steps = (r // tm) * nf
    step_of = lambda i, f: i * nf + f
    row = lambda i, f: (i, 0)
    const = lambda i, f: (0, 0)
    in_specs = [
        pl.BlockSpec((tm, d), row),
        pl.BlockSpec((tm, d), row, pipeline_mode=pl.Buffered(1)),
        pl.BlockSpec((None, d, tf), lambda i, f: (layer, 0, f)),
        pl.BlockSpec((None, d, tf), lambda i, f: (layer, 0, f)),
        pl.BlockSpec((None, tf, d), lambda i, f: (layer, f, 0)),
        pl.BlockSpec((1, d), const),
        pl.BlockSpec((1, d), const),
    ]
    c_in, c_shape, c_out, c_args = _cast_operands(cast_jobs, steps, step_of)
    out_shape = [jax.ShapeDtypeStruct((r, d), F32)] + c_shape
    out_specs = [pl.BlockSpec((tm, d), row)] + c_out
    if zero_rows:
        zr = _chunk_rows(zero_rows, steps)
        out_shape.append(jax.ShapeDtypeStruct((zero_rows, LANES), F32))
        out_specs.append(pl.BlockSpec(
            (zr, LANES), functools.partial(_chunk_dst_map, step_of=step_of, last=zero_rows // zr - 1)))
    outs = pl.pallas_call(
        functools.partial(_ffn_kernel, alpha=alpha, n_cast=len(cast_jobs), with_zeros=bool(zero_rows)),
        out_shape=tuple(out_shape),
        grid=(r // tm, nf),
        in_specs=in_specs + c_in,
        out_specs=tuple(out_specs),
        compiler_params=_cparams(2),
        name="ffn",
    )(xb, x1, wg, wu, wd, g, b, *c_args)
    n = len(cast_jobs)
    return outs[0], outs[1:1 + n], (outs[1 + n] if zero_rows else None)


MOE_BLOCK = 512
DMA_RING = 128
DMA_UNROLL = 8


def _ring_depth(n):
    return 1 << (min(DMA_RING, n).bit_length() - 1)


def _routing_tables(route, counts, *, n_work):
    e1 = route[:, ROUTE_E1].astype(jnp.int32)
    e2 = route[:, ROUTE_E2].astype(jnp.int32)
    r1 = route[:, ROUTE_R1].astype(jnp.int32)
    r2 = route[:, ROUTE_R2].astype(jnp.int32)
    cnt = counts[0, :N_EXPERTS].astype(jnp.int32)
    nblk = (cnt + MOE_BLOCK - 1) // MOE_BLOCK
    blk_end = jnp.cumsum(nblk)
    start = (blk_end - nblk) * MOE_BLOCK
    slot1 = jnp.take(start, e1) + r1
    slot2 = jnp.take(start, e2) + r2
    n_active = blk_end[-1]
    blk = jnp.minimum(jnp.arange(n_work, dtype=jnp.int32), n_active - 1)
    work_exp = jnp.sum(blk[:, None] >= blk_end[None, :], axis=1).astype(jnp.int32)
    return slot1, slot2, work_exp, n_active.reshape(1)


def _ring_copies(n, ring, make_copies):
    def start(i):
        for c in make_copies(i):
            c.start()

    def wait(i):
        for c in make_copies(i):
            c.wait()

    def fill(i, carry):
        start(i)
        return carry

    def steady(i, carry):
        wait(i - ring)
        start(i)
        return carry

    def drain(i, carry):
        wait(i)
        return carry

    lax.fori_loop(0, ring, fill, 0, unroll=DMA_UNROLL)
    lax.fori_loop(ring, n, steady, 0, unroll=DMA_UNROLL)
    lax.fori_loop(n - ring, n, drain, 0, unroll=DMA_UNROLL)


SLAB_ROWS = 16


def _slab(ref, token):
    return ref.at[pl.ds(pl.multiple_of(token * SLAB_PITCH, SUBLANES), SLAB_ROWS), :]


def _dispatch_kernel(s1_ref, s2_ref, x1s_ref, xs_init, xs_hbm, sem):
    del xs_init
    tm = x1s_ref.shape[0] // SLAB_PITCH
    base = pl.program_id(0) * tm
    ring = sem.shape[1]

    def copies(r):
        k = r & (ring - 1)
        src = _slab(x1s_ref, r)
        return (pltpu.make_async_copy(src, _slab(xs_hbm, s1_ref[base + r]), sem.at[0, k]),
                pltpu.make_async_copy(src, _slab(xs_hbm, s2_ref[base + r]), sem.at[1, k]))

    _ring_copies(tm, ring, copies)


def _moe_dispatch(slot1, slot2, x1s, xs_init, *, tm):
    r = x1s.shape[0] // SLAB_PITCH
    return pl.pallas_call(
        _dispatch_kernel,
        out_shape=jax.ShapeDtypeStruct(xs_init.shape, F32),
        grid_spec=pltpu.PrefetchScalarGridSpec(
            num_scalar_prefetch=2,
            grid=(r // tm,),
            in_specs=[pl.BlockSpec((tm * SLAB_PITCH, LANES), lambda i, s1, s2: (i, 0)),
                      pl.BlockSpec(memory_space=pl.ANY)],
            out_specs=pl.BlockSpec(memory_space=pl.ANY),
            scratch_shapes=[pltpu.SemaphoreType.DMA((2, _ring_depth(tm)))],
        ),
        input_output_aliases={3: 0},
        compiler_params=_cparams(1),
        name="moe_dispatch",
    )(slot1, slot2, x1s, xs_init)


def _moe_ffn_kernel(we_ref, na_ref, xs_ref, wg_ref, wu_ref, wd_ref, ys_ref, xb_ref, acc_ref):
    del we_ref
    w = pl.program_id(0)
    f = pl.program_id(1)
    n, d = acc_ref.shape

    active = w < na_ref[0]

    def partial_out():
        return jnp.dot(_swiglu(xb_ref[...], wg_ref, wu_ref), wd_ref[...],
                       preferred_element_type=F32)

    @pl.when(active & (f == 0))
    def _():
        for j in range(d // LANES):
            xb_ref[:, j * LANES:(j + 1) * LANES] = _load_slab_chunk(xs_ref, j, n).astype(BF16)
        acc_ref[...] = partial_out()

    @pl.when(active & (f > 0))
    def _():
        acc_ref[...] += partial_out()

    @pl.when(jnp.logical_not(active) & (f == 0))
    def _():
        acc_ref[...] = jnp.zeros_like(acc_ref)

    @pl.when(f == pl.num_programs(1) - 1)
    def _():
        _store_slabs(ys_ref, acc_ref[...])


def _moe_ffn(work_exp, n_active, xs, wg, wu, wd, *, layer, tf):
    d = wg.shape[2]
    ff = wg.shape[3]
    nf = ff // tf
    rows = lambda w, f, we, na: (w, 0)
    chunk = lambda w, f, na: jnp.where(w < na[0], f, nf - 1)
    return pl.pallas_call(
        _moe_ffn_kernel,
        out_shape=jax.ShapeDtypeStruct(xs.shape, F32),
        grid_spec=pltpu.PrefetchScalarGridSpec(
            num_scalar_prefetch=2,
            grid=(xs.shape[0] // (MOE_BLOCK * SLAB_PITCH), nf),
            in_specs=[
                pl.BlockSpec((MOE_BLOCK * SLAB_PITCH, LANES), rows),
                pl.BlockSpec((None, None, d, tf),
                             lambda w, f, we, na: (layer, we[w], 0, chunk(w, f, na))),
                pl.BlockSpec((None, None, d, tf),
                             lambda w, f, we, na: (layer, we[w], 0, chunk(w, f, na))),
                pl.BlockSpec((None, None, tf, d),
                             lambda w, f, we, na: (layer, we[w], chunk(w, f, na), 0)),
            ],
            out_specs=pl.BlockSpec((MOE_BLOCK * SLAB_PITCH, LANES), rows),
            scratch_shapes=[pltpu.VMEM((MOE_BLOCK, d), BF16), pltpu.VMEM((MOE_BLOCK, d), F32)],
        ),
        compiler_params=_cparams(2),
        name="moe_ffn",
    )(work_exp, n_active, xs, wg, wu, wd)


def _combine_kernel(s1_ref, s2_ref, ys_hbm, x1_ref, route_ref, g_ref, b_ref, o_ref,
                    y1_ref, y2_ref, sem, *, alpha):
    tm, d = o_ref.shape
    base = pl.program_id(0) * tm
    ring = sem.shape[1]

    def copies(r):
        k = r & (ring - 1)
        return (pltpu.make_async_copy(_slab(ys_hbm, s1_ref[base + r]), _slab(y1_ref, r), sem.at[0, k]),
                pltpu.make_async_copy(_slab(ys_hbm, s2_ref[base + r]), _slab(y2_ref, r), sem.at[1, k]))

    _ring_copies(tm, ring, copies)
    route = route_ref[...]
    g1 = _lane_pick(route, ROUTE_G1)
    g2 = _lane_pick(route, ROUTE_G2)
    for j in range(d // LANES):
        sl = slice(j * LANES, (j + 1) * LANES)
        o_ref[:, sl] = (alpha * x1_ref[:, sl] + g1 * _load_slab_chunk(y1_ref, j, tm)
                        + g2 * _load_slab_chunk(y2_ref, j, tm))
    o_ref[...] = _layer_norm(o_ref[...], g_ref[...], b_ref[...])


def _moe_combine(slot1, slot2, ys, x1, route, g, b, *, tm, alpha):
    r, d = x1.shape
    row = lambda i, s1, s2: (i, 0)
    const = lambda i, s1, s2: (0, 0)
    return pl.pallas_call(
        functools.partial(_combine_kernel, alpha=alpha),
        out_shape=jax.ShapeDtypeStruct((r, d), F32),
        grid_spec=pltpu.PrefetchScalarGridSpec(
            num_scalar_prefetch=2,
            grid=(r // tm,),
            in_specs=[
                pl.BlockSpec(memory_space=pl.ANY),
                pl.BlockSpec((tm, d), row),
                pl.BlockSpec((tm, LANES), row),
                pl.BlockSpec((1, d), const),
                pl.BlockSpec((1, d), const),
            ],
            out_specs=pl.BlockSpec((tm, d), row),
            scratch_shapes=[pltpu.VMEM((tm * SLAB_PITCH, LANES), F32),
                            pltpu.VMEM((tm * SLAB_PITCH, LANES), F32),
                            pltpu.SemaphoreType.DMA((2, _ring_depth(tm)))],
        ),
        compiler_params=_cparams(1),
        name="moe_combine",
    )(slot1, slot2, ys, x1, route, g, b)


def _moe_work_items(r):
    return (2 * r + MOE_BLOCK - 1) // MOE_BLOCK + N_EXPERTS


def _moe(x1, x1s, route, counts, wg, wu, wd, g, b, xs_buf=None, *, layer, tm, tf, alpha):
    r = x1.shape[0]
    n_work = _moe_work_items(r)
    slot1, slot2, work_exp, n_active = _routing_tables(route, counts, n_work=n_work)
    if xs_buf is None:
        xs_buf = jnp.zeros((n_work * MOE_BLOCK * SLAB_PITCH, LANES), F32)
    xs = _moe_dispatch(slot1, slot2, x1s, xs_buf, tm=tm)
    ys = _moe_ffn(work_exp, n_active, xs, wg, wu, wd, layer=layer, tf=tf)
    return _moe_combine(slot1, slot2, ys, x1, route, g, b, tm=tm, alpha=alpha), xs


def _rope_tables(pos):
    half = HEAD_DIM // 2
    inv = ROPE_THETA ** (-jnp.arange(half, dtype=F32) / half)
    ang = pos.astype(F32)[:, None] * inv[None, :]
    cos = jnp.cos(ang)
    sin = jnp.sin(ang)
    cos2 = jnp.concatenate([cos, cos, cos, cos], axis=1)
    sin2 = jnp.concatenate([-sin, sin, -sin, sin], axis=1)
    return cos2, sin2


def _row_tile(r, cap):
    best = 16
    for t in range(16, cap + 1, 16):
        if r % t == 0:
            best = t
    return best


def kernel(x_prompt, x_sample, cache_k, cache_v, state_pool, w_in, w_out, attn_sinks, pool_w,
           pool_scale, sg_w, sg_b, sg_norm_g, sg_norm_b, ln1_g, ln1_b, ln2_g, ln2_b,
           ffn_w_gate, ffn_w_up, ffn_w_down, router_w, router_b, moe_w_gate, moe_w_up, moe_w_down):
    bp, tp, d = x_prompt.shape
    bs, ts, _ = x_sample.shape
    depth = w_in.shape[0]
    d_kv = cache_k.shape[3] * cache_k.shape[4]
    d_pool = pool_scale.shape[1]
    d_sg = sg_norm_g.shape[1]
    d_attn = w_in.shape[2] - 2 * d_kv - d_pool - 2 * d_sg
    win_buf = cache_k.shape[2]
    assert win_buf == WINDOW == BLK and tp % BLK == 0 and tp >= BLK and ts <= SAMPLE_PAD
    assert bs % SEQ_PER_STEP == 0 and state_pool.shape[2] == POOL_STATE
    assert cache_k.shape[3] == N_KV_HEADS and cache_k.shape[4] == HEAD_DIM
    assert d == SLAB_ROWS * LANES
    alpha = (2 * depth) ** 0.25

    r_p = bp * tp
    r_s = bs * SAMPLE_PAD
    r = r_p + r_s
    tm = _row_tile(r, 768)
    tm_ffn = _row_tile(r, 704)
    tm_comb = _row_tile(r, 384)

    xs_pad = jnp.pad(x_sample, ((0, 0), (0, SAMPLE_PAD - ts), (0, 0)))
    x = jnp.concatenate([x_prompt.reshape(r_p, d), xs_pad.reshape(r_s, d)], axis=0)

    pos = jnp.concatenate([
        jnp.tile(jnp.arange(tp, dtype=jnp.int32), bp),
        jnp.tile(PAST_LEN + jnp.arange(SAMPLE_PAD, dtype=jnp.int32), bs)])
    cos, sin = _rope_tables(pos)

    in_w = {0: w_in[0].astype(BF16)}
    out_w = {0: w_out[0].astype(BF16)}
    pool_w_b = pool_w.astype(BF16)
    ffn_bf16 = {}
    moe_bf16 = {}
    rw_pad = jnp.pad(router_w, ((0, 0), (0, 0), (0, LANES - router_w.shape[2])))
    rw_hi = rw_pad.astype(BF16)
    rw_lo = (rw_pad - rw_hi.astype(F32)).astype(BF16)
    rb_pad = jnp.pad(router_b, ((0, 0), (0, LANES - router_b.shape[1])))[:, None, :]
    sg_bias = jnp.repeat(jnp.swapaxes(sg_b, 1, 2), d_sg // N_SG_GROUPS, axis=2)
    state_pad = jnp.pad(state_pool, ((0, 0), (0, 0), (1, 0), (0, 0)))
    state_rows = state_pad.reshape((depth * bs,) + state_pad.shape[2:])
    ck_rows = cache_k.reshape(depth * bs * win_buf, d_kv)
    cv_rows = cache_v.reshape(depth * bs * win_buf, d_kv)

    k_l, v_l, xp_l, vn_l = [], [], [], []
    xs_buf = None
    for l in range(depth):
        q, k, v, xp, u, vn = _inproj(
            x, in_w[l][None], cos, sin, sg_norm_g[l][None], sg_norm_b[l][None],
            layer=0, tm=tm, d_attn=d_attn, d_kv=d_kv, d_pool=d_pool, d_sg=d_sg)
        mixed_s = _mix_sample(attn_sinks[l], q, k, v, xp, u, vn, ck_rows, cv_rows, state_rows,
                              pool_w_b[l], pool_scale[l][None], sg_w[l], sg_bias[l],
                              layer=l, r_prompt=r_p, n_seq=bs)
        i = l // 2
        is_dense = l % 2 == 0
        jobs = []
        if l + 1 < depth:
            jobs += [(w_in, l + 1), (w_out, l + 1)]
        if is_dense:
            jobs += [(ffn_w_gate, i), (ffn_w_up, i), (ffn_w_down, i)]
        main, casted = _mix_out(
            attn_sinks[l], q, k, v, xp, u, vn, pool_w_b[l], pool_scale[l][None], sg_w[l],
            sg_bias[l], mixed_s, x, out_w[l][None], ln1_g[l][None], ln1_b[l][None],
            None if is_dense else (rw_hi[i], rw_lo[i], rb_pad[i]), tuple(jobs),
            layer=0, n_prompt=r_p, seq=tp, alpha=alpha)
        casted = list(casted)
        if l + 1 < depth:
            in_w[l + 1], out_w[l + 1] = casted[:2]
            casted = casted[2:]
        if is_dense:
            x1, x1b = main
            ffn_g, ffn_u, ffn_d = (c[None] for c in casted)
            jobs = ()
            zero_rows = 0
            if l + 1 < depth:
                jobs = tuple((w.reshape(w.shape[0], -1, w.shape[-1]), (l + 1) // 2)
                             for w in (moe_w_gate, moe_w_up, moe_w_down))
                if xs_buf is None:
                    zero_rows = _moe_work_items(r) * MOE_BLOCK * SLAB_PITCH
            x, casted, zeros = _ffn(x1b, x1, ffn_g, ffn_u, ffn_d, ln2_g[l][None], ln2_b[l][None],
                                    jobs, zero_rows, layer=0, tm=tm_ffn, tf=512, alpha=alpha)
            if zero_rows:
                xs_buf = zeros
            if jobs:
                moe_bf16[(l + 1) // 2] = tuple(
                    c.reshape((1,) + w.shape[1:])
                    for c, w in zip(casted, (moe_w_gate, moe_w_up, moe_w_down)))
        else:
            x1, x1s, route, counts = main
            if i not in moe_bf16:
                moe_bf16[i] = tuple(w[i][None].astype(BF16)
                                    for w in (moe_w_gate, moe_w_up, moe_w_down))
            moe_g, moe_u, moe_d = moe_bf16[i]
            x, xs_buf = _moe(x1, x1s, route, counts, moe_g, moe_u, moe_d, ln2_g[l][None],
                             ln2_b[l][None], xs_buf, layer=0, tm=tm_comb, tf=512, alpha=alpha)

        tail = max(WINDOW, POOL_STATE)
        keep = lambda a: jnp.concatenate(
            [a[:r_p].reshape(bp, tp, -1)[:, -tail:].reshape(bp * tail, -1), a[r_p:]], axis=0)
        k_l.append(keep(k))
        v_l.append(keep(v))
        xp_l.append(keep(xp))
        vn_l.append(vn[r_p:])

    kh = (N_KV_HEADS, HEAD_DIM)
    n_tail = bp * max(WINDOW, POOL_STATE)
    ks, vs, xps, vns = (jnp.stack(a) for a in (k_l, v_l, xp_l, vn_l))
    prompt_tail = lambda a, n: a[:, :n_tail].reshape(depth, bp, -1, a.shape[-1])[:, :, -n:]
    sample_new = lambda a: a[:, -r_s:].reshape(depth, bs, SAMPLE_PAD, a.shape[-1])[:, :, :ts]
    new_k_p = prompt_tail(ks, WINDOW).reshape(depth, bp, WINDOW, *kh)
    new_v_p = prompt_tail(vs, WINDOW).reshape(depth, bp, WINDOW, *kh)
    new_pool_p = prompt_tail(xps, POOL_STATE)
    new_k_s = jnp.concatenate(
        [cache_k, sample_new(ks).reshape(depth, bs, ts, *kh)], axis=2)[:, :, -win_buf:]
    new_v_s = jnp.concatenate(
        [cache_v, sample_new(vs).reshape(depth, bs, ts, *kh)], axis=2)[:, :, -win_buf:]
    new_pool_s = jnp.concatenate([state_pool, sample_new(xps)], axis=2)[:, :, -POOL_STATE:]
    new_sg_s = vns.reshape(depth, bs, SAMPLE_PAD, d_sg)[:, :, :ts]

    y_prompt = x[:r_p].reshape(bp, tp, d)
    y_sample = x[r_p:].reshape(bs, SAMPLE_PAD, d)[:, :ts]
    return (y_prompt, y_sample, new_k_p, new_v_p, new_pool_p, new_k_s, new_v_s, new_pool_s, new_sg_s)
```

```python
import functools
import math

import jax
import jax.numpy as jnp
from jax import lax
from jax.experimental import pallas as pl
from jax.experimental.pallas import tpu as pltpu

F32 = jnp.float32
BF16 = jnp.bfloat16

PAST_LEN = 16384
HEAD_DIM = 64
N_KV_HEADS = 2
WINDOW = 128
BLK = 128
ROPE_THETA = 10000.0
POOL_WINDOWS = (2, 4, 8, 16)
POOL_STATE = 15
N_SG_GROUPS = 4
N_EXPERTS = 8
LN_EPS = 1e-5

LANES = 128
SUBLANES = 8
SAMPLE_PAD = SUBLANES
VMEM_LIMIT = 56 * 1024 * 1024

NEG_INF = float("-inf")


def _cparams(n_axes):
    return pltpu.CompilerParams(
        dimension_semantics=("arbitrary",) * n_axes, vmem_limit_bytes=VMEM_LIMIT)


def _layer_norm(x, g, b):
    mu = jnp.mean(x, axis=-1, keepdims=True)
    xc = x - mu
    var = jnp.mean(xc * xc, axis=-1, keepdims=True)
    return xc * lax.rsqrt(var + LN_EPS) * g + b


def _gelu_tanh(x):
    return 0.5 * x * (1.0 + jnp.tanh(0.7978845608028654 * (x + 0.044715 * (x * x * x))))


def _inproj_kernel(x_ref, w_ref, cos_ref, sin_ref, g_ref, b_ref,
                   q_ref, k_ref, v_ref, xp_ref, u_ref, vn_ref, *, d_attn, d_kv, d_pool, d_sg):
    xb = x_ref[...].astype(BF16)
    cos = cos_ref[...]
    sin = sin_ref[...]
    tm = xb.shape[0]
    lane = lax.broadcasted_iota(jnp.int32, (tm, LANES), 1)
    first_half = (lane % HEAD_DIM) < (HEAD_DIM // 2)

    def rope(z):
        rot = jnp.where(first_half, pltpu.roll(z, LANES - HEAD_DIM // 2, 1),
                        pltpu.roll(z, HEAD_DIM // 2, 1))
        return z * cos + rot * sin

    c0 = 0
    zq = jnp.dot(xb, w_ref[:, c0:c0 + d_attn], preferred_element_type=F32)
    for j in range(d_attn // LANES):
        sl = slice(j * LANES, (j + 1) * LANES)
        q_ref[:, sl] = (rope(zq[:, sl]) * (HEAD_DIM ** -0.5)).astype(BF16)
    c0 += d_attn
    zkv = jnp.dot(xb, w_ref[:, c0:c0 + 2 * d_kv], preferred_element_type=F32)
    k_ref[...] = rope(zkv[:, :d_kv])
    v_ref[...] = zkv[:, d_kv:]
    c0 += 2 * d_kv
    xp_ref[...] = jnp.dot(xb, w_ref[:, c0:c0 + d_pool], preferred_element_type=F32)
    c0 += d_pool
    u_ref[...] = _gelu_tanh(jnp.dot(xb, w_ref[:, c0:c0 + d_sg], preferred_element_type=F32))
    c0 += d_sg
    vg = _gelu_tanh(jnp.dot(xb, w_ref[:, c0:c0 + d_sg], preferred_element_type=F32))
    gd = d_sg // N_SG_GROUPS
    for g in range(N_SG_GROUPS):
        sl = slice(g * gd, (g + 1) * gd)
        vn_ref[:, sl] = _layer_norm(vg[:, sl], g_ref[:, sl], b_ref[:, sl])


def _inproj(x, w, cos, sin, sg_g, sg_b, *, layer, tm, d_attn, d_kv, d_pool, d_sg):
    r, d = x.shape
    d_in = w.shape[2]
    row = lambda i: (i, 0)
    const = lambda i: (0, 0)
    outs = (
        jax.ShapeDtypeStruct((r, d_attn), BF16),
        jax.ShapeDtypeStruct((r, d_kv), F32),
        jax.ShapeDtypeStruct((r, d_kv), F32),
        jax.ShapeDtypeStruct((r, d_pool), F32),
        jax.ShapeDtypeStruct((r, d_sg), F32),
        jax.ShapeDtypeStruct((r, d_sg), F32),
    )
    return pl.pallas_call(
        functools.partial(_inproj_kernel, d_attn=d_attn, d_kv=d_kv, d_pool=d_pool, d_sg=d_sg),
        out_shape=outs,
        grid=(r // tm,),
        in_specs=[
            pl.BlockSpec((tm, d), row),
            pl.BlockSpec((None, d, d_in), lambda i: (layer, 0, 0), pipeline_mode=pl.Buffered(1)),
            pl.BlockSpec((tm, LANES), row),
            pl.BlockSpec((tm, LANES), row),
            pl.BlockSpec((1, d_sg), const),
            pl.BlockSpec((1, d_sg), const),
        ],
        out_specs=(
            pl.BlockSpec((tm, d_attn), row),
            pl.BlockSpec((tm, d_kv), row),
            pl.BlockSpec((tm, d_kv), row),
            pl.BlockSpec((tm, d_pool), row),
            pl.BlockSpec((tm, d_sg), row),
            pl.BlockSpec((tm, d_sg), row),
        ),
        compiler_params=_cparams(1),
        name="inproj",
    )(x, w, cos, sin, sg_g, sg_b)


def _split_kv(kv):
    lane = lax.broadcasted_iota(jnp.int32, kv.shape, 1)
    low = lane < HEAD_DIM
    a = kv.astype(BF16)
    b = pltpu.roll(kv, HEAD_DIM, 1).astype(BF16)
    zero = jnp.zeros_like(a)
    head0 = (jnp.where(low, a, zero), jnp.where(low, zero, b))
    head1 = (jnp.where(low, b, zero), jnp.where(low, zero, a))
    return head0, head1


def _attend(q, k_pair, v_pair, allowed, sinks):
    out = None
    for kx, vx, sk in zip(k_pair, v_pair, sinks):
        s = lax.dot_general(q, kx, (((1,), (1,)), ((), ())), preferred_element_type=F32)
        s = jnp.where(allowed, s, NEG_INF)
        m = jnp.maximum(jnp.max(s, axis=1, keepdims=True), sk)
        p = jnp.exp(s - m)
        denom = jnp.sum(p, axis=1, keepdims=True) + jnp.exp(sk - m)
        o = jnp.dot(p.astype(BF16), vx, preferred_element_type=F32) * (1.0 / denom)
        out = o if out is None else out + o
    return out


def _window_sums(full):
    sums = {1: full}
    w = 1
    while w < max(POOL_WINDOWS):
        sums[2 * w] = sums[w] + pltpu.roll(sums[w], w, 0)
        w *= 2
    return sums


def _pool_mix(full, cur, cnt_of, pw_ref, ps_ref, row0):
    n = cur.shape[0]
    gd = cur.shape[1] // len(POOL_WINDOWS)
    outs = []
    for g, w in enumerate(POOL_WINDOWS):
        sl = slice(g * gd, (g + 1) * gd)
        sums = _window_sums(full[:, sl])[w]
        pooled = sums[row0:row0 + n, :] / cnt_of(w)
        d = (pooled - cur[:, sl]).astype(BF16)
        outs.append(jnp.dot(d, pw_ref[g], preferred_element_type=F32) * ps_ref[:, sl])
    return outs


def _causal_weights(sw_ref, g):
    n = sw_ref.shape[1]
    ri = lax.broadcasted_iota(jnp.int32, (n, n), 0)
    ci = lax.broadcasted_iota(jnp.int32, (n, n), 1)
    return jnp.where(ri >= ci, sw_ref[g], 0.0).astype(BF16)


def _prompt_mixers(i, sink_ref, q_ref, kc_ref, kp_ref, vc_ref, vp_ref, xc_ref, xt_ref, u_ref,
                   vn_ref, pw_ref, ps_ref, sw_ref, sb_ref, o_ref, *, d_attn, d_pool, d_sg):
    has_prev = i > 0

    kall = jnp.concatenate([kp_ref[...], kc_ref[...]], axis=0)
    vall = jnp.concatenate([vp_ref[...], vc_ref[...]], axis=0)
    k_heads = _split_kv(kall)
    v_heads = _split_kv(vall)
    ri = lax.broadcasted_iota(jnp.int32, (BLK, 2 * BLK), 0)
    cj = lax.broadcasted_iota(jnp.int32, (BLK, 2 * BLK), 1)
    first_key = jnp.where(has_prev, 0, BLK)
    allowed = (cj >= ri) & (cj <= ri + WINDOW) & (cj >= first_key)
    n_pairs = d_attn // LANES
    pairs_per_kv = n_pairs // N_KV_HEADS
    for p in range(n_pairs):
        g = p // pairs_per_kv
        sl = slice(p * LANES, (p + 1) * LANES)
        o = _attend(q_ref[:, sl], k_heads[g], v_heads[g], allowed,
                    (sink_ref[2 * p], sink_ref[2 * p + 1]))
        o_ref[:, sl] = o.astype(BF16)

    xc = xc_ref[...]
    tail = jnp.where(has_prev, xt_ref[...], 0.0)
    full = jnp.concatenate([tail, xc], axis=0)
    hist = tail.shape[0]
    pos = i * BLK + lax.broadcasted_iota(jnp.int32, (BLK, 1), 0)
    cnt_of = lambda w: jnp.minimum(pos + 1, w).astype(F32)
    pooled = _pool_mix(full, xc, cnt_of, pw_ref, ps_ref, hist)
    gd = d_pool // len(POOL_WINDOWS)
    for g, y in enumerate(pooled):
        o_ref[:, d_attn + g * gd:d_attn + (g + 1) * gd] = y.astype(BF16)

    gs = d_sg // N_SG_GROUPS
    for g in range(N_SG_GROUPS):
        sl = slice(g * gs, (g + 1) * gs)
        s = jnp.dot(_causal_weights(sw_ref, g), vn_ref[:, sl].astype(BF16),
                    preferred_element_type=F32) + sb_ref[:, sl]
        c0 = d_attn + d_pool + g * gs
        o_ref[:, c0:c0 + gs] = (u_ref[:, sl] * s).astype(BF16)


SEQ_PER_STEP = 2


def _mix_sample_kernel(sink_ref, q_ref, kn_ref, vn_new_ref, ck_ref, cv_ref, xn_ref, st_ref,
                       u_ref, vn_ref, pw_ref, ps_ref, sw_ref, sb_ref, o_ref, *, d_attn, d_pool, d_sg):
    sp = SAMPLE_PAD
    n_pairs = d_attn // LANES
    pairs_per_kv = n_pairs // N_KV_HEADS
    m_rows = pairs_per_kv * sp
    qf = q_ref[...].astype(F32)

    ri = lax.broadcasted_iota(jnp.int32, (m_rows, 2 * BLK), 0) % sp
    cj = lax.broadcasted_iota(jnp.int32, (m_rows, 2 * BLK), 1)
    allowed = (cj >= ri) & (cj <= ri + WINDOW)
    row_pair = lax.broadcasted_iota(jnp.int32, (m_rows, 1), 0) // sp

    attn_rows = []
    for s in range(SEQ_PER_STEP):
        rows = slice(s * sp, (s + 1) * sp)
        zpad = jnp.zeros((BLK - sp, kn_ref.shape[1]), F32)
        kall = jnp.concatenate([ck_ref[s * BLK:(s + 1) * BLK, :], kn_ref[rows, :], zpad], axis=0)
        vall = jnp.concatenate([cv_ref[s * BLK:(s + 1) * BLK, :], vn_new_ref[rows, :], zpad], axis=0)
        k_heads = _split_kv(kall)
        v_heads = _split_kv(vall)
        per_pair = []
        for g in range(N_KV_HEADS):
            qst = jnp.concatenate(
                [qf[rows, (g * pairs_per_kv + pl_) * LANES:(g * pairs_per_kv + pl_ + 1) * LANES]
                 for pl_ in range(pairs_per_kv)], axis=0).astype(BF16)
            sink_lo = jnp.zeros((m_rows, 1), F32)
            sink_hi = jnp.zeros((m_rows, 1), F32)
            for pl_ in range(pairs_per_kv):
                h = 2 * (g * pairs_per_kv + pl_)
                sink_lo = jnp.where(row_pair == pl_, sink_ref[h], sink_lo)
                sink_hi = jnp.where(row_pair == pl_, sink_ref[h + 1], sink_hi)
            o = _attend(qst, k_heads[g], v_heads[g], allowed, (sink_lo, sink_hi))
            per_pair.extend(o[pl_ * sp:(pl_ + 1) * sp, :] for pl_ in range(pairs_per_kv))
        attn_rows.append(per_pair)
    for p in range(n_pairs):
        o_ref[:, p * LANES:(p + 1) * LANES] = jnp.concatenate(
            [attn_rows[s][p] for s in range(SEQ_PER_STEP)], axis=0).astype(BF16)

    gd = d_pool // len(POOL_WINDOWS)
    pooled = []
    for s in range(SEQ_PER_STEP):
        rows = slice(s * sp, (s + 1) * sp)
        xc = xn_ref[rows, :]
        full = jnp.concatenate([st_ref[s], xc], axis=0)
        cnt_of = lambda w: float(w)
        pooled.append(_pool_mix(full, xc, cnt_of, pw_ref, ps_ref, st_ref.shape[1]))
    for g in range(len(POOL_WINDOWS)):
        o_ref[:, d_attn + g * gd:d_attn + (g + 1) * gd] = jnp.concatenate(
            [pooled[s][g] for s in range(SEQ_PER_STEP)], axis=0).astype(BF16)

    gs = d_sg // N_SG_GROUPS
    for g in range(N_SG_GROUPS):
        sl = slice(g * gs, (g + 1) * gs)
        wm = _causal_weights(sw_ref, g)
        outs = []
        for s in range(SEQ_PER_STEP):
            rows = slice(s * sp, (s + 1) * sp)
            vpad = jnp.concatenate([vn_ref[rows, sl], jnp.zeros((BLK - sp, gs), F32)], axis=0)
            sg = jnp.dot(wm, vpad.astype(BF16), preferred_element_type=F32)[:sp, :] + sb_ref[:sp, sl]
            outs.append(u_ref[rows, sl] * sg)
        c0 = d_attn + d_pool + g * gs
        o_ref[:, c0:c0 + gs] = jnp.concatenate(outs, axis=0).astype(BF16)


def _mix_sample(sinks, q, k, v, xp, u, vn, cache_k, cache_v, state, pool_w, pool_scale,
                sg_w, sg_bias, *, layer, r_prompt, n_seq):
    d_attn, d_kv, d_pool, d_sg = q.shape[1], k.shape[1], xp.shape[1], u.shape[1]
    d_mix = d_attn + d_pool + d_sg
    rows = SEQ_PER_STEP * SAMPLE_PAD
    base = r_prompt // rows
    n_steps = n_seq // SEQ_PER_STEP
    new = lambda i: (base + i, 0)
    per_seq = lambda i: (i, 0)
    carried = lambda i: (layer * n_steps + i, 0)
    const2 = lambda i: (0, 0)
    const3 = lambda i: (0, 0, 0)
    return pl.pallas_call(
        functools.partial(_mix_sample_kernel, d_attn=d_attn, d_pool=d_pool, d_sg=d_sg),
        out_shape=jax.ShapeDtypeStruct((n_seq * SAMPLE_PAD, d_mix), BF16),
        grid=(n_seq // SEQ_PER_STEP,),
        in_specs=[
            pl.BlockSpec(memory_space=pltpu.SMEM),
            pl.BlockSpec((rows, d_attn), new),
            pl.BlockSpec((rows, d_kv), new),
            pl.BlockSpec((rows, d_kv), new),
            pl.BlockSpec((SEQ_PER_STEP * BLK, d_kv), carried),
            pl.BlockSpec((SEQ_PER_STEP * BLK, d_kv), carried),
            pl.BlockSpec((rows, d_pool), new),
            pl.BlockSpec((SEQ_PER_STEP,) + state.shape[1:], lambda i: (layer * n_steps + i, 0, 0)),
            pl.BlockSpec((rows, d_sg), new),
            pl.BlockSpec((rows, d_sg), new),
            pl.BlockSpec(pool_w.shape, const3),
            pl.BlockSpec((1, d_pool), const2),
            pl.BlockSpec(sg_w.shape, const3),
            pl.BlockSpec((BLK, d_sg), const2),
        ],
        out_specs=pl.BlockSpec((rows, d_mix), per_seq),
        compiler_params=_cparams(1),
        name="mix_sample",
    )(sinks, q, k, v, cache_k, cache_v, xp, state, u, vn, pool_w, pool_scale, sg_w, sg_bias)


ROUTE_E1, ROUTE_E2, ROUTE_G1, ROUTE_G2, ROUTE_R1, ROUTE_R2 = range(6)

SLAB_PITCH = 24


def _lane_pick(rec, k):
    lane = lax.broadcasted_iota(jnp.int32, rec.shape, 1)
    return jnp.sum(jnp.where(lane == k, rec, 0.0), axis=1, keepdims=True)


def _store_slabs(slab_ref, x):
    n, d = x.shape
    slab_ref[...] = jnp.zeros_like(slab_ref)
    for j in range(d // LANES):
        slab_ref[pl.ds(j, n, stride=SLAB_PITCH), :] = x[:, j * LANES:(j + 1) * LANES]


def _load_slab_chunk(slab_ref, j, n):
    return slab_ref[pl.ds(j, n, stride=SLAB_PITCH), :]


def _route(x1, counted, rwh_ref, rwl_ref, rb_ref, route_ref, cnt_ref, carry_ref):
    hi = x1.astype(BF16)
    lo = (x1 - hi.astype(F32)).astype(BF16)
    logits = (jnp.dot(hi, rwh_ref[...], preferred_element_type=F32)
              + jnp.dot(lo, rwh_ref[...], preferred_element_type=F32)
              + jnp.dot(hi, rwl_ref[...], preferred_element_type=F32) + rb_ref[...])
    lane_i = lax.broadcasted_iota(jnp.int32, logits.shape, 1)
    lane = lane_i.astype(F32)
    logits = jnp.where(lane_i < N_EXPERTS, logits, NEG_INF)
    m1 = jnp.max(logits, axis=1, keepdims=True)
    i1 = jnp.min(jnp.where(logits == m1, lane, float(LANES)), axis=1, keepdims=True)
    first = lane == i1
    rest = jnp.where(first, NEG_INF, logits)
    m2 = jnp.max(rest, axis=1, keepdims=True)
    i2 = jnp.min(jnp.where(rest == m2, lane, float(LANES)), axis=1, keepdims=True)
    second = lane == i2
    e = jnp.exp(m2 - m1)
    g1 = 1.0 / (1.0 + e)
    g2 = e * g1

    sel = (jnp.where(first, 1.0, 0.0) + jnp.where(second, 1.0, 0.0)) * counted
    tm = sel.shape[0]
    ri = lax.broadcasted_iota(jnp.int32, (tm, tm), 0)
    ci = lax.broadcasted_iota(jnp.int32, (tm, tm), 1)
    earlier = jnp.where(ri > ci, 1.0, 0.0).astype(BF16)
    ranks = jnp.dot(earlier, sel.astype(BF16), preferred_element_type=F32) + carry_ref[...]
    r1 = jnp.sum(jnp.where(first, ranks, 0.0), axis=1, keepdims=True)
    r2 = jnp.sum(jnp.where(second, ranks, 0.0), axis=1, keepdims=True)
    carry_ref[...] += jnp.sum(sel, axis=0, keepdims=True)
    cnt_ref[...] = carry_ref[...]

    rec = jnp.zeros_like(logits)
    for k, val in ((ROUTE_E1, i1), (ROUTE_E2, i2), (ROUTE_G1, g1), (ROUTE_G2, g2),
                   (ROUTE_R1, r1), (ROUTE_R2, r2)):
        rec = jnp.where(lane_i == k, val, rec)
    route_ref[...] = rec


N_MIX_REFS = 15


def _mix_out_kernel(*refs, alpha, with_router, n_cast, n_prompt_blocks, blocks_per_seq,
                    d_attn, d_pool, d_sg):
    mix_refs = refs[:N_MIX_REFS - 1]
    ms_ref, x_ref, w_ref, g_ref, b_ref = refs[N_MIX_REFS - 1:N_MIX_REFS + 4]
    rest = list(refs[N_MIX_REFS + 4:])
    if with_router:
        rwh_ref, rwl_ref, rb_ref = rest[:3]
        rest = rest[3:]
    cast_src, rest = rest[:n_cast], rest[n_cast:]
    if with_router:
        x1_ref, x1s_ref, route_ref, cnt_ref = rest[:4]
        rest = rest[4:]
    else:
        x1_ref, x1b_ref = rest[:2]
        rest = rest[2:]
    cast_dst, rest = rest[:n_cast], rest[n_cast:]
    if with_router:
        prev_ref, next_ref, carry_ref = rest
    else:
        prev_ref, next_ref = rest
    s = pl.program_id(0)
    _run_casts(cast_src, cast_dst)

    @pl.when(s == 0)
    def _():
        prev_ref[...] = jnp.zeros_like(prev_ref)
        if with_router:
            carry_ref[...] = jnp.zeros_like(carry_ref)

    def project():
        y = alpha * x_ref[...] + jnp.dot(prev_ref[...], w_ref[...], preferred_element_type=F32)
        x1 = _layer_norm(y, g_ref[...], b_ref[...])
        x1_ref[...] = x1
        if with_router:
            _store_slabs(x1s_ref, x1)
            counted = jnp.where(s > 0, 1.0, 0.0)
            _route(x1, counted, rwh_ref, rwl_ref, rb_ref, route_ref, cnt_ref, carry_ref)
        else:
            x1b_ref[...] = x1.astype(BF16)

    @pl.when(s < n_prompt_blocks)
    def _():
        project()
        _prompt_mixers(lax.rem(s, blocks_per_seq), *mix_refs, next_ref,
                       d_attn=d_attn, d_pool=d_pool, d_sg=d_sg)

    @pl.when(s >= n_prompt_blocks)
    def _():
        project()
        next_ref[...] = ms_ref[...]

    prev_ref[...] = next_ref[...]


def _mix_out(sinks, q, k, v, xp, u, vn, pool_w, pool_scale, sg_w, sg_bias, mixed_s, x, w, g, b,
             router=None, cast_jobs=(), *, layer, n_prompt, seq, alpha):
    r, d = x.shape
    d_attn, d_kv, d_pool, d_sg = q.shape[1], k.shape[1], xp.shape[1], u.shape[1]
    d_mix = d_attn + d_pool + d_sg
    assert n_prompt % BLK == 0 and (r - n_prompt) % BLK == 0 and mixed_s.shape[0] == r - n_prompt
    n_p = n_prompt // BLK
    n_s = (r - n_prompt) // BLK
    hist = 2 * SUBLANES
    pblk = lambda s: jnp.minimum(s, n_p - 1)
    cur = lambda s: (pblk(s), 0)
    prev = lambda s: (jnp.maximum(pblk(s) - 1, 0), 0)
    tail = lambda s: (jnp.maximum(pblk(s) * (BLK // hist) - 1, 0), 0)
    samp = lambda s: (jnp.clip(s - n_p, 0, n_s - 1), 0)
    row = lambda s: (jnp.maximum(s - 1, 0), 0)
    const2 = lambda s: (0, 0)
    const3 = lambda s: (0, 0, 0)
    in_specs = [
        pl.BlockSpec(memory_space=pltpu.SMEM),
        pl.BlockSpec((BLK, d_attn), cur),
        pl.BlockSpec((BLK, d_kv), cur),
        pl.BlockSpec((BLK, d_kv), prev),
        pl.BlockSpec((BLK, d_kv), cur),
        pl.BlockSpec((BLK, d_kv), prev),
        pl.BlockSpec((BLK, d_pool), cur),
        pl.BlockSpec((hist, d_pool), tail),
        pl.BlockSpec((BLK, d_sg), cur),
        pl.BlockSpec((BLK, d_sg), cur),
        pl.BlockSpec(pool_w.shape, const3),
        pl.BlockSpec((1, d_pool), const2),
        pl.BlockSpec(sg_w.shape, const3),
        pl.BlockSpec((BLK, d_sg), const2),
        pl.BlockSpec((BLK, d_mix), samp),
        pl.BlockSpec((BLK, d), row),
        pl.BlockSpec((None,) + w.shape[1:], lambda s: (layer, 0, 0), pipeline_mode=pl.Buffered(1)),
        pl.BlockSpec((1, d), const2),
        pl.BlockSpec((1, d), const2),
    ]
    args = [sinks, q, k, k, v, v, xp, xp, u, vn, pool_w, pool_scale, sg_w, sg_bias, mixed_s, x, w, g, b]
    assert len(args) == N_MIX_REFS + 4
    scratch = [pltpu.VMEM((BLK, d_mix), BF16), pltpu.VMEM((BLK, d_mix), BF16)]
    if router is None:
        outs = [jax.ShapeDtypeStruct((r, d), F32), jax.ShapeDtypeStruct((r, d), BF16)]
        out_specs = [pl.BlockSpec((BLK, d), row), pl.BlockSpec((BLK, d), row)]
    else:
        in_specs += [pl.BlockSpec((d, LANES), const2), pl.BlockSpec((d, LANES), const2),
                     pl.BlockSpec((1, LANES), const2)]
        args += list(router)
        outs = [jax.ShapeDtypeStruct((r, d), F32),
                jax.ShapeDtypeStruct((r * SLAB_PITCH, LANES), F32),
                jax.ShapeDtypeStruct((r, LANES), F32),
                jax.ShapeDtypeStruct((1, LANES), F32)]
        out_specs = [pl.BlockSpec((BLK, d), row),
                     pl.BlockSpec((BLK * SLAB_PITCH, LANES), row),
                     pl.BlockSpec((BLK, LANES), row),
                     pl.BlockSpec((1, LANES), const2)]
        scratch.append(pltpu.VMEM((1, LANES), F32))
    n_steps = r // BLK + 1
    c_in, c_shape, c_out, c_args = _cast_operands(cast_jobs, n_steps, lambda s: s)
    n_main = len(outs)
    res = pl.pallas_call(
        functools.partial(_mix_out_kernel, alpha=alpha, with_router=router is not None,
                          n_cast=len(cast_jobs), n_prompt_blocks=n_p, blocks_per_seq=seq // BLK,
                          d_attn=d_attn, d_pool=d_pool, d_sg=d_sg),
        out_shape=tuple(outs + c_shape),
        grid=(n_steps,),
        in_specs=in_specs + c_in,
        out_specs=tuple(out_specs + c_out),
        scratch_shapes=scratch,
        compiler_params=_cparams(1),
        name="mix_out_router" if router is not None else "mix_out",
    )(*args, *c_args)
    return res[:n_main], res[n_main:]


def _swiglu(xb, wg_ref, wu_ref):
    a = jnp.dot(xb, wg_ref[...], preferred_element_type=F32)
    c = jnp.dot(xb, wu_ref[...], preferred_element_type=F32)
    return (a * (1.0 / (1.0 + jnp.exp(-a))) * c).astype(BF16)


def _ffn_kernel(*refs, alpha, n_cast, with_zeros):
    xb_ref, x1_ref, wg_ref, wu_ref, wd_ref, g_ref, b_ref = refs[:7]
    cast_src = refs[7:7 + n_cast]
    o_ref = refs[7 + n_cast]
    cast_dst = refs[8 + n_cast:8 + 2 * n_cast]
    f = pl.program_id(1)
    if with_zeros:
        zero_ref = refs[8 + 2 * n_cast]
        zero_ref[...] = jnp.zeros_like(zero_ref)

    @pl.when(f == 0)
    def _():
        o_ref[...] = alpha * x1_ref[...]

    o_ref[...] += jnp.dot(_swiglu(xb_ref[...], wg_ref, wu_ref), wd_ref[...],
                          preferred_element_type=F32)

    @pl.when(f == pl.num_programs(1) - 1)
    def _():
        o_ref[...] = _layer_norm(o_ref[...], g_ref[...], b_ref[...])

    _run_casts(cast_src, cast_dst)


def _run_casts(src_refs, dst_refs):
    for src, dst in zip(src_refs, dst_refs):
        dst[...] = src[...].astype(BF16)


def _chunk_dst_map(*ids, step_of, last):
    return (jnp.minimum(step_of(*ids), last), 0)


def _chunk_src_map(*ids, step_of, layer, last):
    return (layer,) + _chunk_dst_map(*ids, step_of=step_of, last=last)


def _chunk_rows(rows, steps):
    for t in range(2 * SUBLANES, rows + 1, 2 * SUBLANES):
        if rows % t == 0 and rows // t <= steps:
            return t
    return rows


def _cast_operands(jobs, steps, step_of):
    in_specs, out_shape, out_specs, args = [], [], [], []
    for arr, li in jobs:
        _, rows, cols = arr.shape
        cr = _chunk_rows(rows, steps)
        last = rows // cr - 1
        in_specs.append(pl.BlockSpec(
            (None, cr, cols), functools.partial(_chunk_src_map, step_of=step_of, layer=li, last=last)))
        out_shape.append(jax.ShapeDtypeStruct((rows, cols), BF16))
        out_specs.append(pl.BlockSpec(
            (cr, cols), functools.partial(_chunk_dst_map, step_of=step_of, last=last)))
        args.append(arr)
    return in_specs, out_shape, out_specs, args


def _ffn(xb, x1, wg, wu, wd, g, b, cast_jobs=(), zero_rows=0, *, layer, tm, tf, alpha):
    r, d = x1.shape
    ff = wg.shape[2]
    nf = ff // tf
    steps = (r // tm) * nf
    step_of = lambda i, f: i * nf + f
    row = lambda i, f: (i, 0)
    const = lambda i, f: (0, 0)
    in_specs = [
        pl.BlockSpec((tm, d), row),
        pl.BlockSpec((tm, d), row),
        pl.BlockSpec((None, d, tf), lambda i, f: (layer, 0, f)),
        pl.BlockSpec((None, d, tf), lambda i, f: (layer, 0, f)),
        pl.BlockSpec((None, tf, d), lambda i, f: (layer, f, 0)),
        pl.BlockSpec((1, d), const),
        pl.BlockSpec((1, d), const),
    ]
    c_in, c_shape, c_out, c_args = _cast_operands(cast_jobs, steps, step_of)
    out_shape = [jax.ShapeDtypeStruct((r, d), F32)] + c_shape
    out_specs = [pl.BlockSpec((tm, d), row)] + c_out
    if zero_rows:
        zr = _chunk_rows(zero_rows, steps)
        out_shape.append(jax.ShapeDtypeStruct((zero_rows, LANES), F32))
        out_specs.append(pl.BlockSpec(
            (zr, LANES), functools.partial(_chunk_dst_map, step_of=step_of, last=zero_rows // zr - 1)))
    outs = pl.pallas_call(
        functools.partial(_ffn_kernel, alpha=alpha, n_cast=len(cast_jobs), with_zeros=bool(zero_rows)),
        out_shape=tuple(out_shape),
        grid=(r // tm, nf),
        in_specs=in_specs + c_in,
        out_specs=tuple(out_specs),
        compiler_params=_cparams(2),
        name="ffn",
    )(xb, x1, wg, wu, wd, g, b, *c_args)
    n = len(cast_jobs)
    return outs[0], outs[1:1 + n], (outs[1 + n] if zero_rows else None)


MOE_BLOCK = 512
DMA_RING = 128
DMA_UNROLL = 8


def _ring_depth(n):
    return 1 << (min(DMA_RING, n).bit_length() - 1)


def _routing_tables(route, counts, *, n_work):
    e1 = route[:, ROUTE_E1].astype(jnp.int32)
    e2 = route[:, ROUTE_E2].astype(jnp.int32)
    r1 = route[:, ROUTE_R1].astype(jnp.int32)
    r2 = route[:, ROUTE_R2].astype(jnp.int32)
    cnt = counts[0, :N_EXPERTS].astype(jnp.int32)
    nblk = (cnt + MOE_BLOCK - 1) // MOE_BLOCK
    blk_end = jnp.cumsum(nblk)
    start = (blk_end - nblk) * MOE_BLOCK
    slot1 = jnp.take(start, e1) + r1
    slot2 = jnp.take(start, e2) + r2
    n_active = blk_end[-1]
    w = jnp.arange(n_work, dtype=jnp.int32)
    blk = jnp.minimum(w, n_active - 1)
    work_exp = jnp.sum(blk[:, None] >= blk_end[None, :], axis=1).astype(jnp.int32)
    first_blk = jnp.take(blk_end - nblk, work_exp)
    work_rows = jnp.clip(jnp.take(cnt, work_exp) - (w - first_blk) * MOE_BLOCK, 0, MOE_BLOCK)
    work_rows = jnp.where(w < n_active, work_rows, 0).astype(jnp.int32)
    return slot1, slot2, work_exp, work_rows


def _ring_copies(n, ring, make_copies):
    def start(i):
        for c in make_copies(i):
            c.start()

    def wait(i):
        for c in make_copies(i):
            c.wait()

    def fill(i, carry):
        start(i)
        return carry

    def steady(i, carry):
        wait(i - ring)
        start(i)
        return carry

    def drain(i, carry):
        wait(i)
        return carry

    lax.fori_loop(0, ring, fill, 0, unroll=DMA_UNROLL)
    lax.fori_loop(ring, n, steady, 0, unroll=DMA_UNROLL)
    lax.fori_loop(n - ring, n, drain, 0, unroll=DMA_UNROLL)


SLAB_ROWS = 16


def _slab(ref, token):
    return ref.at[pl.ds(pl.multiple_of(token * SLAB_PITCH, SUBLANES), SLAB_ROWS), :]


def _dispatch_kernel(s1_ref, s2_ref, x1s_ref, xs_init, xs_hbm, sem):
    del xs_init
    tm = x1s_ref.shape[0] // SLAB_PITCH
    base = pl.program_id(0) * tm
    ring = sem.shape[1]

    def copies(r):
        k = r & (ring - 1)
        src = _slab(x1s_ref, r)
        return (pltpu.make_async_copy(src, _slab(xs_hbm, s1_ref[base + r]), sem.at[0, k]),
                pltpu.make_async_copy(src, _slab(xs_hbm, s2_ref[base + r]), sem.at[1, k]))

    _ring_copies(tm, ring, copies)


def _moe_dispatch(slot1, slot2, x1s, xs_init, *, tm):
    r = x1s.shape[0] // SLAB_PITCH
    return pl.pallas_call(
        _dispatch_kernel,
        out_shape=jax.ShapeDtypeStruct(xs_init.shape, F32),
        grid_spec=pltpu.PrefetchScalarGridSpec(
            num_scalar_prefetch=2,
            grid=(r // tm,),
            in_specs=[pl.BlockSpec((tm * SLAB_PITCH, LANES), lambda i, s1, s2: (i, 0)),
                      pl.BlockSpec(memory_space=pl.ANY)],
            out_specs=pl.BlockSpec(memory_space=pl.ANY),
            scratch_shapes=[pltpu.SemaphoreType.DMA((2, _ring_depth(tm)))],
        ),
        input_output_aliases={3: 0},
        compiler_params=_cparams(1),
        name="moe_dispatch",
    )(slot1, slot2, x1s, xs_init)


def _moe_ffn_kernel(we_ref, wr_ref, xs_ref, wg_ref, wu_ref, wd_ref, ys_ref, xb_ref, acc_ref):
    del we_ref
    w = pl.program_id(0)
    f = pl.program_id(1)
    n, d = acc_ref.shape
    active = wr_ref[w] > 0

    def partial_out():
        return jnp.dot(_swiglu(xb_ref[...], wg_ref, wu_ref), wd_ref[...],
                       preferred_element_type=F32)

    @pl.when(active & (f == 0))
    def _():
        for j in range(d // LANES):
            xb_ref[:, j * LANES:(j + 1) * LANES] = _load_slab_chunk(xs_ref, j, n).astype(BF16)
        acc_ref[...] = partial_out()

    @pl.when(active & (f > 0))
    def _():
        acc_ref[...] += partial_out()

    @pl.when(jnp.logical_not(active) & (f == 0))
    def _():
        acc_ref[...] = jnp.zeros_like(acc_ref)

    @pl.when(f == pl.num_programs(1) - 1)
    def _():
        _store_slabs(ys_ref, acc_ref[...])


def _moe_ffn(work_exp, work_rows, xs, wg, wu, wd, *, layer, tf):
    d = wg.shape[2]
    ff = wg.shape[3]
    nf = ff // tf
    rows = lambda w, f, we, wr: (w, 0)
    chunk = lambda w, f, wr: jnp.where(wr[w] > 0, f, nf - 1)
    return pl.pallas_call(
        _moe_ffn_kernel,
        out_shape=jax.ShapeDtypeStruct(xs.shape, F32),
        grid_spec=pltpu.PrefetchScalarGridSpec(
            num_scalar_prefetch=2,
            grid=(xs.shape[0] // (MOE_BLOCK * SLAB_PITCH), nf),
            in_specs=[
                pl.BlockSpec((MOE_BLOCK * SLAB_PITCH, LANES), rows),
                pl.BlockSpec((None, None, d, tf),
                             lambda w, f, we, wr: (layer, we[w], 0, chunk(w, f, wr))),
                pl.BlockSpec((None, None, d, tf),
                             lambda w, f, we, wr: (layer, we[w], 0, chunk(w, f, wr))),
                pl.BlockSpec((None, None, tf, d),
                             lambda w, f, we, wr: (layer, we[w], chunk(w, f, wr), 0)),
            ],
            out_specs=pl.BlockSpec((MOE_BLOCK * SLAB_PITCH, LANES), rows),
            scratch_shapes=[pltpu.VMEM((MOE_BLOCK, d), BF16), pltpu.VMEM((MOE_BLOCK, d), F32)],
        ),
        compiler_params=_cparams(2),
        name="moe_ffn",
    )(work_exp, work_rows, xs, wg, wu, wd)


def _combine_kernel(s1_ref, s2_ref, ys_hbm, x1_ref, route_ref, g_ref, b_ref, o_ref,
                    y1_ref, y2_ref, sem, *, alpha):
    tm, d = o_ref.shape
    base = pl.program_id(0) * tm
    ring = sem.shape[1]

    def copies(r):
        k = r & (ring - 1)
        return (pltpu.make_async_copy(_slab(ys_hbm, s1_ref[base + r]), _slab(y1_ref, r), sem.at[0, k]),
                pltpu.make_async_copy(_slab(ys_hbm, s2_ref[base + r]), _slab(y2_ref, r), sem.at[1, k]))

    _ring_copies(tm, ring, copies)
    route = route_ref[...]
    g1 = _lane_pick(route, ROUTE_G1)
    g2 = _lane_pick(route, ROUTE_G2)
    for j in range(d // LANES):
        sl = slice(j * LANES, (j + 1) * LANES)
        o_ref[:, sl] = (alpha * x1_ref[:, sl] + g1 * _load_slab_chunk(y1_ref, j, tm)
                        + g2 * _load_slab_chunk(y2_ref, j, tm))
    o_ref[...] = _layer_norm(o_ref[...], g_ref[...], b_ref[...])


def _moe_combine(slot1, slot2, ys, x1, route, g, b, *, tm, alpha):
    r, d = x1.shape
    row = lambda i, s1, s2: (i, 0)
    const = lambda i, s1, s2: (0, 0)
    return pl.pallas_call(
        functools.partial(_combine_kernel, alpha=alpha),
        out_shape=jax.ShapeDtypeStruct((r, d), F32),
        grid_spec=pltpu.PrefetchScalarGridSpec(
            num_scalar_prefetch=2,
            grid=(r // tm,),
            in_specs=[
                pl.BlockSpec(memory_space=pl.ANY),
                pl.BlockSpec((tm, d), row),
                pl.BlockSpec((tm, LANES), row),
                pl.BlockSpec((1, d), const),
                pl.BlockSpec((1, d), const),
            ],
            out_specs=pl.BlockSpec((tm, d), row),
            scratch_shapes=[pltpu.VMEM((tm * SLAB_PITCH, LANES), F32),
                            pltpu.VMEM((tm * SLAB_PITCH, LANES), F32),
                            pltpu.SemaphoreType.DMA((2, _ring_depth(tm)))],
        ),
        compiler_params=_cparams(1),
        name="moe_combine",
    )(slot1, slot2, ys, x1, route, g, b)


def _moe_work_items(r):
    return (2 * r + MOE_BLOCK - 1) // MOE_BLOCK + N_EXPERTS


def _moe(x1, x1s, route, counts, wg, wu, wd, g, b, xs_buf=None, *, layer, tm, tf, alpha):
    r = x1.shape[0]
    n_work = _moe_work_items(r)
    slot1, slot2, work_exp, work_rows = _routing_tables(route, counts, n_work=n_work)
    if xs_buf is None:
        xs_buf = jnp.zeros((n_work * MOE_BLOCK * SLAB_PITCH, LANES), F32)
    xs = _moe_dispatch(slot1, slot2, x1s, xs_buf, tm=tm)
    ys = _moe_ffn(work_exp, work_rows, xs, wg, wu, wd, layer=layer, tf=tf)
    return _moe_combine(slot1, slot2, ys, x1, route, g, b, tm=tm, alpha=alpha), xs


def _rope_tables(pos):
    half = HEAD_DIM // 2
    inv = ROPE_THETA ** (-jnp.arange(half, dtype=F32) / half)
    ang = pos.astype(F32)[:, None] * inv[None, :]
    cos = jnp.cos(ang)
    sin = jnp.sin(ang)
    cos2 = jnp.concatenate([cos, cos, cos, cos], axis=1)
    sin2 = jnp.concatenate([-sin, sin, -sin, sin], axis=1)
    return cos2, sin2


def _row_tile(r, cap):
    best = 16
    for t in range(16, cap + 1, 16):
        if r % t == 0:
            best = t
    return best


def kernel(x_prompt, x_sample, cache_k, cache_v, state_pool, w_in, w_out, attn_sinks, pool_w,
           pool_scale, sg_w, sg_b, sg_norm_g, sg_norm_b, ln1_g, ln1_b, ln2_g, ln2_b,
           ffn_w_gate, ffn_w_up, ffn_w_down, router_w, router_b, moe_w_gate, moe_w_up, moe_w_down):
    bp, tp, d = x_prompt.shape
    bs, ts, _ = x_sample.shape
    depth = w_in.shape[0]
    d_kv = cache_k.shape[3] * cache_k.shape[4]
    d_pool = pool_scale.shape[1]
    d_sg = sg_norm_g.shape[1]
    d_attn = w_in.shape[2] - 2 * d_kv - d_pool - 2 * d_sg
    win_buf = cache_k.shape[2]
    assert win_buf == WINDOW == BLK and tp % BLK == 0 and tp >= BLK and ts <= SAMPLE_PAD
    assert bs % SEQ_PER_STEP == 0 and state_pool.shape[2] == POOL_STATE
    assert cache_k.shape[3] == N_KV_HEADS and cache_k.shape[4] == HEAD_DIM
    assert d == SLAB_ROWS * LANES
    alpha = (2 * depth) ** 0.25

    r_p = bp * tp
    r_s = bs * SAMPLE_PAD
    r = r_p + r_s
    tm = _row_tile(r, 768)
    tm_ffn = _row_tile(r, 704)
    tm_comb = _row_tile(r, 384)

    xs_pad = jnp.pad(x_sample, ((0, 0), (0, SAMPLE_PAD - ts), (0, 0)))
    x = jnp.concatenate([x_prompt.reshape(r_p, d), xs_pad.reshape(r_s, d)], axis=0)

    pos = jnp.concatenate([
        jnp.tile(jnp.arange(tp, dtype=jnp.int32), bp),
        jnp.tile(PAST_LEN + jnp.arange(SAMPLE_PAD, dtype=jnp.int32), bs)])
    cos, sin = _rope_tables(pos)

    in_w = {0: w_in[0].astype(BF16)}
    out_w = {0: w_out[0].astype(BF16)}
    pool_w_b = pool_w.astype(BF16)
    rw_pad = jnp.pad(router_w, ((0, 0), (0, 0), (0, LANES - router_w.shape[2])))
    rw_hi = rw_pad.astype(BF16)
    rw_lo = (rw_pad - rw_hi.astype(F32)).astype(BF16)
    rb_pad = jnp.pad(router_b, ((0, 0), (0, LANES - router_b.shape[1])))[:, None, :]
    sg_bias = jnp.repeat(jnp.swapaxes(sg_b, 1, 2), d_sg // N_SG_GROUPS, axis=2)
    state_pad = jnp.pad(state_pool, ((0, 0), (0, 0), (1, 0), (0, 0)))
    state_rows = state_pad.reshape((depth * bs,) + state_pad.shape[2:])
    ck_rows = cache_k.reshape(depth * bs * win_buf, d_kv)
    cv_rows = cache_v.reshape(depth * bs * win_buf, d_kv)

    k_l, v_l, xp_l, vn_l = [], [], [], []
    xs_buf = None
    for l in range(depth):
        q, k, v, xp, u, vn = _inproj(
            x, in_w[l][None], cos, sin, sg_norm_g[l][None], sg_norm_b[l][None],
            layer=0, tm=tm, d_attn=d_attn, d_kv=d_kv, d_pool=d_pool, d_sg=d_sg)
        mixed_s = _mix_sample(attn_sinks[l], q, k, v, xp, u, vn, ck_rows, cv_rows, state_rows,
                              pool_w_b[l], pool_scale[l][None], sg_w[l], sg_bias[l],
                              layer=l, r_prompt=r_p, n_seq=bs)
        i = l // 2
        is_dense = l % 2 == 0
        jobs = []
        if l + 1 < depth:
            jobs += [(w_in, l + 1), (w_out, l + 1)]
        own = (ffn_w_gate, ffn_w_up, ffn_w_down) if is_dense else (moe_w_gate, moe_w_up, moe_w_down)
        jobs += [(w.reshape(w.shape[0], -1, w.shape[-1]), i) for w in own]
        main, casted = _mix_out(
            attn_sinks[l], q, k, v, xp, u, vn, pool_w_b[l], pool_scale[l][None], sg_w[l],
            sg_bias[l], mixed_s, x, out_w[l][None], ln1_g[l][None], ln1_b[l][None],
            None if is_dense else (rw_hi[i], rw_lo[i], rb_pad[i]), tuple(jobs),
            layer=0, n_prompt=r_p, seq=tp, alpha=alpha)
        casted = list(casted)
        if l + 1 < depth:
            in_w[l + 1], out_w[l + 1] = casted[:2]
            casted = casted[2:]
        if is_dense:
            x1, x1b = main
            ffn_g, ffn_u, ffn_d = (c[None] for c in casted)
            zero_rows = 0
            if l + 1 < depth and xs_buf is None:
                zero_rows = _moe_work_items(r) * MOE_BLOCK * SLAB_PITCH
            x, _, zeros = _ffn(x1b, x1, ffn_g, ffn_u, ffn_d, ln2_g[l][None], ln2_b[l][None],
                               (), zero_rows, layer=0, tm=tm_ffn, tf=512, alpha=alpha)
            if zero_rows:
                xs_buf = zeros
        else:
            x1, x1s, route, counts = main
            moe_g, moe_u, moe_d = (c.reshape((1,) + w.shape[1:]) for c, w in zip(casted, own))
            x, xs_buf = _moe(x1, x1s, route, counts, moe_g, moe_u, moe_d, ln2_g[l][None],
                             ln2_b[l][None], xs_buf, layer=0, tm=tm_comb, tf=512, alpha=alpha)

        tail = max(WINDOW, POOL_STATE)
        keep = lambda a: jnp.concatenate(
            [a[:r_p].reshape(bp, tp, -1)[:, -tail:].reshape(bp * tail, -1), a[r_p:]], axis=0)
        k_l.append(keep(k))
        v_l.append(keep(v))
        xp_l.append(keep(xp))
        vn_l.append(vn[r_p:])

    kh = (N_KV_HEADS, HEAD_DIM)
    n_tail = bp * max(WINDOW, POOL_STATE)
    ks, vs, xps, vns = (jnp.stack(a) for a in (k_l, v_l, xp_l, vn_l))
    prompt_tail = lambda a, n: a[:, :n_tail].reshape(depth, bp, -1, a.shape[-1])[:, :, -n:]
    sample_new = lambda a: a[:, -r_s:].reshape(depth, bs, SAMPLE_PAD, a.shape[-1])[:, :, :ts]
    new_k_p = prompt_tail(ks, WINDOW).reshape(depth, bp, WINDOW, *kh)
    new_v_p = prompt_tail(vs, WINDOW).reshape(depth, bp, WINDOW, *kh)
    new_pool_p = prompt_tail(xps, POOL_STATE)
    new_k_s = jnp.concatenate(
        [cache_k, sample_new(ks).reshape(depth, bs, ts, *kh)], axis=2)[:, :, -win_buf:]
    new_v_s = jnp.concatenate(
        [cache_v, sample_new(vs).reshape(depth, bs, ts, *kh)], axis=2)[:, :, -win_buf:]
    new_pool_s = jnp.concatenate([state_pool, sample_new(xps)], axis=2)[:, :, -POOL_STATE:]
    new_sg_s = vns.reshape(depth, bs, SAMPLE_PAD, d_sg)[:, :, :ts]

    y_prompt = x[:r_p].reshape(bp, tp, d)
    y_sample = x[r_p:].reshape(bs, SAMPLE_PAD, d)[:, :ts]
    return (y_prompt, y_sample, new_k_p, new_v_p, new_pool_p, new_k_s, new_v_s, new_pool_s, new_sg_s)
```

```python
import functools
import math

import jax
import jax.numpy as jnp
from jax import lax
from jax.experimental import pallas as pl
from jax.experimental.pallas import tpu as pltpu

F32 = jnp.float32
BF16 = jnp.bfloat16

PAST_LEN = 16384
HEAD_DIM = 64
N_KV_HEADS = 2
WINDOW = 128
BLK = 128
ROPE_THETA = 10000.0
POOL_WINDOWS = (2, 4, 8, 16)
POOL_STATE = 15
N_SG_GROUPS = 4
N_EXPERTS = 8
LN_EPS = 1e-5

LANES = 128
SUBLANES = 8
SAMPLE_PAD = SUBLANES
VMEM_LIMIT = 56 * 1024 * 1024

NEG_INF = float("-inf")


def _cparams(n_axes):
    return pltpu.CompilerParams(
        dimension_semantics=("arbitrary",) * n_axes, vmem_limit_bytes=VMEM_LIMIT)


def _layer_norm(x, g, b):
    mu = jnp.mean(x, axis=-1, keepdims=True)
    xc = x - mu
    var = jnp.mean(xc * xc, axis=-1, keepdims=True)
    return xc * lax.rsqrt(var + LN_EPS) * g + b


def _gelu_tanh(x):
    return 0.5 * x * (1.0 + jnp.tanh(0.7978845608028654 * (x + 0.044715 * (x * x * x))))


def _inproj_kernel(x_ref, w_ref, cos_ref, sin_ref, g_ref, b_ref,
                   q_ref, k_ref, v_ref, xp_ref, u_ref, vn_ref, *, d_attn, d_kv, d_pool, d_sg):
    xb = x_ref[...].astype(BF16)
    cos = cos_ref[...]
    sin = sin_ref[...]
    tm = xb.shape[0]
    lane = lax.broadcasted_iota(jnp.int32, (tm, LANES), 1)
    first_half = (lane % HEAD_DIM) < (HEAD_DIM // 2)

    def rope(z):
        rot = jnp.where(first_half, pltpu.roll(z, LANES - HEAD_DIM // 2, 1),
                        pltpu.roll(z, HEAD_DIM // 2, 1))
        return z * cos + rot * sin

    c0 = 0
    zq = jnp.dot(xb, w_ref[:, c0:c0 + d_attn], preferred_element_type=F32)
    for j in range(d_attn // LANES):
        sl = slice(j * LANES, (j + 1) * LANES)
        q_ref[:, sl] = (rope(zq[:, sl]) * (HEAD_DIM ** -0.5)).astype(BF16)
    c0 += d_attn
    zkv = jnp.dot(xb, w_ref[:, c0:c0 + 2 * d_kv], preferred_element_type=F32)
    k_ref[...] = rope(zkv[:, :d_kv])
    v_ref[...] = zkv[:, d_kv:]
    c0 += 2 * d_kv
    xp_ref[...] = jnp.dot(xb, w_ref[:, c0:c0 + d_pool], preferred_element_type=F32)
    c0 += d_pool
    u_ref[...] = _gelu_tanh(jnp.dot(xb, w_ref[:, c0:c0 + d_sg], preferred_element_type=F32))
    c0 += d_sg
    vg = _gelu_tanh(jnp.dot(xb, w_ref[:, c0:c0 + d_sg], preferred_element_type=F32))
    gd = d_sg // N_SG_GROUPS
    for g in range(N_SG_GROUPS):
        sl = slice(g * gd, (g + 1) * gd)
        vn_ref[:, sl] = _layer_norm(vg[:, sl], g_ref[:, sl], b_ref[:, sl])


def _inproj(x, w, cos, sin, sg_g, sg_b, *, layer, tm, d_attn, d_kv, d_pool, d_sg):
    r, d = x.shape
    d_in = w.shape[2]
    row = lambda i: (i, 0)
    const = lambda i: (0, 0)
    outs = (
        jax.ShapeDtypeStruct((r, d_attn), BF16),
        jax.ShapeDtypeStruct((r, d_kv), F32),
        jax.ShapeDtypeStruct((r, d_kv), F32),
        jax.ShapeDtypeStruct((r, d_pool), F32),
        jax.ShapeDtypeStruct((r, d_sg), F32),
        jax.ShapeDtypeStruct((r, d_sg), F32),
    )
    return pl.pallas_call(
        functools.partial(_inproj_kernel, d_attn=d_attn, d_kv=d_kv, d_pool=d_pool, d_sg=d_sg),
        out_shape=outs,
        grid=(r // tm,),
        in_specs=[
            pl.BlockSpec((tm, d), row),
            pl.BlockSpec((None, d, d_in), lambda i: (layer, 0, 0), pipeline_mode=pl.Buffered(1)),
            pl.BlockSpec((tm, LANES), row),
            pl.BlockSpec((tm, LANES), row),
            pl.BlockSpec((1, d_sg), const),
            pl.BlockSpec((1, d_sg), const),
        ],
        out_specs=(
            pl.BlockSpec((tm, d_attn), row),
            pl.BlockSpec((tm, d_kv), row),
            pl.BlockSpec((tm, d_kv), row),
            pl.BlockSpec((tm, d_pool), row),
            pl.BlockSpec((tm, d_sg), row),
            pl.BlockSpec((tm, d_sg), row),
        ),
        compiler_params=_cparams(1),
        name="inproj",
    )(x, w, cos, sin, sg_g, sg_b)


def _split_kv(kv):
    lane = lax.broadcasted_iota(jnp.int32, kv.shape, 1)
    low = lane < HEAD_DIM
    a = kv.astype(BF16)
    b = pltpu.roll(kv, HEAD_DIM, 1).astype(BF16)
    zero = jnp.zeros_like(a)
    head0 = (jnp.where(low, a, zero), jnp.where(low, zero, b))
    head1 = (jnp.where(low, b, zero), jnp.where(low, zero, a))
    return head0, head1


def _attend(q, k_pair, v_pair, allowed, sinks):
    out = None
    for kx, vx, sk in zip(k_pair, v_pair, sinks):
        s = lax.dot_general(q, kx, (((1,), (1,)), ((), ())), preferred_element_type=F32)
        s = jnp.where(allowed, s, NEG_INF)
        m = jnp.maximum(jnp.max(s, axis=1, keepdims=True), sk)
        p = jnp.exp(s - m)
        denom = jnp.sum(p, axis=1, keepdims=True) + jnp.exp(sk - m)
        o = jnp.dot(p.astype(BF16), vx, preferred_element_type=F32) * (1.0 / denom)
        out = o if out is None else out + o
    return out


def _window_sums(full):
    sums = {1: full}
    w = 1
    while w < max(POOL_WINDOWS):
        sums[2 * w] = sums[w] + pltpu.roll(sums[w], w, 0)
        w *= 2
    return sums


def _pool_mix(full, cur, cnt_of, pw_ref, ps_ref, row0):
    n = cur.shape[0]
    gd = cur.shape[1] // len(POOL_WINDOWS)
    outs = []
    for g, w in enumerate(POOL_WINDOWS):
        sl = slice(g * gd, (g + 1) * gd)
        sums = _window_sums(full[:, sl])[w]
        pooled = sums[row0:row0 + n, :] / cnt_of(w)
        d = (pooled - cur[:, sl]).astype(BF16)
        outs.append(jnp.dot(d, pw_ref[g], preferred_element_type=F32) * ps_ref[:, sl])
    return outs


def _causal_weights(sw_ref, g):
    n = sw_ref.shape[1]
    ri = lax.broadcasted_iota(jnp.int32, (n, n), 0)
    ci = lax.broadcasted_iota(jnp.int32, (n, n), 1)
    return jnp.where(ri >= ci, sw_ref[g], 0.0).astype(BF16)


def _prompt_mixers(i, sink_ref, q_ref, kc_ref, kp_ref, vc_ref, vp_ref, xc_ref, xt_ref, u_ref,
                   vn_ref, pw_ref, ps_ref, sw_ref, sb_ref, o_ref, *, d_attn, d_pool, d_sg):
    has_prev = i > 0

    kall = jnp.concatenate([kp_ref[...], kc_ref[...]], axis=0)
    vall = jnp.concatenate([vp_ref[...], vc_ref[...]], axis=0)
    k_heads = _split_kv(kall)
    v_heads = _split_kv(vall)
    ri = lax.broadcasted_iota(jnp.int32, (BLK, 2 * BLK), 0)
    cj = lax.broadcasted_iota(jnp.int32, (BLK, 2 * BLK), 1)
    first_key = jnp.where(has_prev, 0, BLK)
    allowed = (cj >= ri) & (cj <= ri + WINDOW) & (cj >= first_key)
    n_pairs = d_attn // LANES
    pairs_per_kv = n_pairs // N_KV_HEADS
    for p in range(n_pairs):
        g = p // pairs_per_kv
        sl = slice(p * LANES, (p + 1) * LANES)
        o = _attend(q_ref[:, sl], k_heads[g], v_heads[g], allowed,
                    (sink_ref[2 * p], sink_ref[2 * p + 1]))
        o_ref[:, sl] = o.astype(BF16)

    xc = xc_ref[...]
    tail = jnp.where(has_prev, xt_ref[...], 0.0)
    full = jnp.concatenate([tail, xc], axis=0)
    hist = tail.shape[0]
    pos = i * BLK + lax.broadcasted_iota(jnp.int32, (BLK, 1), 0)
    cnt_of = lambda w: jnp.minimum(pos + 1, w).astype(F32)
    pooled = _pool_mix(full, xc, cnt_of, pw_ref, ps_ref, hist)
    gd = d_pool // len(POOL_WINDOWS)
    for g, y in enumerate(pooled):
        o_ref[:, d_attn + g * gd:d_attn + (g + 1) * gd] = y.astype(BF16)

    gs = d_sg // N_SG_GROUPS
    for g in range(N_SG_GROUPS):
        sl = slice(g * gs, (g + 1) * gs)
        s = jnp.dot(_causal_weights(sw_ref, g), vn_ref[:, sl].astype(BF16),
                    preferred_element_type=F32) + sb_ref[:, sl]
        c0 = d_attn + d_pool + g * gs
        o_ref[:, c0:c0 + gs] = (u_ref[:, sl] * s).astype(BF16)


SEQ_PER_STEP = 2


def _mix_sample_kernel(sink_ref, q_ref, kn_ref, vn_new_ref, ck_ref, cv_ref, xn_ref, st_ref,
                       u_ref, vn_ref, pw_ref, ps_ref, sw_ref, sb_ref, o_ref, *, d_attn, d_pool, d_sg):
    sp = SAMPLE_PAD
    n_pairs = d_attn // LANES
    pairs_per_kv = n_pairs // N_KV_HEADS
    m_rows = pairs_per_kv * sp
    qf = q_ref[...].astype(F32)

    ri = lax.broadcasted_iota(jnp.int32, (m_rows, 2 * BLK), 0) % sp
    cj = lax.broadcasted_iota(jnp.int32, (m_rows, 2 * BLK), 1)
    allowed = (cj >= ri) & (cj <= ri + WINDOW)
    row_pair = lax.broadcasted_iota(jnp.int32, (m_rows, 1), 0) // sp

    attn_rows = []
    for s in range(SEQ_PER_STEP):
        rows = slice(s * sp, (s + 1) * sp)
        zpad = jnp.zeros((BLK - sp, kn_ref.shape[1]), F32)
        kall = jnp.concatenate([ck_ref[s * BLK:(s + 1) * BLK, :], kn_ref[rows, :], zpad], axis=0)
        vall = jnp.concatenate([cv_ref[s * BLK:(s + 1) * BLK, :], vn_new_ref[rows, :], zpad], axis=0)
        k_heads = _split_kv(kall)
        v_heads = _split_kv(vall)
        per_pair = []
        for g in range(N_KV_HEADS):
            qst = jnp.concatenate(
                [qf[rows, (g * pairs_per_kv + pl_) * LANES:(g * pairs_per_kv + pl_ + 1) * LANES]
                 for pl_ in range(pairs_per_kv)], axis=0).astype(BF16)
            sink_lo = jnp.zeros((m_rows, 1), F32)
            sink_hi = jnp.zeros((m_rows, 1), F32)
            for pl_ in range(pairs_per_kv):
                h = 2 * (g * pairs_per_kv + pl_)
                sink_lo = jnp.where(row_pair == pl_, sink_ref[h], sink_lo)
                sink_hi = jnp.where(row_pair == pl_, sink_ref[h + 1], sink_hi)
            o = _attend(qst, k_heads[g], v_heads[g], allowed, (sink_lo, sink_hi))
            per_pair.extend(o[pl_ * sp:(pl_ + 1) * sp, :] for pl_ in range(pairs_per_kv))
        attn_rows.append(per_pair)
    for p in range(n_pairs):
        o_ref[:, p * LANES:(p + 1) * LANES] = jnp.concatenate(
            [attn_rows[s][p] for s in range(SEQ_PER_STEP)], axis=0).astype(BF16)

    gd = d_pool // len(POOL_WINDOWS)
    pooled = []
    for s in range(SEQ_PER_STEP):
        rows = slice(s * sp, (s + 1) * sp)
        xc = xn_ref[rows, :]
        full = jnp.concatenate([st_ref[s], xc], axis=0)
        cnt_of = lambda w: float(w)
        pooled.append(_pool_mix(full, xc, cnt_of, pw_ref, ps_ref, st_ref.shape[1]))
    for g in range(len(POOL_WINDOWS)):
        o_ref[:, d_attn + g * gd:d_attn + (g + 1) * gd] = jnp.concatenate(
            [pooled[s][g] for s in range(SEQ_PER_STEP)], axis=0).astype(BF16)

    gs = d_sg // N_SG_GROUPS
    for g in range(N_SG_GROUPS):
        sl = slice(g * gs, (g + 1) * gs)
        wm = _causal_weights(sw_ref, g)
        outs = []
        for s in range(SEQ_PER_STEP):
            rows = slice(s * sp, (s + 1) * sp)
            vpad = jnp.concatenate([vn_ref[rows, sl], jnp.zeros((BLK - sp, gs), F32)], axis=0)
            sg = jnp.dot(wm, vpad.astype(BF16), preferred_element_type=F32)[:sp, :] + sb_ref[:sp, sl]
            outs.append(u_ref[rows, sl] * sg)
        c0 = d_attn + d_pool + g * gs
        o_ref[:, c0:c0 + gs] = jnp.concatenate(outs, axis=0).astype(BF16)


def _mix_sample(sinks, q, k, v, xp, u, vn, cache_k, cache_v, state, pool_w, pool_scale,
                sg_w, sg_bias, *, layer, r_prompt, n_seq):
    d_attn, d_kv, d_pool, d_sg = q.shape[1], k.shape[1], xp.shape[1], u.shape[1]
    d_mix = d_attn + d_pool + d_sg
    rows = SEQ_PER_STEP * SAMPLE_PAD
    base = r_prompt // rows
    n_steps = n_seq // SEQ_PER_STEP
    new = lambda i: (base + i, 0)
    per_seq = lambda i: (i, 0)
    carried = lambda i: (layer * n_steps + i, 0)
    const2 = lambda i: (0, 0)
    const3 = lambda i: (0, 0, 0)
    return pl.pallas_call(
        functools.partial(_mix_sample_kernel, d_attn=d_attn, d_pool=d_pool, d_sg=d_sg),
        out_shape=jax.ShapeDtypeStruct((n_seq * SAMPLE_PAD, d_mix), BF16),
        grid=(n_seq // SEQ_PER_STEP,),
        in_specs=[
            pl.BlockSpec(memory_space=pltpu.SMEM),
            pl.BlockSpec((rows, d_attn), new),
            pl.BlockSpec((rows, d_kv), new),
            pl.BlockSpec((rows, d_kv), new),
            pl.BlockSpec((SEQ_PER_STEP * BLK, d_kv), carried),
            pl.BlockSpec((SEQ_PER_STEP * BLK, d_kv), carried),
            pl.BlockSpec((rows, d_pool), new),
            pl.BlockSpec((SEQ_PER_STEP,) + state.shape[1:], lambda i: (layer * n_steps + i, 0, 0)),
            pl.BlockSpec((rows, d_sg), new),
            pl.BlockSpec((rows, d_sg), new),
            pl.BlockSpec(pool_w.shape, const3),
            pl.BlockSpec((1, d_pool), const2),
            pl.BlockSpec(sg_w.shape, const3),
            pl.BlockSpec((BLK, d_sg), const2),
        ],
        out_specs=pl.BlockSpec((rows, d_mix), per_seq),
        compiler_params=_cparams(1),
        name="mix_sample",
    )(sinks, q, k, v, cache_k, cache_v, xp, state, u, vn, pool_w, pool_scale, sg_w, sg_bias)


ROUTE_E1, ROUTE_E2, ROUTE_G1, ROUTE_G2, ROUTE_R1, ROUTE_R2 = range(6)

SLAB_PITCH = 24


def _lane_pick(rec, k):
    lane = lax.broadcasted_iota(jnp.int32, rec.shape, 1)
    return jnp.sum(jnp.where(lane == k, rec, 0.0), axis=1, keepdims=True)


def _store_slabs(slab_ref, x):
    n, d = x.shape
    slab_ref[...] = jnp.zeros_like(slab_ref)
    for j in range(d // LANES):
        slab_ref[pl.ds(j, n, stride=SLAB_PITCH), :] = x[:, j * LANES:(j + 1) * LANES]


def _load_slab_chunk(slab_ref, j, n):
    return slab_ref[pl.ds(j, n, stride=SLAB_PITCH), :]


def _route(x1, counted, rwh_ref, rwl_ref, rb_ref, route_ref, cnt_ref, carry_ref):
    hi = x1.astype(BF16)
    lo = (x1 - hi.astype(F32)).astype(BF16)
    logits = (jnp.dot(hi, rwh_ref[...], preferred_element_type=F32)
              + jnp.dot(lo, rwh_ref[...], preferred_element_type=F32)
              + jnp.dot(hi, rwl_ref[...], preferred_element_type=F32) + rb_ref[...])
    lane_i = lax.broadcasted_iota(jnp.int32, logits.shape, 1)
    lane = lane_i.astype(F32)
    logits = jnp.where(lane_i < N_EXPERTS, logits, NEG_INF)
    m1 = jnp.max(logits, axis=1, keepdims=True)
    i1 = jnp.min(jnp.where(logits == m1, lane, float(LANES)), axis=1, keepdims=True)
    first = lane == i1
    rest = jnp.where(first, NEG_INF, logits)
    m2 = jnp.max(rest, axis=1, keepdims=True)
    i2 = jnp.min(jnp.where(rest == m2, lane, float(LANES)), axis=1, keepdims=True)
    second = lane == i2
    e = jnp.exp(m2 - m1)
    g1 = 1.0 / (1.0 + e)
    g2 = e * g1

    sel = (jnp.where(first, 1.0, 0.0) + jnp.where(second, 1.0, 0.0)) * counted
    tm = sel.shape[0]
    ri = lax.broadcasted_iota(jnp.int32, (tm, tm), 0)
    ci = lax.broadcasted_iota(jnp.int32, (tm, tm), 1)
    earlier = jnp.where(ri > ci, 1.0, 0.0).astype(BF16)
    ranks = jnp.dot(earlier, sel.astype(BF16), preferred_element_type=F32) + carry_ref[...]
    r1 = jnp.sum(jnp.where(first, ranks, 0.0), axis=1, keepdims=True)
    r2 = jnp.sum(jnp.where(second, ranks, 0.0), axis=1, keepdims=True)
    carry_ref[...] += jnp.sum(sel, axis=0, keepdims=True)
    cnt_ref[...] = carry_ref[...]

    rec = jnp.zeros_like(logits)
    for k, val in ((ROUTE_E1, i1), (ROUTE_E2, i2), (ROUTE_G1, g1), (ROUTE_G2, g2),
                   (ROUTE_R1, r1), (ROUTE_R2, r2)):
        rec = jnp.where(lane_i == k, val, rec)
    route_ref[...] = rec


N_MIX_REFS = 15


def _mix_out_kernel(*refs, alpha, with_router, n_cast, n_prompt_blocks, blocks_per_seq,
                    d_attn, d_pool, d_sg):
    mix_refs = refs[:N_MIX_REFS - 1]
    ms_ref, x_ref, w_ref, g_ref, b_ref = refs[N_MIX_REFS - 1:N_MIX_REFS + 4]
    rest = list(refs[N_MIX_REFS + 4:])
    if with_router:
        rwh_ref, rwl_ref, rb_ref = rest[:3]
        rest = rest[3:]
    cast_src, rest = rest[:n_cast], rest[n_cast:]
    if with_router:
        x1_ref, x1s_ref, route_ref, cnt_ref = rest[:4]
        rest = rest[4:]
    else:
        x1_ref, x1b_ref = rest[:2]
        rest = rest[2:]
    cast_dst, rest = rest[:n_cast], rest[n_cast:]
    if with_router:
        prev_ref, next_ref, carry_ref = rest
    else:
        prev_ref, next_ref = rest
    s = pl.program_id(0)
    _run_casts(cast_src, cast_dst)

    @pl.when(s == 0)
    def _():
        prev_ref[...] = jnp.zeros_like(prev_ref)
        if with_router:
            carry_ref[...] = jnp.zeros_like(carry_ref)

    def project():
        y = alpha * x_ref[...] + jnp.dot(prev_ref[...], w_ref[...], preferred_element_type=F32)
        x1 = _layer_norm(y, g_ref[...], b_ref[...])
        x1_ref[...] = x1
        if with_router:
            _store_slabs(x1s_ref, x1)
            counted = jnp.where(s > 0, 1.0, 0.0)
            _route(x1, counted, rwh_ref, rwl_ref, rb_ref, route_ref, cnt_ref, carry_ref)
        else:
            x1b_ref[...] = x1.astype(BF16)

    @pl.when(s < n_prompt_blocks)
    def _():
        project()
        _prompt_mixers(lax.rem(s, blocks_per_seq), *mix_refs, next_ref,
                       d_attn=d_attn, d_pool=d_pool, d_sg=d_sg)

    @pl.when(s >= n_prompt_blocks)
    def _():
        project()
        next_ref[...] = ms_ref[...]

    prev_ref[...] = next_ref[...]


def _mix_out(sinks, q, k, v, xp, u, vn, pool_w, pool_scale, sg_w, sg_bias, mixed_s, x, w, g, b,
             router=None, cast_jobs=(), *, layer, n_prompt, seq, alpha):
    r, d = x.shape
    d_attn, d_kv, d_pool, d_sg = q.shape[1], k.shape[1], xp.shape[1], u.shape[1]
    d_mix = d_attn + d_pool + d_sg
    assert n_prompt % BLK == 0 and (r - n_prompt) % BLK == 0 and mixed_s.shape[0] == r - n_prompt
    n_p = n_prompt // BLK
    n_s = (r - n_prompt) // BLK
    hist = 2 * SUBLANES
    pblk = lambda s: jnp.minimum(s, n_p - 1)
    cur = lambda s: (pblk(s), 0)
    prev = lambda s: (jnp.maximum(pblk(s) - 1, 0), 0)
    tail = lambda s: (jnp.maximum(pblk(s) * (BLK // hist) - 1, 0), 0)
    samp = lambda s: (jnp.clip(s - n_p, 0, n_s - 1), 0)
    row = lambda s: (jnp.maximum(s - 1, 0), 0)
    const2 = lambda s: (0, 0)
    const3 = lambda s: (0, 0, 0)
    in_specs = [
        pl.BlockSpec(memory_space=pltpu.SMEM),
        pl.BlockSpec((BLK, d_attn), cur),
        pl.BlockSpec((BLK, d_kv), cur),
        pl.BlockSpec((BLK, d_kv), prev),
        pl.BlockSpec((BLK, d_kv), cur),
        pl.BlockSpec((BLK, d_kv), prev),
        pl.BlockSpec((BLK, d_pool), cur),
        pl.BlockSpec((hist, d_pool), tail),
        pl.BlockSpec((BLK, d_sg), cur),
        pl.BlockSpec((BLK, d_sg), cur),
        pl.BlockSpec(pool_w.shape, const3),
        pl.BlockSpec((1, d_pool), const2),
        pl.BlockSpec(sg_w.shape, const3),
        pl.BlockSpec((BLK, d_sg), const2),
        pl.BlockSpec((BLK, d_mix), samp),
        pl.BlockSpec((BLK, d), row),
        pl.BlockSpec((None,) + w.shape[1:], lambda s: (layer, 0, 0), pipeline_mode=pl.Buffered(1)),
        pl.BlockSpec((1, d), const2),
        pl.BlockSpec((1, d), const2),
    ]
    args = [sinks, q, k, k, v, v, xp, xp, u, vn, pool_w, pool_scale, sg_w, sg_bias, mixed_s, x, w, g, b]
    assert len(args) == N_MIX_REFS + 4
    scratch = [pltpu.VMEM((BLK, d_mix), BF16), pltpu.VMEM((BLK, d_mix), BF16)]
    if router is None:
        outs = [jax.ShapeDtypeStruct((r, d), F32), jax.ShapeDtypeStruct((r, d), BF16)]
        out_specs = [pl.BlockSpec((BLK, d), row), pl.BlockSpec((BLK, d), row)]
    else:
        in_specs += [pl.BlockSpec((d, LANES), const2), pl.BlockSpec((d, LANES), const2),
                     pl.BlockSpec((1, LANES), const2)]
        args += list(router)
        outs = [jax.ShapeDtypeStruct((r, d), F32),
                jax.ShapeDtypeStruct((r * SLAB_PITCH, LANES), F32),
                jax.ShapeDtypeStruct((r, LANES), F32),
                jax.ShapeDtypeStruct((1, LANES), F32)]
        out_specs = [pl.BlockSpec((BLK, d), row),
                     pl.BlockSpec((BLK * SLAB_PITCH, LANES), row),
                     pl.BlockSpec((BLK, LANES), row),
                     pl.BlockSpec((1, LANES), const2)]
        scratch.append(pltpu.VMEM((1, LANES), F32))
    n_steps = r // BLK + 1
    c_in, c_shape, c_out, c_args = _cast_operands(cast_jobs, n_steps, lambda s: s)
    n_main = len(outs)
    res = pl.pallas_call(
        functools.partial(_mix_out_kernel, alpha=alpha, with_router=router is not None,
                          n_cast=len(cast_jobs), n_prompt_blocks=n_p, blocks_per_seq=seq // BLK,
                          d_attn=d_attn, d_pool=d_pool, d_sg=d_sg),
        out_shape=tuple(outs + c_shape),
        grid=(n_steps,),
        in_specs=in_specs + c_in,
        out_specs=tuple(out_specs + c_out),
        scratch_shapes=scratch,
        compiler_params=_cparams(1),
        name="mix_out_router" if router is not None else "mix_out",
    )(*args, *c_args)
    return res[:n_main], res[n_main:]


def _swiglu(xb, wg_ref, wu_ref):
    a = jnp.dot(xb, wg_ref[...], preferred_element_type=F32)
    c = jnp.dot(xb, wu_ref[...], preferred_element_type=F32)
    return (a * (1.0 / (1.0 + jnp.exp(-a))) * c).astype(BF16)


def _ffn_kernel(*refs, alpha, n_cast, with_zeros):
    xb_ref, x1_ref, wg_ref, wu_ref, wd_ref, g_ref, b_ref = refs[:7]
    cast_src = refs[7:7 + n_cast]
    o_ref = refs[7 + n_cast]
    cast_dst = refs[8 + n_cast:8 + 2 * n_cast]
    f = pl.program_id(1)
    if with_zeros:
        zero_ref = refs[8 + 2 * n_cast]
        zero_ref[...] = jnp.zeros_like(zero_ref)

    @pl.when(f == 0)
    def _():
        o_ref[...] = alpha * x1_ref[...]

    o_ref[...] += jnp.dot(_swiglu(xb_ref[...], wg_ref, wu_ref), wd_ref[...],
                          preferred_element_type=F32)

    @pl.when(f == pl.num_programs(1) - 1)
    def _():
        o_ref[...] = _layer_norm(o_ref[...], g_ref[...], b_ref[...])

    _run_casts(cast_src, cast_dst)


def _run_casts(src_refs, dst_refs):
    for src, dst in zip(src_refs, dst_refs):
        dst[...] = src[...].astype(BF16)


def _chunk_dst_map(*ids, step_of, last):
    return (jnp.minimum(step_of(*ids), last), 0)


def _chunk_src_map(*ids, step_of, layer, last):
    return (layer,) + _chunk_dst_map(*ids, step_of=step_of, last=last)


def _chunk_rows(rows, steps):
    for t in range(2 * SUBLANES, rows + 1, 2 * SUBLANES):
        if rows % t == 0 and rows // t <= steps:
            return t
    return rows


def _cast_operands(jobs, steps, step_of):
    in_specs, out_shape, out_specs, args = [], [], [], []
    for arr, li in jobs:
        _, rows, cols = arr.shape
        cr = _chunk_rows(rows, steps)
        last = rows // cr - 1
        in_specs.append(pl.BlockSpec(
            (None, cr, cols), functools.partial(_chunk_src_map, step_of=step_of, layer=li, last=last)))
        out_shape.append(jax.ShapeDtypeStruct((rows, cols), BF16))
        out_specs.append(pl.BlockSpec(
            (cr, cols), functools.partial(_chunk_dst_map, step_of=step_of, last=last)))
        args.append(arr)
    return in_specs, out_shape, out_specs, args


def _ffn(xb, x1, wg, wu, wd, g, b, cast_jobs=(), zero_rows=0, *, layer, tm, tf, alpha):
    r, d = x1.shape
    ff = wg.shape[2]
    nf = ff // tf
    steps = (r // tm) * nf
    step_of = lambda i, f: i * nf + f
    row = lambda i, f: (i, 0)
    const = lambda i, f: (0, 0)
    in_specs = [
        pl.BlockSpec((tm, d), row),
        pl.BlockSpec((tm, d), row),
        pl.BlockSpec((None, d, tf), lambda i, f: (layer, 0, f)),
        pl.BlockSpec((None, d, tf), lambda i, f: (layer, 0, f)),
        pl.BlockSpec((None, tf, d), lambda i, f: (layer, f, 0)),
        pl.BlockSpec((1, d), const),
        pl.BlockSpec((1, d), const),
    ]
    c_in, c_shape, c_out, c_args = _cast_operands(cast_jobs, steps, step_of)
    out_shape = [jax.ShapeDtypeStruct((r, d), F32)] + c_shape
    out_specs = [pl.BlockSpec((tm, d), row)] + c_out
    if zero_rows:
        zr = _chunk_rows(zero_rows, steps)
        out_shape.append(jax.ShapeDtypeStruct((zero_rows, LANES), F32))
        out_specs.append(pl.BlockSpec(
            (zr, LANES), functools.partial(_chunk_dst_map, step_of=step_of, last=zero_rows // zr - 1)))
    outs = pl.pallas_call(
        functools.partial(_ffn_kernel, alpha=alpha, n_cast=len(cast_jobs), with_zeros=bool(zero_rows)),
        out_shape=tuple(out_shape),
        grid=(r // tm, nf),
        in_specs=in_specs + c_in,
        out_specs=tuple(out_specs),
        compiler_params=_cparams(2),
        name="ffn",
    )(xb, x1, wg, wu, wd, g, b, *c_args)
    n = len(cast_jobs)
    return outs[0], outs[1:1 + n], (outs[1 + n] if zero_rows else None)


MOE_BLOCK = 512
DMA_RING = 128
DMA_UNROLL = 8


def _ring_depth(n):
    return 1 << (min(DMA_RING, n).bit_length() - 1)


def _routing_tables(route, counts, *, n_work):
    e1 = route[:, ROUTE_E1].astype(jnp.int32)
    e2 = route[:, ROUTE_E2].astype(jnp.int32)
    r1 = route[:, ROUTE_R1].astype(jnp.int32)
    r2 = route[:, ROUTE_R2].astype(jnp.int32)
    cnt = counts[0, :N_EXPERTS].astype(jnp.int32)
    nblk = (cnt + MOE_BLOCK - 1) // MOE_BLOCK
    blk_end = jnp.cumsum(nblk)
    start = (blk_end - nblk) * MOE_BLOCK
    slot1 = jnp.take(start, e1) + r1
    slot2 = jnp.take(start, e2) + r2
    n_active = blk_end[-1]
    w = jnp.arange(n_work, dtype=jnp.int32)
    blk = jnp.minimum(w, n_active - 1)
    work_exp = jnp.sum(blk[:, None] >= blk_end[None, :], axis=1).astype(jnp.int32)
    first_blk = jnp.take(blk_end - nblk, work_exp)
    work_rows = jnp.clip(jnp.take(cnt, work_exp) - (w - first_blk) * MOE_BLOCK, 0, MOE_BLOCK)
    work_rows = jnp.where(w < n_active, work_rows, 0).astype(jnp.int32)
    return slot1, slot2, work_exp, work_rows


def _ring_copies(n, ring, make_copies):
    def start(i):
        for c in make_copies(i):
            c.start()

    def wait(i):
        for c in make_copies(i):
            c.wait()

    def fill(i, carry):
        start(i)
        return carry

    def steady(i, carry):
        wait(i - ring)
        start(i)
        return carry

    def drain(i, carry):
        wait(i)
        return carry

    lax.fori_loop(0, ring, fill, 0, unroll=DMA_UNROLL)
    lax.fori_loop(ring, n, steady, 0, unroll=DMA_UNROLL)
    lax.fori_loop(n - ring, n, drain, 0, unroll=DMA_UNROLL)


SLAB_ROWS = 16


def _slab(ref, token):
    return ref.at[pl.ds(pl.multiple_of(token * SLAB_PITCH, SUBLANES), SLAB_ROWS), :]


def _dispatch_kernel(s1_ref, s2_ref, x1s_ref, xs_init, xs_hbm, sem):
    del xs_init
    tm = x1s_ref.shape[0] // SLAB_PITCH
    base = pl.program_id(0) * tm
    ring = sem.shape[1]

    def copies(r):
        k = r & (ring - 1)
        src = _slab(x1s_ref, r)
        return (pltpu.make_async_copy(src, _slab(xs_hbm, s1_ref[base + r]), sem.at[0, k]),
                pltpu.make_async_copy(src, _slab(xs_hbm, s2_ref[base + r]), sem.at[1, k]))

    _ring_copies(tm, ring, copies)


def _moe_dispatch(slot1, slot2, x1s, xs_init, *, tm):
    r = x1s.shape[0] // SLAB_PITCH
    return pl.pallas_call(
        _dispatch_kernel,
        out_shape=jax.ShapeDtypeStruct(xs_init.shape, F32),
        grid_spec=pltpu.PrefetchScalarGridSpec(
            num_scalar_prefetch=2,
            grid=(r // tm,),
            in_specs=[pl.BlockSpec((tm * SLAB_PITCH, LANES), lambda i, s1, s2: (i, 0)),
                      pl.BlockSpec(memory_space=pl.ANY)],
            out_specs=pl.BlockSpec(memory_space=pl.ANY),
            scratch_shapes=[pltpu.SemaphoreType.DMA((2, _ring_depth(tm)))],
        ),
        input_output_aliases={3: 0},
        compiler_params=_cparams(1),
        name="moe_dispatch",
    )(slot1, slot2, x1s, xs_init)


def _moe_ffn_kernel(we_ref, wr_ref, xs_ref, wg_ref, wu_ref, wd_ref, ys_ref, xb_ref, acc_ref):
    del we_ref
    w = pl.program_id(0)
    f = pl.program_id(1)
    n, d = acc_ref.shape
    active = wr_ref[w] > 0

    def partial_out():
        return jnp.dot(_swiglu(xb_ref[...], wg_ref, wu_ref), wd_ref[...],
                       preferred_element_type=F32)

    @pl.when(active & (f == 0))
    def _():
        for j in range(d // LANES):
            xb_ref[:, j * LANES:(j + 1) * LANES] = _load_slab_chunk(xs_ref, j, n).astype(BF16)
        acc_ref[...] = partial_out()

    @pl.when(active & (f > 0))
    def _():
        acc_ref[...] += partial_out()

    @pl.when(jnp.logical_not(active) & (f == 0))
    def _():
        acc_ref[...] = jnp.zeros_like(acc_ref)

    @pl.when(f == pl.num_programs(1) - 1)
    def _():
        _store_slabs(ys_ref, acc_ref[...])


def _moe_ffn(work_exp, work_rows, xs, wg, wu, wd, *, layer, tf):
    d = wg.shape[2]
    ff = wg.shape[3]
    nf = ff // tf
    rows = lambda w, f, we, wr: (w, 0)
    chunk = lambda w, f, wr: jnp.where(wr[w] > 0, f, nf - 1)
    return pl.pallas_call(
        _moe_ffn_kernel,
        out_shape=jax.ShapeDtypeStruct(xs.shape, F32),
        grid_spec=pltpu.PrefetchScalarGridSpec(
            num_scalar_prefetch=2,
            grid=(xs.shape[0] // (MOE_BLOCK * SLAB_PITCH), nf),
            in_specs=[
                pl.BlockSpec((MOE_BLOCK * SLAB_PITCH, LANES), rows),
                pl.BlockSpec((None, None, d, tf),
                             lambda w, f, we, wr: (layer, we[w], 0, chunk(w, f, wr))),
                pl.BlockSpec((None, None, d, tf),
                             lambda w, f, we, wr: (layer, we[w], 0, chunk(w, f, wr))),
                pl.BlockSpec((None, None, tf, d),
                             lambda w, f, we, wr: (layer, we[w], chunk(w, f, wr), 0)),
            ],
            out_specs=pl.BlockSpec((MOE_BLOCK * SLAB_PITCH, LANES), rows),
            scratch_shapes=[pltpu.VMEM((MOE_BLOCK, d), BF16), pltpu.VMEM((MOE_BLOCK, d), F32)],
        ),
        compiler_params=_cparams(2),
        name="moe_ffn",
    )(work_exp, work_rows, xs, wg, wu, wd)


def _combine_kernel(s1_ref, s2_ref, ys_hbm, x1_ref, route_ref, g_ref, b_ref, o_ref,
                    y1_ref, y2_ref, sem, *, alpha):
    tm, d = o_ref.shape
    base = pl.program_id(0) * tm
    ring = sem.shape[1]

    def copies(r):
        k = r & (ring - 1)
        return (pltpu.make_async_copy(_slab(ys_hbm, s1_ref[base + r]), _slab(y1_ref, r), sem.at[0, k]),
                pltpu.make_async_copy(_slab(ys_hbm, s2_ref[base + r]), _slab(y2_ref, r), sem.at[1, k]))

    _ring_copies(tm, ring, copies)
    route = route_ref[...]
    g1 = _lane_pick(route, ROUTE_G1)
    g2 = _lane_pick(route, ROUTE_G2)
    for j in range(d // LANES):
        sl = slice(j * LANES, (j + 1) * LANES)
        o_ref[:, sl] = (alpha * x1_ref[:, sl] + g1 * _load_slab_chunk(y1_ref, j, tm)
                        + g2 * _load_slab_chunk(y2_ref, j, tm))
    o_ref[...] = _layer_norm(o_ref[...], g_ref[...], b_ref[...])


def _moe_combine(slot1, slot2, ys, x1, route, g, b, *, tm, alpha):
    r, d = x1.shape
    row = lambda i, s1, s2: (i, 0)
    const = lambda i, s1, s2: (0, 0)
    return pl.pallas_call(
        functools.partial(_combine_kernel, alpha=alpha),
        out_shape=jax.ShapeDtypeStruct((r, d), F32),
        grid_spec=pltpu.PrefetchScalarGridSpec(
            num_scalar_prefetch=2,
            grid=(r // tm,),
            in_specs=[
                pl.BlockSpec(memory_space=pl.ANY),
                pl.BlockSpec((tm, d), row),
                pl.BlockSpec((tm, LANES), row),
                pl.BlockSpec((1, d), const),
                pl.BlockSpec((1, d), const),
            ],
            out_specs=pl.BlockSpec((tm, d), row),
            scratch_shapes=[pltpu.VMEM((tm * SLAB_PITCH, LANES), F32),
                            pltpu.VMEM((tm * SLAB_PITCH, LANES), F32),
                            pltpu.SemaphoreType.DMA((2, _ring_depth(tm)))],
        ),
        compiler_params=_cparams(1),
        name="moe_combine",
    )(slot1, slot2, ys, x1, route, g, b)


def _moe_work_items(r):
    return (2 * r + MOE_BLOCK - 1) // MOE_BLOCK + N_EXPERTS


def _moe(x1, x1s, route, counts, wg, wu, wd, g, b, xs_buf=None, *, layer, tm, tf, alpha):
    r = x1.shape[0]
    n_work = _moe_work_items(r)
    slot1, slot2, work_exp, work_rows = _routing_tables(route, counts, n_work=n_work)
    if xs_buf is None:
        xs_buf = jnp.zeros((n_work * MOE_BLOCK * SLAB_PITCH, LANES), F32)
    xs = _moe_dispatch(slot1, slot2, x1s, xs_buf, tm=tm)
    ys = _moe_ffn(work_exp, work_rows, xs, wg, wu, wd, layer=layer, tf=tf)
    return _moe_combine(slot1, slot2, ys, x1, route, g, b, tm=tm, alpha=alpha), xs


def _rope_tables(pos):
    half = HEAD_DIM // 2
    inv = ROPE_THETA ** (-jnp.arange(half, dtype=F32) / half)
    ang = pos.astype(F32)[:, None] * inv[None, :]
    cos = jnp.cos(ang)
    sin = jnp.sin(ang)
    cos2 = jnp.concatenate([cos, cos, cos, cos], axis=1)
    sin2 = jnp.concatenate([-sin, sin, -sin, sin], axis=1)
    return cos2, sin2


def _row_tile(r, cap):
    best = 16
    for t in range(16, cap + 1, 16):
        if r % t == 0:
            best = t
    return best


def kernel(x_prompt, x_sample, cache_k, cache_v, state_pool, w_in, w_out, attn_sinks, pool_w,
           pool_scale, sg_w, sg_b, sg_norm_g, sg_norm_b, ln1_g, ln1_b, ln2_g, ln2_b,
           ffn_w_gate, ffn_w_up, ffn_w_down, router_w, router_b, moe_w_gate, moe_w_up, moe_w_down):
    bp, tp, d = x_prompt.shape
    bs, ts, _ = x_sample.shape
    depth = w_in.shape[0]
    d_kv = cache_k.shape[3] * cache_k.shape[4]
    d_pool = pool_scale.shape[1]
    d_sg = sg_norm_g.shape[1]
    d_attn = w_in.shape[2] - 2 * d_kv - d_pool - 2 * d_sg
    win_buf = cache_k.shape[2]
    assert win_buf == WINDOW == BLK and tp % BLK == 0 and tp >= BLK and ts <= SAMPLE_PAD
    assert bs % SEQ_PER_STEP == 0 and state_pool.shape[2] == POOL_STATE
    assert cache_k.shape[3] == N_KV_HEADS and cache_k.shape[4] == HEAD_DIM
    assert d == SLAB_ROWS * LANES
    alpha = (2 * depth) ** 0.25

    r_p = bp * tp
    r_s = bs * SAMPLE_PAD
    r = r_p + r_s
    tm = _row_tile(r, 768)
    tm_ffn = _row_tile(r, 704)
    tm_comb = _row_tile(r, 384)

    xs_pad = jnp.pad(x_sample, ((0, 0), (0, SAMPLE_PAD - ts), (0, 0)))
    x = jnp.concatenate([x_prompt.reshape(r_p, d), xs_pad.reshape(r_s, d)], axis=0)

    pos = jnp.concatenate([
        jnp.tile(jnp.arange(tp, dtype=jnp.int32), bp),
        jnp.tile(PAST_LEN + jnp.arange(SAMPLE_PAD, dtype=jnp.int32), bs)])
    cos, sin = _rope_tables(pos)

    in_w = {0: w_in[0].astype(BF16)}
    out_w = {0: w_out[0].astype(BF16)}
    pool_w_b = pool_w.astype(BF16)
    rw_pad = jnp.pad(router_w, ((0, 0), (0, 0), (0, LANES - router_w.shape[2])))
    rw_hi = rw_pad.astype(BF16)
    rw_lo = (rw_pad - rw_hi.astype(F32)).astype(BF16)
    rb_pad = jnp.pad(router_b, ((0, 0), (0, LANES - router_b.shape[1])))[:, None, :]
    sg_bias = jnp.repeat(jnp.swapaxes(sg_b, 1, 2), d_sg // N_SG_GROUPS, axis=2)
    state_pad = jnp.pad(state_pool, ((0, 0), (0, 0), (1, 0), (0, 0)))
    state_rows = state_pad.reshape((depth * bs,) + state_pad.shape[2:])
    ck_rows = cache_k.reshape(depth * bs * win_buf, d_kv)
    cv_rows = cache_v.reshape(depth * bs * win_buf, d_kv)

    k_l, v_l, xp_l, vn_l = [], [], [], []
    xs_buf = None
    early_gate = {}
    for l in range(depth):
        q, k, v, xp, u, vn = _inproj(
            x, in_w[l][None], cos, sin, sg_norm_g[l][None], sg_norm_b[l][None],
            layer=0, tm=tm, d_attn=d_attn, d_kv=d_kv, d_pool=d_pool, d_sg=d_sg)
        mixed_s = _mix_sample(attn_sinks[l], q, k, v, xp, u, vn, ck_rows, cv_rows, state_rows,
                              pool_w_b[l], pool_scale[l][None], sg_w[l], sg_bias[l],
                              layer=l, r_prompt=r_p, n_seq=bs)
        i = l // 2
        is_dense = l % 2 == 0
        jobs = []
        if l + 1 < depth:
            jobs += [(w_in, l + 1), (w_out, l + 1)]
        own = (ffn_w_gate, ffn_w_up, ffn_w_down) if is_dense else (moe_w_gate, moe_w_up, moe_w_down)
        rows_of = lambda w: w.reshape(w.shape[0], -1, w.shape[-1])
        early = early_gate.pop(i, None) if not is_dense else None
        jobs += [(rows_of(w), i) for w in (own[1:] if early is not None else own)]
        lends_hand = is_dense and l + 1 < depth
        if lends_hand:
            jobs.append((rows_of(moe_w_gate), (l + 1) // 2))
        main, casted = _mix_out(
            attn_sinks[l], q, k, v, xp, u, vn, pool_w_b[l], pool_scale[l][None], sg_w[l],
            sg_bias[l], mixed_s, x, out_w[l][None], ln1_g[l][None], ln1_b[l][None],
            None if is_dense else (rw_hi[i], rw_lo[i], rb_pad[i]), tuple(jobs),
            layer=0, n_prompt=r_p, seq=tp, alpha=alpha)
        casted = list(casted)
        if l + 1 < depth:
            in_w[l + 1], out_w[l + 1] = casted[:2]
            casted = casted[2:]
        if lends_hand:
            early_gate[(l + 1) // 2] = casted.pop()
        if early is not None:
            casted = [early] + casted
        if is_dense:
            x1, x1b = main
            ffn_g, ffn_u, ffn_d = (c[None] for c in casted)
            zero_rows = 0
            if l + 1 < depth and xs_buf is None:
                zero_rows = _moe_work_items(r) * MOE_BLOCK * SLAB_PITCH
            x, _, zeros = _ffn(x1b, x1, ffn_g, ffn_u, ffn_d, ln2_g[l][None], ln2_b[l][None],
                               (), zero_rows, layer=0, tm=tm_ffn, tf=512, alpha=alpha)
            if zero_rows:
                xs_buf = zeros
        else:
            x1, x1s, route, counts = main
            moe_g, moe_u, moe_d = (c.reshape((1,) + w.shape[1:]) for c, w in zip(casted, own))
            x, xs_buf = _moe(x1, x1s, route, counts, moe_g, moe_u, moe_d, ln2_g[l][None],
                             ln2_b[l][None], xs_buf, layer=0, tm=tm_comb, tf=512, alpha=alpha)

        tail = max(WINDOW, POOL_STATE)
        keep = lambda a: jnp.concatenate(
            [a[:r_p].reshape(bp, tp, -1)[:, -tail:].reshape(bp * tail, -1), a[r_p:]], axis=0)
        k_l.append(keep(k))
        v_l.append(keep(v))
        xp_l.append(keep(xp))
        vn_l.append(vn[r_p:])

    kh = (N_KV_HEADS, HEAD_DIM)
    n_tail = bp * max(WINDOW, POOL_STATE)
    ks, vs, xps, vns = (jnp.stack(a) for a in (k_l, v_l, xp_l, vn_l))
    prompt_tail = lambda a, n: a[:, :n_tail].reshape(depth, bp, -1, a.shape[-1])[:, :, -n:]
    sample_new = lambda a: a[:, -r_s:].reshape(depth, bs, SAMPLE_PAD, a.shape[-1])[:, :, :ts]
    new_k_p = prompt_tail(ks, WINDOW).reshape(depth, bp, WINDOW, *kh)
    new_v_p = prompt_tail(vs, WINDOW).reshape(depth, bp, WINDOW, *kh)
    new_pool_p = prompt_tail(xps, POOL_STATE)
    new_k_s = jnp.concatenate(
        [cache_k, sample_new(ks).reshape(depth, bs, ts, *kh)], axis=2)[:, :, -win_buf:]
    new_v_s = jnp.concatenate(
        [cache_v, sample_new(vs).reshape(depth, bs, ts, *kh)], axis=2)[:, :, -win_buf:]
    new_pool_s = jnp.concatenate([state_pool, sample_new(xps)], axis=2)[:, :, -POOL_STATE:]
    new_sg_s = vns.reshape(depth, bs, SAMPLE_PAD, d_sg)[:, :, :ts]

    y_prompt = x[:r_p].reshape(bp, tp, d)
    y_sample = x[r_p:].reshape(bs, SAMPLE_PAD, d)[:, :ts]
    return (y_prompt, y_sample, new_k_p, new_v_p, new_pool_p, new_k_s, new_v_s, new_pool_s, new_sg_s)
```

```python
import functools
import math

import jax
import jax.numpy as jnp
from jax import lax
from jax.experimental import pallas as pl
from jax.experimental.pallas import tpu as pltpu

F32 = jnp.float32
BF16 = jnp.bfloat16

PAST_LEN = 16384
HEAD_DIM = 64
N_KV_HEADS = 2
WINDOW = 128
BLK = 128
ROPE_THETA = 10000.0
POOL_WINDOWS = (2, 4, 8, 16)
POOL_STATE = 15
N_SG_GROUPS = 4
N_EXPERTS = 8
LN_EPS = 1e-5

LANES = 128
SUBLANES = 8
SAMPLE_PAD = SUBLANES
VMEM_LIMIT = 56 * 1024 * 1024

NEG_INF = float("-inf")


def _cparams(n_axes):
    return pltpu.CompilerParams(
        dimension_semantics=("arbitrary",) * n_axes, vmem_limit_bytes=VMEM_LIMIT)


def _layer_norm(x, g, b):
    mu = jnp.mean(x, axis=-1, keepdims=True)
    xc = x - mu
    var = jnp.mean(xc * xc, axis=-1, keepdims=True)
    return xc * lax.rsqrt(var + LN_EPS) * g + b


def _gelu_tanh(x):
    return 0.5 * x * (1.0 + jnp.tanh(0.7978845608028654 * (x + 0.044715 * (x * x * x))))


def _inproj_kernel(x_ref, w_ref, cos_ref, sin_ref, g_ref, b_ref,
                   q_ref, k_ref, v_ref, xp_ref, u_ref, vn_ref, *, d_attn, d_kv, d_pool, d_sg):
    xb = x_ref[...].astype(BF16)
    cos = cos_ref[...]
    sin = sin_ref[...]
    tm = xb.shape[0]
    lane = lax.broadcasted_iota(jnp.int32, (tm, LANES), 1)
    first_half = (lane % HEAD_DIM) < (HEAD_DIM // 2)

    def rope(z):
        rot = jnp.where(first_half, pltpu.roll(z, LANES - HEAD_DIM // 2, 1),
                        pltpu.roll(z, HEAD_DIM // 2, 1))
        return z * cos + rot * sin

    c0 = 0
    zq = jnp.dot(xb, w_ref[:, c0:c0 + d_attn], preferred_element_type=F32)
    for j in range(d_attn // LANES):
        sl = slice(j * LANES, (j + 1) * LANES)
        q_ref[:, sl] = (rope(zq[:, sl]) * (HEAD_DIM ** -0.5)).astype(BF16)
    c0 += d_attn
    zkv = jnp.dot(xb, w_ref[:, c0:c0 + 2 * d_kv], preferred_element_type=F32)
    k_ref[...] = rope(zkv[:, :d_kv])
    v_ref[...] = zkv[:, d_kv:]
    c0 += 2 * d_kv
    xp_ref[...] = jnp.dot(xb, w_ref[:, c0:c0 + d_pool], preferred_element_type=F32)
    c0 += d_pool
    u_ref[...] = _gelu_tanh(jnp.dot(xb, w_ref[:, c0:c0 + d_sg], preferred_element_type=F32))
    c0 += d_sg
    vg = _gelu_tanh(jnp.dot(xb, w_ref[:, c0:c0 + d_sg], preferred_element_type=F32))
    gd = d_sg // N_SG_GROUPS
    for g in range(N_SG_GROUPS):
        sl = slice(g * gd, (g + 1) * gd)
        vn_ref[:, sl] = _layer_norm(vg[:, sl], g_ref[:, sl], b_ref[:, sl])


def _inproj(x, w, cos, sin, sg_g, sg_b, *, layer, tm, d_attn, d_kv, d_pool, d_sg):
    r, d = x.shape
    d_in = w.shape[2]
    row = lambda i: (i, 0)
    const = lambda i: (0, 0)
    outs = (
        jax.ShapeDtypeStruct((r, d_attn), BF16),
        jax.ShapeDtypeStruct((r, d_kv), F32),
        jax.ShapeDtypeStruct((r, d_kv), F32),
        jax.ShapeDtypeStruct((r, d_pool), F32),
        jax.ShapeDtypeStruct((r, d_sg), F32),
        jax.ShapeDtypeStruct((r, d_sg), F32),
    )
    return pl.pallas_call(
        functools.partial(_inproj_kernel, d_attn=d_attn, d_kv=d_kv, d_pool=d_pool, d_sg=d_sg),
        out_shape=outs,
        grid=(r // tm,),
        in_specs=[
            pl.BlockSpec((tm, d), row),
            pl.BlockSpec((None, d, d_in), lambda i: (layer, 0, 0), pipeline_mode=pl.Buffered(1)),
            pl.BlockSpec((tm, LANES), row),
            pl.BlockSpec((tm, LANES), row),
            pl.BlockSpec((1, d_sg), const),
            pl.BlockSpec((1, d_sg), const),
        ],
        out_specs=(
            pl.BlockSpec((tm, d_attn), row),
            pl.BlockSpec((tm, d_kv), row),
            pl.BlockSpec((tm, d_kv), row),
            pl.BlockSpec((tm, d_pool), row),
            pl.BlockSpec((tm, d_sg), row),
            pl.BlockSpec((tm, d_sg), row),
        ),
        compiler_params=_cparams(1),
        name="inproj",
    )(x, w, cos, sin, sg_g, sg_b)


def _split_kv(kv):
    lane = lax.broadcasted_iota(jnp.int32, kv.shape, 1)
    low = lane < HEAD_DIM
    a = kv.astype(BF16)
    b = pltpu.roll(kv, HEAD_DIM, 1).astype(BF16)
    zero = jnp.zeros_like(a)
    head0 = (jnp.where(low, a, zero), jnp.where(low, zero, b))
    head1 = (jnp.where(low, b, zero), jnp.where(low, zero, a))
    return head0, head1


def _attend(q, k_pair, v_pair, allowed, sinks):
    out = None
    for kx, vx, sk in zip(k_pair, v_pair, sinks):
        s = lax.dot_general(q, kx, (((1,), (1,)), ((), ())), preferred_element_type=F32)
        s = jnp.where(allowed, s, NEG_INF)
        m = jnp.maximum(jnp.max(s, axis=1, keepdims=True), sk)
        p = jnp.exp(s - m)
        denom = jnp.sum(p, axis=1, keepdims=True) + jnp.exp(sk - m)
        o = jnp.dot(p.astype(BF16), vx, preferred_element_type=F32) * (1.0 / denom)
        out = o if out is None else out + o
    return out


def _window_sums(full):
    sums = {1: full}
    w = 1
    while w < max(POOL_WINDOWS):
        sums[2 * w] = sums[w] + pltpu.roll(sums[w], w, 0)
        w *= 2
    return sums


def _pool_mix(full, cur, cnt_of, pw_ref, ps_ref, row0):
    n = cur.shape[0]
    gd = cur.shape[1] // len(POOL_WINDOWS)
    outs = []
    for g, w in enumerate(POOL_WINDOWS):
        sl = slice(g * gd, (g + 1) * gd)
        sums = _window_sums(full[:, sl])[w]
        pooled = sums[row0:row0 + n, :] / cnt_of(w)
        d = (pooled - cur[:, sl]).astype(BF16)
        outs.append(jnp.dot(d, pw_ref[g], preferred_element_type=F32) * ps_ref[:, sl])
    return outs


def _causal_weights(sw_ref, g):
    n = sw_ref.shape[1]
    ri = lax.broadcasted_iota(jnp.int32, (n, n), 0)
    ci = lax.broadcasted_iota(jnp.int32, (n, n), 1)
    return jnp.where(ri >= ci, sw_ref[g], 0.0).astype(BF16)


def _prompt_mixers(i, sink_ref, q_ref, kc_ref, kp_ref, vc_ref, vp_ref, xc_ref, xt_ref, u_ref,
                   vn_ref, pw_ref, ps_ref, sw_ref, sb_ref, o_ref, *, d_attn, d_pool, d_sg):
    has_prev = i > 0

    kall = jnp.concatenate([kp_ref[...], kc_ref[...]], axis=0)
    vall = jnp.concatenate([vp_ref[...], vc_ref[...]], axis=0)
    k_heads = _split_kv(kall)
    v_heads = _split_kv(vall)
    ri = lax.broadcasted_iota(jnp.int32, (BLK, 2 * BLK), 0)
    cj = lax.broadcasted_iota(jnp.int32, (BLK, 2 * BLK), 1)
    first_key = jnp.where(has_prev, 0, BLK)
    allowed = (cj >= ri) & (cj <= ri + WINDOW) & (cj >= first_key)
    n_pairs = d_attn // LANES
    pairs_per_kv = n_pairs // N_KV_HEADS
    for p in range(n_pairs):
        g = p // pairs_per_kv
        sl = slice(p * LANES, (p + 1) * LANES)
        o = _attend(q_ref[:, sl], k_heads[g], v_heads[g], allowed,
                    (sink_ref[2 * p], sink_ref[2 * p + 1]))
        o_ref[:, sl] = o.astype(BF16)

    xc = xc_ref[...]
    tail = jnp.where(has_prev, xt_ref[...], 0.0)
    full = jnp.concatenate([tail, xc], axis=0)
    hist = tail.shape[0]
    pos = i * BLK + lax.broadcasted_iota(jnp.int32, (BLK, 1), 0)
    cnt_of = lambda w: jnp.minimum(pos + 1, w).astype(F32)
    pooled = _pool_mix(full, xc, cnt_of, pw_ref, ps_ref, hist)
    gd = d_pool // len(POOL_WINDOWS)
    for g, y in enumerate(pooled):
        o_ref[:, d_attn + g * gd:d_attn + (g + 1) * gd] = y.astype(BF16)

    gs = d_sg // N_SG_GROUPS
    for g in range(N_SG_GROUPS):
        sl = slice(g * gs, (g + 1) * gs)
        s = jnp.dot(_causal_weights(sw_ref, g), vn_ref[:, sl].astype(BF16),
                    preferred_element_type=F32) + sb_ref[:, sl]
        c0 = d_attn + d_pool + g * gs
        o_ref[:, c0:c0 + gs] = (u_ref[:, sl] * s).astype(BF16)


SEQ_PER_STEP = 2


def _mix_sample_kernel(sink_ref, q_ref, kn_ref, vn_new_ref, ck_ref, cv_ref, xn_ref, st_ref,
                       u_ref, vn_ref, pw_ref, ps_ref, sw_ref, sb_ref, o_ref, *, d_attn, d_pool, d_sg):
    sp = SAMPLE_PAD
    n_pairs = d_attn // LANES
    pairs_per_kv = n_pairs // N_KV_HEADS
    m_rows = pairs_per_kv * sp
    qf = q_ref[...].astype(F32)

    ri = lax.broadcasted_iota(jnp.int32, (m_rows, 2 * BLK), 0) % sp
    cj = lax.broadcasted_iota(jnp.int32, (m_rows, 2 * BLK), 1)
    allowed = (cj >= ri) & (cj <= ri + WINDOW)
    row_pair = lax.broadcasted_iota(jnp.int32, (m_rows, 1), 0) // sp

    attn_rows = []
    for s in range(SEQ_PER_STEP):
        rows = slice(s * sp, (s + 1) * sp)
        zpad = jnp.zeros((BLK - sp, kn_ref.shape[1]), F32)
        kall = jnp.concatenate([ck_ref[s * BLK:(s + 1) * BLK, :], kn_ref[rows, :], zpad], axis=0)
        vall = jnp.concatenate([cv_ref[s * BLK:(s + 1) * BLK, :], vn_new_ref[rows, :], zpad], axis=0)
        k_heads = _split_kv(kall)
        v_heads = _split_kv(vall)
        per_pair = []
        for g in range(N_KV_HEADS):
            qst = jnp.concatenate(
                [qf[rows, (g * pairs_per_kv + pl_) * LANES:(g * pairs_per_kv + pl_ + 1) * LANES]
                 for pl_ in range(pairs_per_kv)], axis=0).astype(BF16)
            sink_lo = jnp.zeros((m_rows, 1), F32)
            sink_hi = jnp.zeros((m_rows, 1), F32)
            for pl_ in range(pairs_per_kv):
                h = 2 * (g * pairs_per_kv + pl_)
                sink_lo = jnp.where(row_pair == pl_, sink_ref[h], sink_lo)
                sink_hi = jnp.where(row_pair == pl_, sink_ref[h + 1], sink_hi)
            o = _attend(qst, k_heads[g], v_heads[g], allowed, (sink_lo, sink_hi))
            per_pair.extend(o[pl_ * sp:(pl_ + 1) * sp, :] for pl_ in range(pairs_per_kv))
        attn_rows.append(per_pair)
    for p in range(n_pairs):
        o_ref[:, p * LANES:(p + 1) * LANES] = jnp.concatenate(
            [attn_rows[s][p] for s in range(SEQ_PER_STEP)], axis=0).astype(BF16)

    gd = d_pool // len(POOL_WINDOWS)
    pooled = []
    for s in range(SEQ_PER_STEP):
        rows = slice(s * sp, (s + 1) * sp)
        xc = xn_ref[rows, :]
        full = jnp.concatenate([st_ref[s], xc], axis=0)
        cnt_of = lambda w: float(w)
        pooled.append(_pool_mix(full, xc, cnt_of, pw_ref, ps_ref, st_ref.shape[1]))
    for g in range(len(POOL_WINDOWS)):
        o_ref[:, d_attn + g * gd:d_attn + (g + 1) * gd] = jnp.concatenate(
            [pooled[s][g] for s in range(SEQ_PER_STEP)], axis=0).astype(BF16)

    gs = d_sg // N_SG_GROUPS
    for g in range(N_SG_GROUPS):
        sl = slice(g * gs, (g + 1) * gs)
        wm = _causal_weights(sw_ref, g)
        outs = []
        for s in range(SEQ_PER_STEP):
            rows = slice(s * sp, (s + 1) * sp)
            vpad = jnp.concatenate([vn_ref[rows, sl], jnp.zeros((BLK - sp, gs), F32)], axis=0)
            sg = jnp.dot(wm, vpad.astype(BF16), preferred_element_type=F32)[:sp, :] + sb_ref[:sp, sl]
            outs.append(u_ref[rows, sl] * sg)
        c0 = d_attn + d_pool + g * gs
        o_ref[:, c0:c0 + gs] = jnp.concatenate(outs, axis=0).astype(BF16)


def _mix_sample(sinks, q, k, v, xp, u, vn, cache_k, cache_v, state, pool_w, pool_scale,
                sg_w, sg_bias, *, layer, r_prompt, n_seq):
    d_attn, d_kv, d_pool, d_sg = q.shape[1], k.shape[1], xp.shape[1], u.shape[1]
    d_mix = d_attn + d_pool + d_sg
    rows = SEQ_PER_STEP * SAMPLE_PAD
    base = r_prompt // rows
    n_steps = n_seq // SEQ_PER_STEP
    new = lambda i: (base + i, 0)
    per_seq = lambda i: (i, 0)
    carried = lambda i: (layer * n_steps + i, 0)
    const2 = lambda i: (0, 0)
    const3 = lambda i: (0, 0, 0)
    return pl.pallas_call(
        functools.partial(_mix_sample_kernel, d_attn=d_attn, d_pool=d_pool, d_sg=d_sg),
        out_shape=jax.ShapeDtypeStruct((n_seq * SAMPLE_PAD, d_mix), BF16),
        grid=(n_seq // SEQ_PER_STEP,),
        in_specs=[
            pl.BlockSpec(memory_space=pltpu.SMEM),
            pl.BlockSpec((rows, d_attn), new),
            pl.BlockSpec((rows, d_kv), new),
            pl.BlockSpec((rows, d_kv), new),
            pl.BlockSpec((SEQ_PER_STEP * BLK, d_kv), carried),
            pl.BlockSpec((SEQ_PER_STEP * BLK, d_kv), carried),
            pl.BlockSpec((rows, d_pool), new),
            pl.BlockSpec((SEQ_PER_STEP,) + state.shape[1:], lambda i: (layer * n_steps + i, 0, 0)),
            pl.BlockSpec((rows, d_sg), new),
            pl.BlockSpec((rows, d_sg), new),
            pl.BlockSpec(pool_w.shape, const3),
            pl.BlockSpec((1, d_pool), const2),
            pl.BlockSpec(sg_w.shape, const3),
            pl.BlockSpec((BLK, d_sg), const2),
        ],
        out_specs=pl.BlockSpec((rows, d_mix), per_seq),
        compiler_params=_cparams(1),
        name="mix_sample",
    )(sinks, q, k, v, cache_k, cache_v, xp, state, u, vn, pool_w, pool_scale, sg_w, sg_bias)


ROUTE_E1, ROUTE_E2, ROUTE_G1, ROUTE_G2, ROUTE_R1, ROUTE_R2 = range(6)

SLAB_PITCH = 24


def _lane_pick(rec, k):
    lane = lax.broadcasted_iota(jnp.int32, rec.shape, 1)
    return jnp.sum(jnp.where(lane == k, rec, 0.0), axis=1, keepdims=True)


def _store_slabs(slab_ref, x):
    n, d = x.shape
    slab_ref[...] = jnp.zeros_like(slab_ref)
    for j in range(d // LANES):
        slab_ref[pl.ds(j, n, stride=SLAB_PITCH), :] = x[:, j * LANES:(j + 1) * LANES]


def _load_slab_chunk(slab_ref, j, n):
    return slab_ref[pl.ds(j, n, stride=SLAB_PITCH), :]


def _route(x1, counted, rwh_ref, rwl_ref, rb_ref, route_ref, cnt_ref, carry_ref):
    hi = x1.astype(BF16)
    lo = (x1 - hi.astype(F32)).astype(BF16)
    logits = (jnp.dot(hi, rwh_ref[...], preferred_element_type=F32)
              + jnp.dot(lo, rwh_ref[...], preferred_element_type=F32)
              + jnp.dot(hi, rwl_ref[...], preferred_element_type=F32) + rb_ref[...])
    lane_i = lax.broadcasted_iota(jnp.int32, logits.shape, 1)
    lane = lane_i.astype(F32)
    logits = jnp.where(lane_i < N_EXPERTS, logits, NEG_INF)
    m1 = jnp.max(logits, axis=1, keepdims=True)
    i1 = jnp.min(jnp.where(logits == m1, lane, float(LANES)), axis=1, keepdims=True)
    first = lane == i1
    rest = jnp.where(first, NEG_INF, logits)
    m2 = jnp.max(rest, axis=1, keepdims=True)
    i2 = jnp.min(jnp.where(rest == m2, lane, float(LANES)), axis=1, keepdims=True)
    second = lane == i2
    e = jnp.exp(m2 - m1)
    g1 = 1.0 / (1.0 + e)
    g2 = e * g1

    sel = (jnp.where(first, 1.0, 0.0) + jnp.where(second, 1.0, 0.0)) * counted
    tm = sel.shape[0]
    ri = lax.broadcasted_iota(jnp.int32, (tm, tm), 0)
    ci = lax.broadcasted_iota(jnp.int32, (tm, tm), 1)
    earlier = jnp.where(ri > ci, 1.0, 0.0).astype(BF16)
    ranks = jnp.dot(earlier, sel.astype(BF16), preferred_element_type=F32) + carry_ref[...]
    r1 = jnp.sum(jnp.where(first, ranks, 0.0), axis=1, keepdims=True)
    r2 = jnp.sum(jnp.where(second, ranks, 0.0), axis=1, keepdims=True)
    carry_ref[...] += jnp.sum(sel, axis=0, keepdims=True)
    cnt_ref[...] = carry_ref[...]

    rec = jnp.zeros_like(logits)
    for k, val in ((ROUTE_E1, i1), (ROUTE_E2, i2), (ROUTE_G1, g1), (ROUTE_G2, g2),
                   (ROUTE_R1, r1), (ROUTE_R2, r2)):
        rec = jnp.where(lane_i == k, val, rec)
    route_ref[...] = rec


N_MIX_REFS = 15


def _mix_out_kernel(*refs, alpha, with_router, n_cast, n_prompt_blocks, blocks_per_seq,
                    d_attn, d_pool, d_sg):
    mix_refs = refs[:N_MIX_REFS - 1]
    ms_ref, x_ref, w_ref, g_ref, b_ref = refs[N_MIX_REFS - 1:N_MIX_REFS + 4]
    rest = list(refs[N_MIX_REFS + 4:])
    if with_router:
        rwh_ref, rwl_ref, rb_ref = rest[:3]
        rest = rest[3:]
    cast_src, rest = rest[:n_cast], rest[n_cast:]
    if with_router:
        x1_ref, x1s_ref, route_ref, cnt_ref = rest[:4]
        rest = rest[4:]
    else:
        x1_ref, x1b_ref = rest[:2]
        rest = rest[2:]
    cast_dst, rest = rest[:n_cast], rest[n_cast:]
    if with_router:
        prev_ref, next_ref, carry_ref = rest
    else:
        prev_ref, next_ref = rest
    s = pl.program_id(0)
    _run_casts(cast_src, cast_dst)

    @pl.when(s == 0)
    def _():
        prev_ref[...] = jnp.zeros_like(prev_ref)
        if with_router:
            carry_ref[...] = jnp.zeros_like(carry_ref)

    def project():
        y = alpha * x_ref[...] + jnp.dot(prev_ref[...], w_ref[...], preferred_element_type=F32)
        x1 = _layer_norm(y, g_ref[...], b_ref[...])
        x1_ref[...] = x1
        if with_router:
            _store_slabs(x1s_ref, x1)
            counted = jnp.where(s > 0, 1.0, 0.0)
            _route(x1, counted, rwh_ref, rwl_ref, rb_ref, route_ref, cnt_ref, carry_ref)
        else:
            x1b_ref[...] = x1.astype(BF16)

    @pl.when(s < n_prompt_blocks)
    def _():
        project()
        _prompt_mixers(lax.rem(s, blocks_per_seq), *mix_refs, next_ref,
                       d_attn=d_attn, d_pool=d_pool, d_sg=d_sg)

    @pl.when(s >= n_prompt_blocks)
    def _():
        project()
        next_ref[...] = ms_ref[...]

    prev_ref[...] = next_ref[...]


def _mix_out(sinks, q, k, v, xp, u, vn, pool_w, pool_scale, sg_w, sg_bias, mixed_s, x, w, g, b,
             router=None, cast_jobs=(), *, layer, n_prompt, seq, alpha):
    r, d = x.shape
    d_attn, d_kv, d_pool, d_sg = q.shape[1], k.shape[1], xp.shape[1], u.shape[1]
    d_mix = d_attn + d_pool + d_sg
    assert n_prompt % BLK == 0 and (r - n_prompt) % BLK == 0 and mixed_s.shape[0] == r - n_prompt
    n_p = n_prompt // BLK
    n_s = (r - n_prompt) // BLK
    hist = 2 * SUBLANES
    pblk = lambda s: jnp.minimum(s, n_p - 1)
    cur = lambda s: (pblk(s), 0)
    prev = lambda s: (jnp.maximum(pblk(s) - 1, 0), 0)
    tail = lambda s: (jnp.maximum(pblk(s) * (BLK // hist) - 1, 0), 0)
    samp = lambda s: (jnp.clip(s - n_p, 0, n_s - 1), 0)
    row = lambda s: (jnp.maximum(s - 1, 0), 0)
    const2 = lambda s: (0, 0)
    const3 = lambda s: (0, 0, 0)
    in_specs = [
        pl.BlockSpec(memory_space=pltpu.SMEM),
        pl.BlockSpec((BLK, d_attn), cur),
        pl.BlockSpec((BLK, d_kv), cur),
        pl.BlockSpec((BLK, d_kv), prev),
        pl.BlockSpec((BLK, d_kv), cur),
        pl.BlockSpec((BLK, d_kv), prev),
        pl.BlockSpec((BLK, d_pool), cur),
        pl.BlockSpec((hist, d_pool), tail),
        pl.BlockSpec((BLK, d_sg), cur),
        pl.BlockSpec((BLK, d_sg), cur),
        pl.BlockSpec(pool_w.shape, const3),
        pl.BlockSpec((1, d_pool), const2),
        pl.BlockSpec(sg_w.shape, const3),
        pl.BlockSpec((BLK, d_sg), const2),
        pl.BlockSpec((BLK, d_mix), samp),
        pl.BlockSpec((BLK, d), row),
        pl.BlockSpec((None,) + w.shape[1:], lambda s: (layer, 0, 0), pipeline_mode=pl.Buffered(1)),
        pl.BlockSpec((1, d), const2),
        pl.BlockSpec((1, d), const2),
    ]
    args = [sinks, q, k, k, v, v, xp, xp, u, vn, pool_w, pool_scale, sg_w, sg_bias, mixed_s, x, w, g, b]
    assert len(args) == N_MIX_REFS + 4
    scratch = [pltpu.VMEM((BLK, d_mix), BF16), pltpu.VMEM((BLK, d_mix), BF16)]
    if router is None:
        outs = [jax.ShapeDtypeStruct((r, d), F32), jax.ShapeDtypeStruct((r, d), BF16)]
        out_specs = [pl.BlockSpec((BLK, d), row), pl.BlockSpec((BLK, d), row)]
    else:
        in_specs += [pl.BlockSpec((d, LANES), const2), pl.BlockSpec((d, LANES), const2),
                     pl.BlockSpec((1, LANES), const2)]
        args += list(router)
        outs = [jax.ShapeDtypeStruct((r, d), F32),
                jax.ShapeDtypeStruct((r * SLAB_PITCH, LANES), F32),
                jax.ShapeDtypeStruct((r, LANES), F32),
                jax.ShapeDtypeStruct((1, LANES), F32)]
        out_specs = [pl.BlockSpec((BLK, d), row),
                     pl.BlockSpec((BLK * SLAB_PITCH, LANES), row),
                     pl.BlockSpec((BLK, LANES), row),
                     pl.BlockSpec((1, LANES), const2)]
        scratch.append(pltpu.VMEM((1, LANES), F32))
    n_steps = r // BLK + 1
    c_in, c_shape, c_out, c_args = _cast_operands(cast_jobs, n_steps, lambda s: s)
    n_main = len(outs)
    res = pl.pallas_call(
        functools.partial(_mix_out_kernel, alpha=alpha, with_router=router is not None,
                          n_cast=len(cast_jobs), n_prompt_blocks=n_p, blocks_per_seq=seq // BLK,
                          d_attn=d_attn, d_pool=d_pool, d_sg=d_sg),
        out_shape=tuple(outs + c_shape),
        grid=(n_steps,),
        in_specs=in_specs + c_in,
        out_specs=tuple(out_specs + c_out),
        scratch_shapes=scratch,
        compiler_params=_cparams(1),
        name="mix_out_router" if router is not None else "mix_out",
    )(*args, *c_args)
    return res[:n_main], res[n_main:]


def _swiglu(xb, wg_ref, wu_ref):
    a = jnp.dot(xb, wg_ref[...], preferred_element_type=F32)
    c = jnp.dot(xb, wu_ref[...], preferred_element_type=F32)
    return (a * (1.0 / (1.0 + jnp.exp(-a))) * c).astype(BF16)


def _ffn_kernel(*refs, alpha, n_cast, with_zeros):
    xb_ref, x1_ref, wg_ref, wu_ref, wd_ref, g_ref, b_ref = refs[:7]
    cast_src = refs[7:7 + n_cast]
    o_ref = refs[7 + n_cast]
    cast_dst = refs[8 + n_cast:8 + 2 * n_cast]
    f = pl.program_id(1)
    if with_zeros:
        zero_ref = refs[8 + 2 * n_cast]
        zero_ref[...] = jnp.zeros_like(zero_ref)

    @pl.when(f == 0)
    def _():
        o_ref[...] = alpha * x1_ref[...]

    o_ref[...] += jnp.dot(_swiglu(xb_ref[...], wg_ref, wu_ref), wd_ref[...],
                          preferred_element_type=F32)

    @pl.when(f == pl.num_programs(1) - 1)
    def _():
        o_ref[...] = _layer_norm(o_ref[...], g_ref[...], b_ref[...])

    _run_casts(cast_src, cast_dst)


def _run_casts(src_refs, dst_refs):
    for src, dst in zip(src_refs, dst_refs):
        dst[...] = src[...].astype(BF16)


def _chunk_dst_map(*ids, step_of, last):
    return (jnp.minimum(step_of(*ids), last), 0)


def _chunk_src_map(*ids, step_of, layer, last):
    return (layer,) + _chunk_dst_map(*ids, step_of=step_of, last=last)


def _chunk_rows(rows, steps):
    for t in range(2 * SUBLANES, rows + 1, 2 * SUBLANES):
        if rows % t == 0 and rows // t <= steps:
            return t
    return rows


def _cast_operands(jobs, steps, step_of):
    in_specs, out_shape, out_specs, args = [], [], [], []
    for arr, li in jobs:
        _, rows, cols = arr.shape
        cr = _chunk_rows(rows, steps)
        last = rows // cr - 1
        in_specs.append(pl.BlockSpec(
            (None, cr, cols), functools.partial(_chunk_src_map, step_of=step_of, layer=li, last=last)))
        out_shape.append(jax.ShapeDtypeStruct((rows, cols), BF16))
        out_specs.append(pl.BlockSpec(
            (cr, cols), functools.partial(_chunk_dst_map, step_of=step_of, last=last)))
        args.append(arr)
    return in_specs, out_shape, out_specs, args


def _ffn(xb, x1, wg, wu, wd, g, b, cast_jobs=(), zero_rows=0, *, layer, tm, tf, alpha):
    r, d = x1.shape
    ff = wg.shape[2]
    nf = ff // tf
    steps = (r // tm) * nf
    step_of = lambda i, f: i * nf + f
    row = lambda i, f: (i, 0)
    const = lambda i, f: (0, 0)
    in_specs = [
        pl.BlockSpec((tm, d), row),
        pl.BlockSpec((tm, d), row),
        pl.BlockSpec((None, d, tf), lambda i, f: (layer, 0, f)),
        pl.BlockSpec((None, d, tf), lambda i, f: (layer, 0, f)),
        pl.BlockSpec((None, tf, d), lambda i, f: (layer, f, 0)),
        pl.BlockSpec((1, d), const),
        pl.BlockSpec((1, d), const),
    ]
    c_in, c_shape, c_out, c_args = _cast_operands(cast_jobs, steps, step_of)
    out_shape = [jax.ShapeDtypeStruct((r, d), F32)] + c_shape
    out_specs = [pl.BlockSpec((tm, d), row)] + c_out
    if zero_rows:
        zr = _chunk_rows(zero_rows, steps)
        out_shape.append(jax.ShapeDtypeStruct((zero_rows, LANES), F32))
        out_specs.append(pl.BlockSpec(
            (zr, LANES), functools.partial(_chunk_dst_map, step_of=step_of, last=zero_rows // zr - 1)))
    outs = pl.pallas_call(
        functools.partial(_ffn_kernel, alpha=alpha, n_cast=len(cast_jobs), with_zeros=bool(zero_rows)),
        out_shape=tuple(out_shape),
        grid=(r // tm, nf),
        in_specs=in_specs + c_in,
        out_specs=tuple(out_specs),
        compiler_params=_cparams(2),
        name="ffn",
    )(xb, x1, wg, wu, wd, g, b, *c_args)
    n = len(cast_jobs)
    return outs[0], outs[1:1 + n], (outs[1 + n] if zero_rows else None)


MOE_BLOCK = 512
DMA_RING = 128
DMA_UNROLL = 8


def _ring_depth(n):
    return 1 << (min(DMA_RING, n).bit_length() - 1)


def _routing_tables(route, counts, *, n_work):
    e1 = route[:, ROUTE_E1].astype(jnp.int32)
    e2 = route[:, ROUTE_E2].astype(jnp.int32)
    r1 = route[:, ROUTE_R1].astype(jnp.int32)
    r2 = route[:, ROUTE_R2].astype(jnp.int32)
    cnt = counts[0, :N_EXPERTS].astype(jnp.int32)
    nblk = (cnt + MOE_BLOCK - 1) // MOE_BLOCK
    blk_end = jnp.cumsum(nblk)
    start = (blk_end - nblk) * MOE_BLOCK
    slot1 = jnp.take(start, e1) + r1
    slot2 = jnp.take(start, e2) + r2
    n_active = blk_end[-1]
    w = jnp.arange(n_work, dtype=jnp.int32)
    blk = jnp.minimum(w, n_active - 1)
    work_exp = jnp.sum(blk[:, None] >= blk_end[None, :], axis=1).astype(jnp.int32)
    first_blk = jnp.take(blk_end - nblk, work_exp)
    work_rows = jnp.clip(jnp.take(cnt, work_exp) - (w - first_blk) * MOE_BLOCK, 0, MOE_BLOCK)
    work_rows = jnp.where(w < n_active, work_rows, 0).astype(jnp.int32)
    return slot1, slot2, work_exp, work_rows


def _ring_copies(n, ring, make_copies):
    def start(i):
        for stream, c in enumerate(make_copies(i)):
            c.start(priority=stream % 2)

    def wait(i):
        for c in make_copies(i):
            c.wait()

    def fill(i, carry):
        start(i)
        return carry

    def steady(i, carry):
        wait(i - ring)
        start(i)
        return carry

    def drain(i, carry):
        wait(i)
        return carry

    lax.fori_loop(0, ring, fill, 0, unroll=DMA_UNROLL)
    lax.fori_loop(ring, n, steady, 0, unroll=DMA_UNROLL)
    lax.fori_loop(n - ring, n, drain, 0, unroll=DMA_UNROLL)


SLAB_ROWS = 16


def _slab(ref, token):
    return ref.at[pl.ds(pl.multiple_of(token * SLAB_PITCH, SUBLANES), SLAB_ROWS), :]


def _dispatch_kernel(s1_ref, s2_ref, x1s_ref, xs_init, xs_hbm, sem):
    del xs_init
    tm = x1s_ref.shape[0] // SLAB_PITCH
    base = pl.program_id(0) * tm
    ring = sem.shape[1]

    def copies(r):
        k = r & (ring - 1)
        src = _slab(x1s_ref, r)
        return (pltpu.make_async_copy(src, _slab(xs_hbm, s1_ref[base + r]), sem.at[0, k]),
                pltpu.make_async_copy(src, _slab(xs_hbm, s2_ref[base + r]), sem.at[1, k]))

    _ring_copies(tm, ring, copies)


def _moe_dispatch(slot1, slot2, x1s, xs_init, *, tm):
    r = x1s.shape[0] // SLAB_PITCH
    return pl.pallas_call(
        _dispatch_kernel,
        out_shape=jax.ShapeDtypeStruct(xs_init.shape, F32),
        grid_spec=pltpu.PrefetchScalarGridSpec(
            num_scalar_prefetch=2,
            grid=(r // tm,),
            in_specs=[pl.BlockSpec((tm * SLAB_PITCH, LANES), lambda i, s1, s2: (i, 0)),
                      pl.BlockSpec(memory_space=pl.ANY)],
            out_specs=pl.BlockSpec(memory_space=pl.ANY),
            scratch_shapes=[pltpu.SemaphoreType.DMA((2, _ring_depth(tm)))],
        ),
        input_output_aliases={3: 0},
        compiler_params=_cparams(1),
        name="moe_dispatch",
    )(slot1, slot2, x1s, xs_init)


def _moe_ffn_kernel(we_ref, wr_ref, xs_ref, wg_ref, wu_ref, wd_ref, ys_ref, xb_ref, acc_ref):
    del we_ref
    w = pl.program_id(0)
    f = pl.program_id(1)
    n, d = acc_ref.shape
    active = wr_ref[w] > 0

    def partial_out():
        return jnp.dot(_swiglu(xb_ref[...], wg_ref, wu_ref), wd_ref[...],
                       preferred_element_type=F32)

    @pl.when(active & (f == 0))
    def _():
        for j in range(d // LANES):
            xb_ref[:, j * LANES:(j + 1) * LANES] = _load_slab_chunk(xs_ref, j, n).astype(BF16)
        acc_ref[...] = partial_out()

    @pl.when(active & (f > 0))
    def _():
        acc_ref[...] += partial_out()

    @pl.when(jnp.logical_not(active) & (f == 0))
    def _():
        acc_ref[...] = jnp.zeros_like(acc_ref)

    @pl.when(f == pl.num_programs(1) - 1)
    def _():
        _store_slabs(ys_ref, acc_ref[...])


def _moe_ffn(work_exp, work_rows, xs, wg, wu, wd, *, layer, tf):
    d = wg.shape[2]
    ff = wg.shape[3]
    nf = ff // tf
    rows = lambda w, f, we, wr: (w, 0)
    chunk = lambda w, f, wr: jnp.where(wr[w] > 0, f, nf - 1)
    return pl.pallas_call(
        _moe_ffn_kernel,
        out_shape=jax.ShapeDtypeStruct(xs.shape, F32),
        grid_spec=pltpu.PrefetchScalarGridSpec(
            num_scalar_prefetch=2,
            grid=(xs.shape[0] // (MOE_BLOCK * SLAB_PITCH), nf),
            in_specs=[
                pl.BlockSpec((MOE_BLOCK * SLAB_PITCH, LANES), rows),
                pl.BlockSpec((None, None, d, tf),
                             lambda w, f, we, wr: (layer, we[w], 0, chunk(w, f, wr))),
                pl.BlockSpec((None, None, d, tf),
                             lambda w, f, we, wr: (layer, we[w], 0, chunk(w, f, wr))),
                pl.BlockSpec((None, None, tf, d),
                             lambda w, f, we, wr: (layer, we[w], chunk(w, f, wr), 0)),
            ],
            out_specs=pl.BlockSpec((MOE_BLOCK * SLAB_PITCH, LANES), rows),
            scratch_shapes=[pltpu.VMEM((MOE_BLOCK, d), BF16), pltpu.VMEM((MOE_BLOCK, d), F32)],
        ),
        compiler_params=_cparams(2),
        name="moe_ffn",
    )(work_exp, work_rows, xs, wg, wu, wd)


def _combine_kernel(s1_ref, s2_ref, ys_hbm, x1_ref, route_ref, g_ref, b_ref, o_ref,
                    y1_ref, y2_ref, sem, *, alpha):
    tm, d = o_ref.shape
    base = pl.program_id(0) * tm
    ring = sem.shape[1]

    def copies(r):
        k = r & (ring - 1)
        return (pltpu.make_async_copy(_slab(ys_hbm, s1_ref[base + r]), _slab(y1_ref, r), sem.at[0, k]),
                pltpu.make_async_copy(_slab(ys_hbm, s2_ref[base + r]), _slab(y2_ref, r), sem.at[1, k]))

    _ring_copies(tm, ring, copies)
    route = route_ref[...]
    g1 = _lane_pick(route, ROUTE_G1)
    g2 = _lane_pick(route, ROUTE_G2)
    for j in range(d // LANES):
        sl = slice(j * LANES, (j + 1) * LANES)
        o_ref[:, sl] = (alpha * x1_ref[:, sl] + g1 * _load_slab_chunk(y1_ref, j, tm)
                        + g2 * _load_slab_chunk(y2_ref, j, tm))
    o_ref[...] = _layer_norm(o_ref[...], g_ref[...], b_ref[...])


def _moe_combine(slot1, slot2, ys, x1, route, g, b, *, tm, alpha):
    r, d = x1.shape
    row = lambda i, s1, s2: (i, 0)
    const = lambda i, s1, s2: (0, 0)
    return pl.pallas_call(
        functools.partial(_combine_kernel, alpha=alpha),
        out_shape=jax.ShapeDtypeStruct((r, d), F32),
        grid_spec=pltpu.PrefetchScalarGridSpec(
            num_scalar_prefetch=2,
            grid=(r // tm,),
            in_specs=[
                pl.BlockSpec(memory_space=pl.ANY),
                pl.BlockSpec((tm, d), row),
                pl.BlockSpec((tm, LANES), row),
                pl.BlockSpec((1, d), const),
                pl.BlockSpec((1, d), const),
            ],
            out_specs=pl.BlockSpec((tm, d), row),
            scratch_shapes=[pltpu.VMEM((tm * SLAB_PITCH, LANES), F32),
                            pltpu.VMEM((tm * SLAB_PITCH, LANES), F32),
                            pltpu.SemaphoreType.DMA((2, _ring_depth(tm)))],
        ),
        compiler_params=_cparams(1),
        name="moe_combine",
    )(slot1, slot2, ys, x1, route, g, b)


def _moe_work_items(r):
    return (2 * r + MOE_BLOCK - 1) // MOE_BLOCK + N_EXPERTS


def _moe(x1, x1s, route, counts, wg, wu, wd, g, b, xs_buf=None, *, layer, tm, tf, alpha):
    r = x1.shape[0]
    n_work = _moe_work_items(r)
    slot1, slot2, work_exp, work_rows = _routing_tables(route, counts, n_work=n_work)
    if xs_buf is None:
        xs_buf = jnp.zeros((n_work * MOE_BLOCK * SLAB_PITCH, LANES), F32)
    xs = _moe_dispatch(slot1, slot2, x1s, xs_buf, tm=tm)
    ys = _moe_ffn(work_exp, work_rows, xs, wg, wu, wd, layer=layer, tf=tf)
    return _moe_combine(slot1, slot2, ys, x1, route, g, b, tm=tm, alpha=alpha), xs


def _rope_tables(pos):
    half = HEAD_DIM // 2
    inv = ROPE_THETA ** (-jnp.arange(half, dtype=F32) / half)
    ang = pos.astype(F32)[:, None] * inv[None, :]
    cos = jnp.cos(ang)
    sin = jnp.sin(ang)
    cos2 = jnp.concatenate([cos, cos, cos, cos], axis=1)
    sin2 = jnp.concatenate([-sin, sin, -sin, sin], axis=1)
    return cos2, sin2


def _row_tile(r, cap):
    best = 16
    for t in range(16, cap + 1, 16):
        if r % t == 0:
            best = t
    return best


def kernel(x_prompt, x_sample, cache_k, cache_v, state_pool, w_in, w_out, attn_sinks, pool_w,
           pool_scale, sg_w, sg_b, sg_norm_g, sg_norm_b, ln1_g, ln1_b, ln2_g, ln2_b,
           ffn_w_gate, ffn_w_up, ffn_w_down, router_w, router_b, moe_w_gate, moe_w_up, moe_w_down):
    bp, tp, d = x_prompt.shape
    bs, ts, _ = x_sample.shape
    depth = w_in.shape[0]
    d_kv = cache_k.shape[3] * cache_k.shape[4]
    d_pool = pool_scale.shape[1]
    d_sg = sg_norm_g.shape[1]
    d_attn = w_in.shape[2] - 2 * d_kv - d_pool - 2 * d_sg
    win_buf = cache_k.shape[2]
    assert win_buf == WINDOW == BLK and tp % BLK == 0 and tp >= BLK and ts <= SAMPLE_PAD
    assert bs % SEQ_PER_STEP == 0 and state_pool.shape[2] == POOL_STATE
    assert cache_k.shape[3] == N_KV_HEADS and cache_k.shape[4] == HEAD_DIM
    assert d == SLAB_ROWS * LANES
    alpha = (2 * depth) ** 0.25

    r_p = bp * tp
    r_s = bs * SAMPLE_PAD
    r = r_p + r_s
    tm = _row_tile(r, 768)
    tm_ffn = _row_tile(r, 704)
    tm_comb = _row_tile(r, 384)

    xs_pad = jnp.pad(x_sample, ((0, 0), (0, SAMPLE_PAD - ts), (0, 0)))
    x = jnp.concatenate([x_prompt.reshape(r_p, d), xs_pad.reshape(r_s, d)], axis=0)

    pos = jnp.concatenate([
        jnp.tile(jnp.arange(tp, dtype=jnp.int32), bp),
        jnp.tile(PAST_LEN + jnp.arange(SAMPLE_PAD, dtype=jnp.int32), bs)])
    cos, sin = _rope_tables(pos)

    in_w = {0: w_in[0].astype(BF16)}
    out_w = {0: w_out[0].astype(BF16)}
    pool_w_b = pool_w.astype(BF16)
    rw_pad = jnp.pad(router_w, ((0, 0), (0, 0), (0, LANES - router_w.shape[2])))
    rw_hi = rw_pad.astype(BF16)
    rw_lo = (rw_pad - rw_hi.astype(F32)).astype(BF16)
    rb_pad = jnp.pad(router_b, ((0, 0), (0, LANES - router_b.shape[1])))[:, None, :]
    sg_bias = jnp.repeat(jnp.swapaxes(sg_b, 1, 2), d_sg // N_SG_GROUPS, axis=2)
    state_pad = jnp.pad(state_pool, ((0, 0), (0, 0), (1, 0), (0, 0)))
    state_rows = state_pad.reshape((depth * bs,) + state_pad.shape[2:])
    ck_rows = cache_k.reshape(depth * bs * win_buf, d_kv)
    cv_rows = cache_v.reshape(depth * bs * win_buf, d_kv)

    k_l, v_l, xp_l, vn_l = [], [], [], []
    xs_buf = None
    early_gate = {}
    for l in range(depth):
        q, k, v, xp, u, vn = _inproj(
            x, in_w[l][None], cos, sin, sg_norm_g[l][None], sg_norm_b[l][None],
            layer=0, tm=tm, d_attn=d_attn, d_kv=d_kv, d_pool=d_pool, d_sg=d_sg)
        mixed_s = _mix_sample(attn_sinks[l], q, k, v, xp, u, vn, ck_rows, cv_rows, state_rows,
                              pool_w_b[l], pool_scale[l][None], sg_w[l], sg_bias[l],
                              layer=l, r_prompt=r_p, n_seq=bs)
        i = l // 2
        is_dense = l % 2 == 0
        jobs = []
        if l + 1 < depth:
            jobs += [(w_in, l + 1), (w_out, l + 1)]
        own = (ffn_w_gate, ffn_w_up, ffn_w_down) if is_dense else (moe_w_gate, moe_w_up, moe_w_down)
        rows_of = lambda w: w.reshape(w.shape[0], -1, w.shape[-1])
        early = early_gate.pop(i, None) if not is_dense else None
        jobs += [(rows_of(w), i) for w in (own[1:] if early is not None else own)]
        lends_hand = is_dense and l + 1 < depth
        if lends_hand:
            jobs.append((rows_of(moe_w_gate), (l + 1) // 2))
        main, casted = _mix_out(
            attn_sinks[l], q, k, v, xp, u, vn, pool_w_b[l], pool_scale[l][None], sg_w[l],
            sg_bias[l], mixed_s, x, out_w[l][None], ln1_g[l][None], ln1_b[l][None],
            None if is_dense else (rw_hi[i], rw_lo[i], rb_pad[i]), tuple(jobs),
            layer=0, n_prompt=r_p, seq=tp, alpha=alpha)
        casted = list(casted)
        if l + 1 < depth:
            in_w[l + 1], out_w[l + 1] = casted[:2]
            casted = casted[2:]
        if lends_hand:
            early_gate[(l + 1) // 2] = casted.pop()
        if early is not None:
            casted = [early] + casted
        if is_dense:
            x1, x1b = main
            ffn_g, ffn_u, ffn_d = (c[None] for c in casted)
            zero_rows = 0
            if l + 1 < depth and xs_buf is None:
                zero_rows = _moe_work_items(r) * MOE_BLOCK * SLAB_PITCH
            x, _, zeros = _ffn(x1b, x1, ffn_g, ffn_u, ffn_d, ln2_g[l][None], ln2_b[l][None],
                               (), zero_rows, layer=0, tm=tm_ffn, tf=512, alpha=alpha)
            if zero_rows:
                xs_buf = zeros
        else:
            x1, x1s, route, counts = main
            moe_g, moe_u, moe_d = (c.reshape((1,) + w.shape[1:]) for c, w in zip(casted, own))
            x, xs_buf = _moe(x1, x1s, route, counts, moe_g, moe_u, moe_d, ln2_g[l][None],
                             ln2_b[l][None], xs_buf, layer=0, tm=tm_comb, tf=512, alpha=alpha)

        tail = max(WINDOW, POOL_STATE)
        keep = lambda a: jnp.concatenate(
            [a[:r_p].reshape(bp, tp, -1)[:, -tail:].reshape(bp * tail, -1), a[r_p:]], axis=0)
        k_l.append(keep(k))
        v_l.append(keep(v))
        xp_l.append(keep(xp))
        vn_l.append(vn[r_p:])

    kh = (N_KV_HEADS, HEAD_DIM)
    n_tail = bp * max(WINDOW, POOL_STATE)
    ks, vs, xps, vns = (jnp.stack(a) for a in (k_l, v_l, xp_l, vn_l))
    prompt_tail = lambda a, n: a[:, :n_tail].reshape(depth, bp, -1, a.shape[-1])[:, :, -n:]
    sample_new = lambda a: a[:, -r_s:].reshape(depth, bs, SAMPLE_PAD, a.shape[-1])[:, :, :ts]
    new_k_p = prompt_tail(ks, WINDOW).reshape(depth, bp, WINDOW, *kh)
    new_v_p = prompt_tail(vs, WINDOW).reshape(depth, bp, WINDOW, *kh)
    new_pool_p = prompt_tail(xps, POOL_STATE)
    new_k_s = jnp.concatenate(
        [cache_k, sample_new(ks).reshape(depth, bs, ts, *kh)], axis=2)[:, :, -win_buf:]
    new_v_s = jnp.concatenate(
        [cache_v, sample_new(vs).reshape(depth, bs, ts, *kh)], axis=2)[:, :, -win_buf:]
    new_pool_s = jnp.concatenate([state_pool, sample_new(xps)], axis=2)[:, :, -POOL_STATE:]
    new_sg_s = vns.reshape(depth, bs, SAMPLE_PAD, d_sg)[:, :, :ts]

    y_prompt = x[:r_p].reshape(bp, tp, d)
    y_sample = x[r_p:].reshape(bs, SAMPLE_PAD, d)[:, :ts]
    return (y_prompt, y_sample, new_k_p, new_v_p, new_pool_p, new_k_s, new_v_s, new_pool_s, new_sg_s)
```
